```python
import jax, jax.numpy as jnp
from jax import lax
import numpy as np

D_MODEL = 1024
BATCH = 2
SEQ = 8192
DEPTH = 4

HEAD_DIM = 64
BLOCK = 128
SB_HEADS = 4
SB_WIDTH = SB_HEADS * HEAD_DIM
RW_HEADS = 4
RW_WIDTH = RW_HEADS * HEAD_DIM
DECAY_LORA = 64
AAA_LORA = 64
GATE_LORA = 160
RW_GN_EPS = 64e-5
SW_HEADS = 8
SW_KV_HEADS = 2
SW_WIDTH = SW_HEADS * HEAD_DIM
SW_KV_WIDTH = SW_KV_HEADS * HEAD_DIM
WINDOW = 128
MIX_WIDTH = SB_WIDTH + RW_WIDTH + SW_WIDTH
RW_IN_SIZES = (RW_WIDTH, RW_WIDTH, RW_WIDTH, DECAY_LORA, AAA_LORA, GATE_LORA)
RW_IN_WIDTH = 3 * RW_WIDTH + DECAY_LORA + AAA_LORA + GATE_LORA
IN_SIZES = (SB_WIDTH, SB_WIDTH, SB_WIDTH, RW_IN_WIDTH, SW_WIDTH, SW_KV_WIDTH, SW_KV_WIDTH)
IN_WIDTH = 3 * SB_WIDTH + RW_IN_WIDTH + SW_WIDTH + 2 * SW_KV_WIDTH
D_FF = 2816
NORM_EPS = 1e-6

kernel_name = "hybrid_sb_rwkv7_swa_macaron"


def _split_cols(t, sizes):
    idx, acc = [], 0
    for s in sizes[:-1]:
        acc += s
        idx.append(acc)
    return jnp.split(t, idx, axis=-1)


def rmsnorm(x, g):
    xf = x.astype(jnp.float32)
    y = xf * lax.rsqrt(jnp.mean(xf * xf, axis=-1, keepdims=True) + NORM_EPS)
    return (y * g.astype(jnp.float32)).astype(x.dtype)


def swiglu(h, w_in, w_out):
    gate, up = jnp.split(h @ w_in, 2, axis=-1)
    return (jax.nn.silu(gate) * up) @ w_out


def stick_breaking_attention(q, k, v):
    B, S, H, d = q.shape
    nb = S // BLOCK
    scale = HEAD_DIM ** -0.5
    kf = k.astype(jnp.float32)
    vf = v.astype(jnp.float32)
    qb = q.astype(jnp.float32).reshape(B, nb, BLOCK, H, d).transpose(1, 0, 2, 3, 4)
    key_pos = jnp.arange(S)

    def one_block(args):
        q_blk, i = args
        t = i * BLOCK + jnp.arange(BLOCK)
        causal = key_pos[None, :] < t[:, None]
        z = jnp.einsum('bqhd,bshd->bhqs', q_blk, kf) * scale
        log_one_minus = jnp.where(causal, jax.nn.log_sigmoid(-z), 0.0)
        tail = lax.cumsum(log_one_minus, axis=3, reverse=True) - log_one_minus
        a = jnp.where(causal, jnp.exp(jax.nn.log_sigmoid(z) + tail), 0.0)
        return jnp.einsum('bhqs,bshd->bqhd', a, vf)

    out = lax.map(one_block, (qb, jnp.arange(nb)))
    return out.transpose(1, 0, 2, 3, 4).reshape(B, S, H * d).astype(q.dtype)


def rwkv7_time_mix(r, k, v, xw, xa, xg, w0, w_up, a0, a_up, g_up,
                   k_k, k_a, r_k, gn_g, gn_b):
    B, S, C = r.shape
    f32 = jnp.float32
    out_dtype = r.dtype
    log_w = -jax.nn.softplus(-(w0 + jnp.tanh(xw) @ w_up).astype(f32)) - 0.5
    decay = jnp.exp(-jnp.exp(log_w))
    a = jax.nn.sigmoid((a0 + xa @ a_up).astype(f32))
    g = (jax.nn.sigmoid(xg) @ g_up).astype(f32)
    r, k, v = r.astype(f32), k.astype(f32), v.astype(f32)
    heads = lambda t: t.reshape(B, S, RW_HEADS, HEAD_DIM)
    kk = heads(k * k_k.astype(f32))
    kk = kk * lax.rsqrt(jnp.maximum(jnp.sum(kk * kk, axis=-1, keepdims=True), 1e-24))
    k = k * (1.0 + (a - 1.0) * k_a.astype(f32))
    rh, kh, vh, wh, ah = heads(r), heads(k), heads(v), heads(decay), heads(a)

    def step(state, inp):
        r_t, w_t, k_t, v_t, kk_t, b_t = inp
        sa = jnp.einsum('bhij,bhj->bhi', state, -kk_t)
        state = (state * w_t[:, :, None, :]
                 + sa[..., None] * b_t[:, :, None, :]
                 + v_t[..., None] * k_t[:, :, None, :])
        y = jnp.einsum('bhij,bhj->bhi', state, r_t)
        return state, y

    tm = lambda t: jnp.swapaxes(t, 0, 1)
    xs = (tm(rh), tm(wh), tm(kh), tm(vh), tm(kk), tm(kk * ah))
    s0 = jnp.zeros((B, RW_HEADS, HEAD_DIM, HEAD_DIM), f32)
    _, ys = lax.scan(step, s0, xs)
    y = tm(ys)
    mu = jnp.mean(y, axis=-1, keepdims=True)
    var = jnp.mean(jnp.square(y - mu), axis=-1, keepdims=True)
    yn = ((y - mu) * lax.rsqrt(var + RW_GN_EPS)).reshape(B, S, C)
    yn = yn * gn_g.astype(f32) + gn_b.astype(f32)
    bonus = jnp.sum(rh * kh * r_k.astype(f32), axis=-1, keepdims=True) * vh
    return ((yn + bonus.reshape(B, S, C)) * g).astype(out_dtype)


def sliding_window_gqa_sinks(q, k, v, sinks):
    B, S, Hq, d = q.shape
    hkv = k.shape[2]
    grp = Hq // hkv
    nb = S // BLOCK
    f32 = jnp.float32
    scale = HEAD_DIM ** -0.5
    slopes = (2.0 ** (-8.0 * (jnp.arange(Hq, dtype=f32) + 1.0) / Hq)).reshape(hkv, grp)
    qb = q.astype(f32).reshape(B, nb, BLOCK, hkv, grp, d)

    def band(t):
        tb = t.astype(f32).reshape(B, nb, BLOCK, hkv, d)
        prev = jnp.pad(tb, ((0, 0), (1, 0), (0, 0), (0, 0), (0, 0)))[:, :-1]
        return jnp.concatenate([prev, tb], axis=2)

    kw, vw = band(k), band(v)
    z = jnp.einsum('bnqhgd,bnshd->bnhgqs', qb, kw) * scale
    qi = jnp.arange(BLOCK)[:, None]
    si = jnp.arange(2 * BLOCK)[None, :]
    dist = (qi + BLOCK - si).astype(f32)
    in_win = (dist >= 0) & (dist < WINDOW)
    key_abs = jnp.arange(nb)[:, None] * BLOCK - BLOCK + jnp.arange(2 * BLOCK)[None, :]
    valid = in_win[None, :, :] & (key_abs >= 0)[:, None, :]
    z = z - slopes[:, :, None, None] * dist
    z = jnp.where(valid[None, :, None, None], z, -jnp.inf)
    sink = sinks.astype(f32).reshape(hkv, grp)[:, :, None, None]
    m = jnp.maximum(jnp.max(z, axis=-1, keepdims=True), sink)
    p = jnp.exp(z - m)
    p = p / (jnp.sum(p, axis=-1, keepdims=True) + jnp.exp(sink - m))
    out = jnp.einsum('bnhgqs,bnshd->bnqhgd', p, vw)
    return out.reshape(B, S, Hq * d).astype(q.dtype)


def hybrid_mixer(h, w_in, w_out, rw_mu, rw_w0, rw_w_up, rw_a0, rw_a_up, rw_g_up,
                 rw_k_k, rw_k_a, rw_r_k, rw_gn_g, rw_gn_b, sw_sinks):
    B, S, _ = h.shape
    p = h @ w_in
    sb_q, sb_k, sb_v, rw_all, sw_q, sw_k, sw_v = _split_cols(p, IN_SIZES)
    hd = lambda t, n: t.reshape(B, S, n, HEAD_DIM)
    sb_out = stick_breaking_attention(hd(sb_q, SB_HEADS), hd(sb_k, SB_HEADS), hd(sb_v, SB_HEADS))
    rw_prev = jnp.pad(rw_all, ((0, 0), (1, 0), (0, 0)))[:, :-1]
    rw_all = rw_all + (rw_prev - rw_all) * rw_mu
    r, k, v, xw, xa, xg = _split_cols(rw_all, RW_IN_SIZES)
    rw_out = rwkv7_time_mix(r, k, v, xw, xa, xg, rw_w0, rw_w_up, rw_a0, rw_a_up, rw_g_up,
                            rw_k_k, rw_k_a, rw_r_k, rw_gn_g, rw_gn_b)
    sw_out = sliding_window_gqa_sinks(hd(sw_q, SW_HEADS), hd(sw_k, SW_KV_HEADS),
                                      hd(sw_v, SW_KV_HEADS), sw_sinks)
    return jnp.concatenate([sb_out, rw_out, sw_out], axis=-1) @ w_out


def setup_inputs(seed: int = 0) -> dict:
    key = jax.random.key(seed)
    ks = iter(jax.random.split(key, 32))
    nrm = lambda shape, s: jax.random.normal(next(ks), shape, jnp.float32) * s
    L, D = DEPTH, D_MODEL
    gain = lambda shape: 1.0 + nrm(shape, 0.05)
    return {
        "x": jax.random.normal(next(ks), (BATCH, SEQ, D), jnp.float32),
        "ffn1_norm": gain((L, D)),
        "ffn1_w_in": nrm((L, D, 2 * D_FF), D ** -0.5),
        "ffn1_w_out": nrm((L, D_FF, D), D_FF ** -0.5),
        "mix_norm": gain((L, D)),
        "mix_w_in": nrm((L, D, IN_WIDTH), D ** -0.5),
        "mix_w_out": nrm((L, MIX_WIDTH, D), MIX_WIDTH ** -0.5),
        "rw_mu": jax.random.uniform(next(ks), (L, RW_IN_WIDTH), jnp.float32),
        "rw_w0": nrm((L, RW_WIDTH), 0.5),
        "rw_w_up": nrm((L, DECAY_LORA, RW_WIDTH), 0.5 * DECAY_LORA ** -0.5),
        "rw_a0": nrm((L, RW_WIDTH), 0.5),
        "rw_a_up": nrm((L, AAA_LORA, RW_WIDTH), 0.5 * AAA_LORA ** -0.5),
        "rw_g_up": nrm((L, GATE_LORA, RW_WIDTH), GATE_LORA ** -0.5),
        "rw_k_k": 0.85 + nrm((L, RW_WIDTH), 0.05),
        "rw_k_a": 1.0 + nrm((L, RW_WIDTH), 0.05),
        "rw_r_k": nrm((L, RW_HEADS, HEAD_DIM), 0.1),
        "rw_gn_g": gain((L, RW_WIDTH)),
        "rw_gn_b": nrm((L, RW_WIDTH), 0.02),
        "sw_sinks": nrm((L, SW_HEADS), 0.5),
        "ffn2_norm": gain((L, D)),
        "ffn2_w_in": nrm((L, D, 2 * D_FF), D ** -0.5),
        "ffn2_w_out": nrm((L, D_FF, D), D_FF ** -0.5),
        "final_norm": gain((D,)),
    }


def reference(x, ffn1_norm, ffn1_w_in, ffn1_w_out, mix_norm, mix_w_in, mix_w_out,
              rw_mu, rw_w0, rw_w_up, rw_a0, rw_a_up, rw_g_up, rw_k_k, rw_k_a, rw_r_k,
              rw_gn_g, rw_gn_b, sw_sinks, ffn2_norm, ffn2_w_in, ffn2_w_out, final_norm):
    for l in range(DEPTH):
        x = x + 0.5 * swiglu(rmsnorm(x, ffn1_norm[l]), ffn1_w_in[l], ffn1_w_out[l])
        x = x + hybrid_mixer(rmsnorm(x, mix_norm[l]), mix_w_in[l], mix_w_out[l],
                             rw_mu[l], rw_w0[l], rw_w_up[l], rw_a0[l], rw_a_up[l], rw_g_up[l],
                             rw_k_k[l], rw_k_a[l], rw_r_k[l], rw_gn_g[l], rw_gn_b[l], sw_sinks[l])
        x = x + 0.5 * swiglu(rmsnorm(x, ffn2_norm[l]), ffn2_w_in[l], ffn2_w_out[l])
    return rmsnorm(x, final_norm)
```

```python
import functools

import jax
import jax.numpy as jnp
from jax import lax
from jax.experimental import pallas as pl
from jax.experimental.pallas import tpu as pltpu

D_MODEL = 1024
DEPTH = 4
HEAD_DIM = 64
SB_HEADS = 4
SB_WIDTH = SB_HEADS * HEAD_DIM
RW_HEADS = 4
RW_WIDTH = RW_HEADS * HEAD_DIM
DECAY_LORA = 64
AAA_LORA = 64
GATE_LORA = 160
RW_GN_EPS = 64e-5
SW_HEADS = 8
SW_KV_HEADS = 2
SW_WIDTH = SW_HEADS * HEAD_DIM
SW_KV_WIDTH = SW_KV_HEADS * HEAD_DIM
WINDOW = 128
D_FF = 2816
NORM_EPS = 1e-6

LANES = 128
MXU_DIM = 256
RW_PAD_WIDTH = 3 * RW_WIDTH + 2 * LANES + 2 * LANES
SB_IN_WIDTH = 3 * SB_WIDTH
SW_IN_WIDTH = SW_WIDTH + 2 * SW_KV_WIDTH
IN_PAD_WIDTH = SB_IN_WIDTH + RW_PAD_WIDTH + SW_IN_WIDTH

ROW_TILE = 512
FF_TILE = MXU_DIM
SB_TILE = 256
SW_TILE = WINDOW
RW_ROWS = 256
RW_CHUNK = 64

VMEM_LIMIT_BYTES = 56 * 1024 * 1024

_MXU = jnp.bfloat16
_F32 = jnp.float32
_HI = lax.Precision.HIGHEST


def _dot(a, b, precision=None):
    return jnp.dot(a, b, preferred_element_type=_F32, precision=precision)


def _dot_nt(a, b, precision=None):
    return lax.dot_general(a, b, (((1,), (1,)), ((), ())),
                           preferred_element_type=_F32, precision=precision)


def _sigmoid(x):
    return 1.0 / (1.0 + jnp.exp(-x))


def _softplus(x):
    return jnp.maximum(x, 0.0) + jnp.log(1.0 + jnp.exp(-jnp.abs(x)))


def _params(*sem):
    return pltpu.CompilerParams(dimension_semantics=sem, vmem_limit_bytes=VMEM_LIMIT_BYTES)


def _const_spec(shape):
    nd = len(shape)
    return pl.BlockSpec(shape, lambda *_: (0,) * nd, pipeline_mode=pl.Buffered(1))


def _ffn_body(x_ref, g_ref, win_ref, wout_ref, fg_ref, o_ref, act_ref, *, final):
    x = x_ref[...]
    ms = jnp.mean(x * x, axis=-1, keepdims=True)
    hn = (x * lax.rsqrt(ms + NORM_EPS) * g_ref[...]).astype(_MXU)
    for c in range(D_FF // FF_TILE):
        lo, hi = c * FF_TILE, (c + 1) * FF_TILE
        gate = _dot(hn, win_ref[:, lo:hi])
        up = _dot(hn, win_ref[:, D_FF + lo:D_FF + hi])
        act_ref[:, lo:hi] = (gate * _sigmoid(gate) * up).astype(_MXU)
    y = x + 0.5 * _dot(act_ref[...], wout_ref[...])
    if final:
        ms = jnp.mean(y * y, axis=-1, keepdims=True)
        y = y * lax.rsqrt(ms + NORM_EPS) * fg_ref[...]
    o_ref[...] = y


def _ffn(x, g, w_in, w_out, final_g, final):
    m = x.shape[0]
    row = lambda i: (i, 0)
    return pl.pallas_call(
        functools.partial(_ffn_body, final=final),
        grid=(m // ROW_TILE,),
        in_specs=[
            pl.BlockSpec((ROW_TILE, D_MODEL), row),
            _const_spec((1, D_MODEL)),
            _const_spec((D_MODEL, 2 * D_FF)),
            _const_spec((D_FF, D_MODEL)),
            _const_spec((1, D_MODEL)),
        ],
        out_specs=pl.BlockSpec((ROW_TILE, D_MODEL), row),
        out_shape=jax.ShapeDtypeStruct((m, D_MODEL), _F32),
        scratch_shapes=[pltpu.VMEM((ROW_TILE, D_FF), _MXU)],
        compiler_params=_params("parallel"),
        name="ffn",
    )(x, g, w_in, w_out, final_g)


def _inproj_body(x_ref, g_ref, w_ref, sb_ref, rw_ref, sw_ref):
    x = x_ref[...]
    ms = jnp.mean(x * x, axis=-1, keepdims=True)
    hn = (x * lax.rsqrt(ms + NORM_EPS) * g_ref[...]).astype(_MXU)
    sb_ref[...] = _dot(hn, w_ref[:, :SB_IN_WIDTH]).astype(sb_ref.dtype)
    rw_ref[...] = _dot(hn, w_ref[:, SB_IN_WIDTH:SB_IN_WIDTH + RW_PAD_WIDTH])
    sw_ref[...] = _dot(hn, w_ref[:, SB_IN_WIDTH + RW_PAD_WIDTH:]).astype(sw_ref.dtype)


def _inproj(x, g, w):
    m = x.shape[0]
    row = lambda i: (i, 0)
    return pl.pallas_call(
        _inproj_body,
        grid=(m // ROW_TILE,),
        in_specs=[
            pl.BlockSpec((ROW_TILE, D_MODEL), row),
            _const_spec((1, D_MODEL)),
            _const_spec((D_MODEL, IN_PAD_WIDTH)),
        ],
        out_specs=[
            pl.BlockSpec((ROW_TILE, SB_IN_WIDTH), row),
            pl.BlockSpec((ROW_TILE, RW_PAD_WIDTH), row),
            pl.BlockSpec((ROW_TILE, SW_IN_WIDTH), row),
        ],
        out_shape=[
            jax.ShapeDtypeStruct((m, SB_IN_WIDTH), _MXU),
            jax.ShapeDtypeStruct((m, RW_PAD_WIDTH), _F32),
            jax.ShapeDtypeStruct((m, SW_IN_WIDTH), _MXU),
        ],
        compiler_params=_params("parallel"),
        name="inproj",
    )(x, g, w)


def _outproj_body(x_ref, sb_ref, rw_ref, sw_ref, wsb_ref, wrw_ref, wsw_ref, o_ref):
    y = _dot(sb_ref[...], wsb_ref[...])
    y = y + _dot(rw_ref[...], wrw_ref[...])
    y = y + _dot(sw_ref[...], wsw_ref[...])
    o_ref[...] = x_ref[...] + y


def _outproj(x, sb, rw, sw, w_sb, w_rw, w_sw):
    m = x.shape[0]
    row = lambda i: (i, 0)
    return pl.pallas_call(
        _outproj_body,
        grid=(m // ROW_TILE,),
        in_specs=[
            pl.BlockSpec((ROW_TILE, D_MODEL), row),
            pl.BlockSpec((ROW_TILE, SB_WIDTH), row),
            pl.BlockSpec((ROW_TILE, RW_WIDTH), row),
            pl.BlockSpec((ROW_TILE, SW_WIDTH), row),
            _const_spec((SB_WIDTH, D_MODEL)),
            _const_spec((RW_WIDTH, D_MODEL)),
            _const_spec((SW_WIDTH, D_MODEL)),
        ],
        out_specs=pl.BlockSpec((ROW_TILE, D_MODEL), row),
        out_shape=jax.ShapeDtypeStruct((m, D_MODEL), _F32),
        compiler_params=_params("parallel"),
        name="outproj",
    )(x, sb, rw, sw, w_sb, w_rw, w_sw)


def _sb_body(q_ref, kt_ref, v_ref, o_ref):
    t = SB_TILE
    i = pl.program_id(1)
    q = q_ref[0]
    lane_head = lax.broadcasted_iota(jnp.int32, (1, SB_WIDTH), 1) // HEAD_DIM
    row = lax.broadcasted_iota(jnp.int32, (t, t), 0)
    col = lax.broadcasted_iota(jnp.int32, (t, t), 1)
    later = jnp.where(row > col, 1.0, 0.0).astype(_MXU)
    causal = col < row
    scale = HEAD_DIM ** -0.5
    out = jnp.zeros((t, SB_WIDTH), _F32)
    for h in range(SB_HEADS):
        qh = jnp.where(lane_head == h, q, jnp.zeros_like(q))

        def block(jb, acc, carry, diagonal, qh=qh):
            z = _dot(qh, kt_ref[0, jb]) * scale
            lom = -_softplus(z)
            if diagonal:
                lom = jnp.where(causal, lom, 0.0)
            hi = lom.astype(_MXU)
            lo = (lom - hi.astype(_F32)).astype(_MXU)
            tail = _dot(hi, later) + _dot(lo, later) + carry
            a = jnp.exp(z + lom + tail)
            if diagonal:
                a = jnp.where(causal, a, 0.0)
            vb = v_ref[0, pl.ds(pl.multiple_of(jb * t, t), t), :]
            acc = acc + _dot(a.astype(_MXU), vb)
            carry = carry + jnp.sum(lom, axis=1, keepdims=True)
            return acc, carry

        acc, carry = block(i, jnp.zeros((t, SB_WIDTH), _F32), jnp.zeros((t, 1), _F32), True)
        acc, carry = lax.fori_loop(
            0, i, lambda s, c: block(i - 1 - s, c[0], c[1], False), (acc, carry))
        out = jnp.where(lane_head == h, acc, out)
    o_ref[0] = out.astype(o_ref.dtype)


def _sb_attention(sb, kt):
    b, s, _ = sb.shape
    t = SB_TILE
    return pl.pallas_call(
        _sb_body,
        grid=(b, s // t),
        in_specs=[
            pl.BlockSpec((1, t, SB_WIDTH), lambda bi, i: (bi, i, 0)),
            pl.BlockSpec((1, s // t, SB_WIDTH, t), lambda bi, i: (bi, 0, 0, 0)),
            pl.BlockSpec((1, s, SB_WIDTH), lambda bi, i: (bi, 0, 2)),
        ],
        out_specs=pl.BlockSpec((1, t, SB_WIDTH), lambda bi, i: (bi, i, 0)),
        out_shape=jax.ShapeDtypeStruct((b, s, SB_WIDTH), _MXU),
        compiler_params=_params("parallel", "arbitrary"),
        name="sb_attn",
    )(sb, kt, sb)


def _sw_body(sinks_ref, q_ref, kp_ref, kc_ref, vp_ref, vc_ref, o_ref):
    t = SW_TILE
    i = pl.program_id(1)
    kcat = jnp.concatenate([kp_ref[0], kc_ref[0]], axis=0)
    vcat = jnp.concatenate([vp_ref[0], vc_ref[0]], axis=0)
    qi = lax.broadcasted_iota(jnp.int32, (t, 2 * t), 0)
    si = lax.broadcasted_iota(jnp.int32, (t, 2 * t), 1)
    dist = qi + t - si
    valid = (dist >= 0) & (dist < WINDOW) & (i * t - t + si >= 0)
    distf = dist.astype(_F32)
    lane_kv = lax.broadcasted_iota(jnp.int32, (1, LANES), 1) // HEAD_DIM
    scale = HEAD_DIM ** -0.5
    grp = SW_HEADS // SW_KV_HEADS
    for g in range(grp):
        qp = q_ref[0, :, g * LANES:(g + 1) * LANES]
        res = None
        for kv in range(SW_KV_HEADS):
            head = kv * grp + g
            slope = 2.0 ** (-8.0 * (head + 1.0) / SW_HEADS)
            qh = jnp.where(lane_kv == kv, qp, jnp.zeros_like(qp))
            z = _dot_nt(qh, kcat) * scale - slope * distf
            z = jnp.where(valid, z, -1e30)
            sink = sinks_ref[head]
            m = jnp.maximum(jnp.max(z, axis=1, keepdims=True), sink)
            p = jnp.where(valid, jnp.exp(z - m), 0.0)
            denom = jnp.sum(p, axis=1, keepdims=True) + jnp.exp(sink - m)
            o = _dot(p.astype(_MXU), vcat) / denom
            res = o if kv == 0 else jnp.where(lane_kv == 0, res, o)
        o_ref[0, :, g * LANES:(g + 1) * LANES] = res.astype(o_ref.dtype)


def _sw_attention(sw, sinks):
    b, s, _ = sw.shape
    t = SW_TILE
    qb = SW_WIDTH // LANES
    cur = lambda c: (lambda bi, i: (bi, i, c))
    prev = lambda c: (lambda bi, i: (bi, jnp.maximum(i - 1, 0), c))
    return pl.pallas_call(
        _sw_body,
        grid=(b, s // t),
        in_specs=[
            pl.BlockSpec(memory_space=pltpu.SMEM),
            pl.BlockSpec((1, t, SW_WIDTH), lambda bi, i: (bi, i, 0)),
            pl.BlockSpec((1, t, LANES), prev(qb)),
            pl.BlockSpec((1, t, LANES), cur(qb)),
            pl.BlockSpec((1, t, LANES), prev(qb + 1)),
            pl.BlockSpec((1, t, LANES), cur(qb + 1)),
        ],
        out_specs=pl.BlockSpec((1, t, SW_WIDTH), lambda bi, i: (bi, i, 0)),
        out_shape=jax.ShapeDtypeStruct((b, s, SW_WIDTH), _MXU),
        compiler_params=_params("parallel", "arbitrary"),
        name="sw_attn",
    )(sinks, sw, sw, sw, sw, sw)


def _head_block_ones():
    r = lax.broadcasted_iota(jnp.int32, (RW_WIDTH, RW_WIDTH), 0) // HEAD_DIM
    c = lax.broadcasted_iota(jnp.int32, (RW_WIDTH, RW_WIDTH), 1) // HEAD_DIM
    return r == c


def _rw_prep_body(x_ref, xp_ref, mu_ref, w0_ref, wup_ref, a0_ref, aup_ref, gup_ref,
                  kk_ref, ka_ref, r_o, kp_o, v_o, kkn_o, b_o, lw_o, g_o):
    i = pl.program_id(1)
    x = x_ref[0]
    n = x.shape[0]
    prev_last = jnp.where(i > 0, xp_ref[0][7:8, :], 0.0)
    first = lax.broadcasted_iota(jnp.int32, (n, 1), 0) == 0
    xprev = jnp.where(first, prev_last, pltpu.roll(x, 1, 0))
    xm = x + (xprev - x) * mu_ref[...]
    c = RW_WIDTH
    r, k, v = xm[:, :c], xm[:, c:2 * c], xm[:, 2 * c:3 * c]
    xw = xm[:, 3 * c:3 * c + LANES]
    xa = xm[:, 3 * c + LANES:3 * c + 2 * LANES]
    xg = xm[:, 3 * c + 2 * LANES:]
    log_w = -_softplus(-(w0_ref[...] + _dot(jnp.tanh(xw), wup_ref[...], _HI))) - 0.5
    a = _sigmoid(a0_ref[...] + _dot(xa, aup_ref[...], _HI))
    g = _dot(_sigmoid(xg), gup_ref[...], _HI)
    kk = k * kk_ref[...]
    ones = jnp.where(_head_block_ones(), 1.0, 0.0)
    n2 = _dot(kk * kk, ones, _HI)
    kk = kk * lax.rsqrt(jnp.maximum(n2, 1e-24))
    r_o[0] = r
    kp_o[0] = k * (1.0 + (a - 1.0) * ka_ref[...])
    v_o[0] = v
    kkn_o[0] = kk
    b_o[0] = kk * a
    lw_o[0] = -jnp.exp(log_w)
    g_o[0] = g


def _rw_prep(rw, mu, w0, wup, a0, aup, gup, k_k, k_a):
    b, s, _ = rw.shape
    n = RW_ROWS
    blk = pl.BlockSpec((1, n, RW_WIDTH), lambda bi, i: (bi, i, 0))
    return pl.pallas_call(
        _rw_prep_body,
        grid=(b, s // n),
        in_specs=[
            pl.BlockSpec((1, n, RW_PAD_WIDTH), lambda bi, i: (bi, i, 0)),
            pl.BlockSpec((1, 8, RW_PAD_WIDTH),
                         lambda bi, i: (bi, jnp.maximum(i * (n // 8) - 1, 0), 0)),
            _const_spec((1, RW_PAD_WIDTH)),
            _const_spec((1, RW_WIDTH)),
            _const_spec((LANES, RW_WIDTH)),
            _const_spec((1, RW_WIDTH)),
            _const_spec((LANES, RW_WIDTH)),
            _const_spec((2 * LANES, RW_WIDTH)),
            _const_spec((1, RW_WIDTH)),
            _const_spec((1, RW_WIDTH)),
        ],
        out_specs=[blk] * 7,
        out_shape=[jax.ShapeDtypeStruct((b, s, RW_WIDTH), _F32)] * 7,
        compiler_params=_params("parallel", "arbitrary"),
        name="rw_prep",
    )(rw, rw, mu, w0, wup, a0, aup, gup, k_k, k_a)


def _head_stack(x, lane_head):
    return jnp.concatenate(
        [jnp.where(lane_head == h, x, 0.0) for h in range(RW_HEADS)], axis=0)


def _rw_chunk_body(r_ref, kp_ref, v_ref, kk_ref, b_ref, lw_ref,
                   w1r_o, w2_o, mkv_o, mb_o, gt_o, pct_o):
    cl = RW_CHUNK
    r, kp, v, kk, b, lw = (x[0] for x in (r_ref, kp_ref, v_ref, kk_ref, b_ref, lw_ref))
    lane_head = lax.broadcasted_iota(jnp.int32, (1, RW_WIDTH), 1) // HEAD_DIM
    tr = lax.broadcasted_iota(jnp.int32, (cl, cl), 0)
    tc = lax.broadcasted_iota(jnp.int32, (cl, cl), 1)
    cum = _dot(jnp.where(tr >= tc, 1.0, 0.0), lw, _HI)
    cum_end = cum[cl - 1:cl, :]
    kk_t = kk * jnp.exp(cum - lw)
    r_t = r * jnp.exp(cum)
    inv = jnp.exp(-cum)
    k_h, b_h = kp * inv, b * inv
    to_end = jnp.exp(cum_end - cum)
    k_end, b_end = kp * to_end, b * to_end

    khs, bhs, vs = (_head_stack(x, lane_head) for x in (k_h, b_h, v))
    t_i = lax.broadcasted_iota(jnp.int32, (cl, RW_WIDTH), 0)
    s_i = lax.broadcasted_iota(jnp.int32, (cl, RW_WIDTH), 1) % cl
    strict, incl = t_i > s_i, t_i >= s_i
    a_k = jnp.where(strict, _dot_nt(kk_t, khs, _HI), 0.0)
    m_k = jnp.where(incl, _dot_nt(r_t, khs, _HI), 0.0)
    m_b = jnp.where(incl, _dot_nt(r_t, bhs, _HI), 0.0)
    akv = _dot(a_k, vs, _HI)
    mkv = _dot(m_k, vs, _HI)

    br = lax.broadcasted_iota(jnp.int32, (RW_WIDTH, RW_WIDTH), 0)
    bc = lax.broadcasted_iota(jnp.int32, (RW_WIDTH, RW_WIDTH), 1)
    bd_strict = (br // cl == bc // cl) & (br > bc)
    kk4 = jnp.concatenate([kk_t] * RW_HEADS, axis=0)
    a_bd = jnp.where(bd_strict, _dot_nt(kk4, bhs, _HI), 0.0)
    t_bd = jnp.where(br == bc, 1.0, 0.0) - a_bd
    a_pow = a_bd
    for _ in range(5):
        a_pow = _dot(a_pow, a_pow, _HI)
        t_bd = t_bd + _dot(t_bd, a_pow, _HI)
    t_cat = t_bd[0:cl] + t_bd[cl:2 * cl] + t_bd[2 * cl:3 * cl] + t_bd[3 * cl:4 * cl]
    w1 = _dot(t_cat, _head_stack(kk_t, lane_head), _HI)
    w2 = _dot(t_cat, _head_stack(akv, lane_head), _HI)

    w1r_o[0] = jnp.concatenate([w1, r_t], axis=0)
    w2_o[0] = w2
    mkv_o[0] = mkv
    mb_o[0] = m_b
    gt_o[0] = jnp.concatenate([k_end, -b_end], axis=0).T
    pct_o[0] = jnp.broadcast_to(jnp.exp(cum_end), (2 * cl, RW_WIDTH)).T


def _rw_chunk(r, kp, v, kk, b, lw):
    bsz, s, _ = r.shape
    cl = RW_CHUNK
    nc = s // cl
    blk = pl.BlockSpec((1, cl, RW_WIDTH), lambda bi, i: (bi, i, 0))
    out = lambda rows, cols: pl.BlockSpec((1, rows, cols), lambda bi, i: (bi, i, 0))
    return pl.pallas_call(
        _rw_chunk_body,
        grid=(bsz, nc),
        in_specs=[blk] * 6,
        out_specs=[out(2 * cl, RW_WIDTH), blk, blk, blk,
                   out(RW_WIDTH, 2 * cl), out(RW_WIDTH, 2 * cl)],
        out_shape=[
            jax.ShapeDtypeStruct((bsz, nc * 2 * cl, RW_WIDTH), _F32),
            jax.ShapeDtypeStruct((bsz, s, RW_WIDTH), _F32),
            jax.ShapeDtypeStruct((bsz, s, RW_WIDTH), _F32),
            jax.ShapeDtypeStruct((bsz, s, RW_WIDTH), _F32),
            jax.ShapeDtypeStruct((bsz, nc * RW_WIDTH, 2 * cl), _F32),
            jax.ShapeDtypeStruct((bsz, nc * RW_WIDTH, 2 * cl), _F32),
        ],
        compiler_params=_params("parallel", "parallel"),
        name="rw_chunk",
    )(r, kp, v, kk, b, lw)


def _rw_scan_body(w1r_ref, w2_ref, mkv_ref, mb_ref, gt_ref, pct_ref, v_ref, y_ref, h_ref):
    cl = RW_CHUNK

    @pl.when(pl.program_id(1) == 0)
    def _():
        h_ref[...] = jnp.zeros_like(h_ref)

    lane_head = lax.broadcasted_iota(jnp.int32, (1, RW_WIDTH), 1) // HEAD_DIM
    h = h_ref[...]
    uy = _dot(w1r_ref[0], h, _HI)
    u = uy[:cl] + w2_ref[0]
    y_ref[0] = uy[cl:] + mkv_ref[0] - _dot(mb_ref[0], _head_stack(u, lane_head), _HI)
    vu = jnp.concatenate([v_ref[0], u], axis=0)
    upd = jnp.where(_head_block_ones(), _dot(gt_ref[0], vu, _HI), 0.0)
    pct = pct_ref[0]
    h_ref[...] = jnp.concatenate([pct, pct], axis=1) * h + upd


def _rw_scan(w1r, w2, mkv, mb, gt, pct, v):
    bsz, s, _ = v.shape
    cl = RW_CHUNK
    blk = pl.BlockSpec((1, cl, RW_WIDTH), lambda bi, i: (bi, i, 0))
    spec = lambda rows, cols: pl.BlockSpec((1, rows, cols), lambda bi, i: (bi, i, 0))
    return pl.pallas_call(
        _rw_scan_body,
        grid=(bsz, s // cl),
        in_specs=[spec(2 * cl, RW_WIDTH), blk, blk, blk,
                  spec(RW_WIDTH, 2 * cl), spec(RW_WIDTH, 2 * cl), blk],
        out_specs=blk,
        out_shape=jax.ShapeDtypeStruct((bsz, s, RW_WIDTH), _F32),
        scratch_shapes=[pltpu.VMEM((RW_WIDTH, RW_WIDTH), _F32)],
        compiler_params=_params("parallel", "arbitrary"),
        name="rw_scan",
    )(w1r, w2, mkv, mb, gt, pct, v)


def _rw_post_body(y_ref, r_ref, kp_ref, v_ref, g_ref, gng_ref, gnb_ref, rk_ref, o_ref):
    y = y_ref[0]
    mean_mat = jnp.where(_head_block_ones(), 1.0 / HEAD_DIM, 0.0)
    mu = _dot(y, mean_mat, _HI)
    d = y - mu
    var = _dot(d * d, mean_mat, _HI)
    yn = d * lax.rsqrt(var + RW_GN_EPS) * gng_ref[...] + gnb_ref[...]
    ones = jnp.where(_head_block_ones(), 1.0, 0.0)
    bonus = _dot(r_ref[0] * kp_ref[0] * rk_ref[...], ones, _HI) * v_ref[0]
    o_ref[0] = ((yn + bonus) * g_ref[0]).astype(o_ref.dtype)


def _rw_post(y, r, kp, v, g, gn_g, gn_b, r_k):
    b, s, _ = y.shape
    n = RW_ROWS
    blk = pl.BlockSpec((1, n, RW_WIDTH), lambda bi, i: (bi, i, 0))
    vec = _const_spec((1, RW_WIDTH))
    return pl.pallas_call(
        _rw_post_body,
        grid=(b, s // n),
        in_specs=[blk] * 5 + [vec] * 3,
        out_specs=blk,
        out_shape=jax.ShapeDtypeStruct((b, s, RW_WIDTH), _MXU),
        compiler_params=_params("parallel", "parallel"),
        name="rw_post",
    )(y, r, kp, v, g, gn_g, gn_b, r_k)


def _rwkv7(rw, mu, w0, wup, a0, aup, gup, k_k, k_a, r_k, gn_g, gn_b):
    r, kp, v, kk, b, lw, g = _rw_prep(rw, mu, w0, wup, a0, aup, gup, k_k, k_a)
    w1r, w2, mkv, mb, gt, pct = _rw_chunk(r, kp, v, kk, b, lw)
    y = _rw_scan(w1r, w2, mkv, mb, gt, pct, v)
    return _rw_post(y, r, kp, v, g, gn_g, gn_b, r_k)


_SW_PAIR_ORDER = tuple(h for g in range(SW_HEADS // SW_KV_HEADS)
                       for h in (g, g + SW_HEADS // SW_KV_HEADS))


def _pad_cols(w, width):
    return jnp.pad(w, ((0, 0), (0, width - w.shape[1])))


def _pad_rows(w, rows):
    return jnp.pad(w, ((0, rows - w.shape[0]), (0, 0)))


def _pair_heads(w, axis):
    shape = w.shape
    w = w.reshape(shape[:axis] + (SW_HEADS, HEAD_DIM) + shape[axis + 1:])
    w = jnp.take(w, jnp.array(_SW_PAIR_ORDER), axis=axis)
    return w.reshape(shape)


def _mix_in_layout(w_in, mu):
    o = 0
    sb = w_in[:, o:o + SB_IN_WIDTH]; o += SB_IN_WIDTH
    rkv = w_in[:, o:o + 3 * RW_WIDTH]; o += 3 * RW_WIDTH
    xw = w_in[:, o:o + DECAY_LORA]; o += DECAY_LORA
    xa = w_in[:, o:o + AAA_LORA]; o += AAA_LORA
    xg = w_in[:, o:o + GATE_LORA]; o += GATE_LORA
    swq = w_in[:, o:o + SW_WIDTH]; o += SW_WIDTH
    swkv = w_in[:, o:]
    w = jnp.concatenate([
        sb, rkv, _pad_cols(xw, LANES), _pad_cols(xa, LANES), _pad_cols(xg, 2 * LANES),
        _pair_heads(swq, 1), swkv], axis=1).astype(_MXU)
    m = mu[None, :]
    c = 3 * RW_WIDTH
    mu_p = jnp.concatenate([
        m[:, :c], _pad_cols(m[:, c:c + DECAY_LORA], LANES),
        _pad_cols(m[:, c + DECAY_LORA:c + DECAY_LORA + AAA_LORA], LANES),
        _pad_cols(m[:, c + DECAY_LORA + AAA_LORA:], 2 * LANES)], axis=1)
    return w, mu_p


def kernel(x, ffn1_norm, ffn1_w_in, ffn1_w_out, mix_norm, mix_w_in, mix_w_out, rw_mu, rw_w0,
           rw_w_up, rw_a0, rw_a_up, rw_g_up, rw_k_k, rw_k_a, rw_r_k, rw_gn_g, rw_gn_b, sw_sinks,
           ffn2_norm, ffn2_w_in, ffn2_w_out, final_norm):
    bsz, s, d = x.shape
    xf = x.reshape(bsz * s, d)
    row = lambda t: t[None, :]
    final_g = row(final_norm)
    for l in range(DEPTH):
        xf = _ffn(xf, row(ffn1_norm[l]), ffn1_w_in[l].astype(_MXU), ffn1_w_out[l].astype(_MXU),
                  final_g, False)
        w_in, mu = _mix_in_layout(mix_w_in[l], rw_mu[l])
        sb, rw, sw = _inproj(xf, row(mix_norm[l]), w_in)
        sb = sb.reshape(bsz, s, SB_IN_WIDTH)
        rw = rw.reshape(bsz, s, RW_PAD_WIDTH)
        sw = sw.reshape(bsz, s, SW_IN_WIDTH)
        kt = sb[:, :, SB_WIDTH:2 * SB_WIDTH].reshape(bsz, s // SB_TILE, SB_TILE, SB_WIDTH)
        kt = jnp.swapaxes(kt, 2, 3)
        sb_out = _sb_attention(sb, kt)
        rw_out = _rwkv7(
            rw, mu, row(rw_w0[l]), _pad_rows(rw_w_up[l], LANES), row(rw_a0[l]),
            _pad_rows(rw_a_up[l], LANES), _pad_rows(rw_g_up[l], 2 * LANES),
            row(rw_k_k[l]), row(rw_k_a[l]), rw_r_k[l].reshape(1, RW_WIDTH),
            row(rw_gn_g[l]), row(rw_gn_b[l]))
        sw_out = _sw_attention(sw, sw_sinks[l])
        w_out = mix_w_out[l].astype(_MXU)
        xf = _outproj(
            xf, sb_out.reshape(bsz * s, SB_WIDTH), rw_out.reshape(bsz * s, RW_WIDTH),
            sw_out.reshape(bsz * s, SW_WIDTH), w_out[:SB_WIDTH],
            w_out[SB_WIDTH:SB_WIDTH + RW_WIDTH], _pair_heads(w_out[SB_WIDTH + RW_WIDTH:], 0))
        xf = _ffn(xf, row(ffn2_norm[l]), ffn2_w_in[l].astype(_MXU), ffn2_w_out[l].astype(_MXU),
                  final_g, l == DEPTH - 1)
    return xf.reshape(bsz, s, d)
```

```python
import functools

import jax
import jax.numpy as jnp
from jax import lax
from jax.experimental import pallas as pl
from jax.experimental.pallas import tpu as pltpu

D_MODEL = 1024
DEPTH = 4
HEAD_DIM = 64
SB_HEADS = 4
SB_WIDTH = SB_HEADS * HEAD_DIM
RW_HEADS = 4
RW_WIDTH = RW_HEADS * HEAD_DIM
DECAY_LORA = 64
AAA_LORA = 64
GATE_LORA = 160
RW_GN_EPS = 64e-5
SW_HEADS = 8
SW_KV_HEADS = 2
SW_WIDTH = SW_HEADS * HEAD_DIM
SW_KV_WIDTH = SW_KV_HEADS * HEAD_DIM
WINDOW = 128
D_FF = 2816
NORM_EPS = 1e-6

LANES = 128
MXU_DIM = 256
RW_PAD_WIDTH = 3 * RW_WIDTH + 2 * LANES + 2 * LANES
SB_IN_WIDTH = 3 * SB_WIDTH
SW_IN_WIDTH = SW_WIDTH + 2 * SW_KV_WIDTH
IN_PAD_WIDTH = SB_IN_WIDTH + RW_PAD_WIDTH + SW_IN_WIDTH

ROW_TILE = 512
FF_TILE = MXU_DIM
SB_TILE = 256
SW_TILE = WINDOW
RW_ROWS = 256
RW_CHUNK = 64
RW_STEP_CHUNKS = 4

VMEM_LIMIT_BYTES = 56 * 1024 * 1024

_MXU = jnp.bfloat16
_F32 = jnp.float32
_HI = lax.Precision.HIGHEST


def _dot(a, b, precision=None):
    return jnp.dot(a, b, preferred_element_type=_F32, precision=precision)


def _dot_nt(a, b, precision=None):
    return lax.dot_general(a, b, (((1,), (1,)), ((), ())),
                           preferred_element_type=_F32, precision=precision)


def _sigmoid(x):
    return 1.0 / (1.0 + jnp.exp(-x))


def _softplus(x):
    return jnp.maximum(x, 0.0) + jnp.log(1.0 + jnp.exp(-jnp.abs(x)))


def _params(*sem):
    return pltpu.CompilerParams(dimension_semantics=sem, vmem_limit_bytes=VMEM_LIMIT_BYTES)


def _const_spec(shape):
    nd = len(shape)
    return pl.BlockSpec(shape, lambda *_: (0,) * nd, pipeline_mode=pl.Buffered(1))


def _ffn_body(x_ref, g_ref, win_ref, wout_ref, fg_ref, o_ref, act_ref, *, final):
    x = x_ref[...]
    ms = jnp.mean(x * x, axis=-1, keepdims=True)
    hn = (x * lax.rsqrt(ms + NORM_EPS) * g_ref[...]).astype(_MXU)
    for c in range(D_FF // FF_TILE):
        lo, hi = c * FF_TILE, (c + 1) * FF_TILE
        gate = _dot(hn, win_ref[:, lo:hi])
        up = _dot(hn, win_ref[:, D_FF + lo:D_FF + hi])
        act_ref[:, lo:hi] = (gate * _sigmoid(gate) * up).astype(_MXU)
    y = x + 0.5 * _dot(act_ref[...], wout_ref[...])
    if final:
        ms = jnp.mean(y * y, axis=-1, keepdims=True)
        y = y * lax.rsqrt(ms + NORM_EPS) * fg_ref[...]
    o_ref[...] = y


def _ffn(x, g, w_in, w_out, final_g, final):
    m = x.shape[0]
    row = lambda i: (i, 0)
    return pl.pallas_call(
        functools.partial(_ffn_body, final=final),
        grid=(m // ROW_TILE,),
        in_specs=[
            pl.BlockSpec((ROW_TILE, D_MODEL), row),
            _const_spec((1, D_MODEL)),
            _const_spec((D_MODEL, 2 * D_FF)),
            _const_spec((D_FF, D_MODEL)),
            _const_spec((1, D_MODEL)),
        ],
        out_specs=pl.BlockSpec((ROW_TILE, D_MODEL), row),
        out_shape=jax.ShapeDtypeStruct((m, D_MODEL), _F32),
        scratch_shapes=[pltpu.VMEM((ROW_TILE, D_FF), _MXU)],
        compiler_params=_params("parallel"),
        name="ffn",
    )(x, g, w_in, w_out, final_g)


def _inproj_body(x_ref, g_ref, w_ref, sb_ref, rw_ref, sw_ref):
    x = x_ref[...]
    ms = jnp.mean(x * x, axis=-1, keepdims=True)
    hn = (x * lax.rsqrt(ms + NORM_EPS) * g_ref[...]).astype(_MXU)
    sb_ref[...] = _dot(hn, w_ref[:, :SB_IN_WIDTH]).astype(sb_ref.dtype)
    rw_ref[...] = _dot(hn, w_ref[:, SB_IN_WIDTH:SB_IN_WIDTH + RW_PAD_WIDTH])
    sw_ref[...] = _dot(hn, w_ref[:, SB_IN_WIDTH + RW_PAD_WIDTH:]).astype(sw_ref.dtype)


def _inproj(x, g, w):
    m = x.shape[0]
    row = lambda i: (i, 0)
    return pl.pallas_call(
        _inproj_body,
        grid=(m // ROW_TILE,),
        in_specs=[
            pl.BlockSpec((ROW_TILE, D_MODEL), row),
            _const_spec((1, D_MODEL)),
            _const_spec((D_MODEL, IN_PAD_WIDTH)),
        ],
        out_specs=[
            pl.BlockSpec((ROW_TILE, SB_IN_WIDTH), row),
            pl.BlockSpec((ROW_TILE, RW_PAD_WIDTH), row),
            pl.BlockSpec((ROW_TILE, SW_IN_WIDTH), row),
        ],
        out_shape=[
            jax.ShapeDtypeStruct((m, SB_IN_WIDTH), _MXU),
            jax.ShapeDtypeStruct((m, RW_PAD_WIDTH), _F32),
            jax.ShapeDtypeStruct((m, SW_IN_WIDTH), _MXU),
        ],
        compiler_params=_params("parallel"),
        name="inproj",
    )(x, g, w)


def _outproj_body(x_ref, sb_ref, rw_ref, sw_ref, wsb_ref, wrw_ref, wsw_ref, o_ref):
    y = _dot(sb_ref[...], wsb_ref[...])
    y = y + _dot(rw_ref[...], wrw_ref[...])
    y = y + _dot(sw_ref[...], wsw_ref[...])
    o_ref[...] = x_ref[...] + y


def _outproj(x, sb, rw, sw, w_sb, w_rw, w_sw):
    m = x.shape[0]
    row = lambda i: (i, 0)
    return pl.pallas_call(
        _outproj_body,
        grid=(m // ROW_TILE,),
        in_specs=[
            pl.BlockSpec((ROW_TILE, D_MODEL), row),
            pl.BlockSpec((ROW_TILE, SB_WIDTH), row),
            pl.BlockSpec((ROW_TILE, RW_WIDTH), row),
            pl.BlockSpec((ROW_TILE, SW_WIDTH), row),
            _const_spec((SB_WIDTH, D_MODEL)),
            _const_spec((RW_WIDTH, D_MODEL)),
            _const_spec((SW_WIDTH, D_MODEL)),
        ],
        out_specs=pl.BlockSpec((ROW_TILE, D_MODEL), row),
        out_shape=jax.ShapeDtypeStruct((m, D_MODEL), _F32),
        compiler_params=_params("parallel"),
        name="outproj",
    )(x, sb, rw, sw, w_sb, w_rw, w_sw)


SB_SKIP_LOG = -88.0


def _sb_body(q_ref, kt_ref, v_ref, o_ref):
    t = SB_TILE
    i = pl.program_id(1)
    q = q_ref[0]
    lane_head = lax.broadcasted_iota(jnp.int32, (1, SB_WIDTH), 1) // HEAD_DIM
    row = lax.broadcasted_iota(jnp.int32, (t, t), 0)
    col = lax.broadcasted_iota(jnp.int32, (t, t), 1)
    later = jnp.where(row > col, 1.0, 0.0).astype(_MXU)
    causal = col < row
    scale = HEAD_DIM ** -0.5
    q_heads = [jnp.where(lane_head == h, q, jnp.zeros_like(q)) for h in range(SB_HEADS)]

    def block(jb, acc, carries, diagonal):
        ktb = kt_ref[0, jb]
        vb = v_ref[0, pl.ds(pl.multiple_of(jb * t, t), t), :]
        weights, new_carries = [], []
        for h in range(SB_HEADS):
            z = _dot(q_heads[h], ktb) * scale
            lom = -_softplus(z)
            if diagonal:
                lom = jnp.where(causal, lom, 0.0)
            hi = lom.astype(_MXU)
            lo = (lom - hi.astype(_F32)).astype(_MXU)
            tail = _dot(hi, later) + _dot(lo, later) + carries[h]
            a = jnp.exp(z + lom + tail)
            if diagonal:
                a = jnp.where(causal, a, 0.0)
            weights.append(a.astype(_MXU))
            new_carries.append(carries[h] + jnp.sum(lom, axis=1, keepdims=True))
        v_heads = jnp.concatenate(
            [jnp.where(lane_head == h, vb, jnp.zeros_like(vb)) for h in range(SB_HEADS)], axis=0)
        acc = acc + _dot(jnp.concatenate(weights, axis=1), v_heads)
        return acc, tuple(new_carries)

    zero = jnp.zeros((t, 1), _F32)
    acc, carries = block(i, jnp.zeros((t, SB_WIDTH), _F32), (zero,) * SB_HEADS, True)

    def live(state):
        jb, _, carries = state
        top = functools.reduce(jnp.maximum, carries)
        return (jb >= 0) & (jnp.max(top) > SB_SKIP_LOG)

    def step(state):
        jb, acc, carries = state
        acc, carries = block(jb, acc, carries, False)
        return jb - 1, acc, carries

    _, acc, _ = lax.while_loop(live, step, (i - 1, acc, carries))
    o_ref[0] = acc.astype(o_ref.dtype)


def _sb_attention(sb, kt):
    b, s, _ = sb.shape
    t = SB_TILE
    return pl.pallas_call(
        _sb_body,
        grid=(b, s // t),
        in_specs=[
            pl.BlockSpec((1, t, SB_WIDTH), lambda bi, i: (bi, i, 0)),
            pl.BlockSpec((1, s // t, SB_WIDTH, t), lambda bi, i: (bi, 0, 0, 0)),
            pl.BlockSpec((1, s, SB_WIDTH), lambda bi, i: (bi, 0, 2)),
        ],
        out_specs=pl.BlockSpec((1, t, SB_WIDTH), lambda bi, i: (bi, i, 0)),
        out_shape=jax.ShapeDtypeStruct((b, s, SB_WIDTH), _MXU),
        compiler_params=_params("parallel", "arbitrary"),
        name="sb_attn",
    )(sb, kt, sb)


def _sw_body(sinks_ref, q_ref, kp_ref, kc_ref, vp_ref, vc_ref, o_ref):
    t = SW_TILE
    i = pl.program_id(1)
    kcat = jnp.concatenate([kp_ref[0], kc_ref[0]], axis=0)
    vcat = jnp.concatenate([vp_ref[0], vc_ref[0]], axis=0)
    qi = lax.broadcasted_iota(jnp.int32, (t, 2 * t), 0)
    si = lax.broadcasted_iota(jnp.int32, (t, 2 * t), 1)
    dist = qi + t - si
    valid = (dist >= 0) & (dist < WINDOW) & (i * t - t + si >= 0)
    distf = dist.astype(_F32)
    lane_kv = lax.broadcasted_iota(jnp.int32, (1, LANES), 1) // HEAD_DIM
    scale = HEAD_DIM ** -0.5
    grp = SW_HEADS // SW_KV_HEADS
    for g in range(grp):
        qp = q_ref[0, :, g * LANES:(g + 1) * LANES]
        res = None
        for kv in range(SW_KV_HEADS):
            head = kv * grp + g
            slope = 2.0 ** (-8.0 * (head + 1.0) / SW_HEADS)
            qh = jnp.where(lane_kv == kv, qp, jnp.zeros_like(qp))
            z = _dot_nt(qh, kcat) * scale - slope * distf
            z = jnp.where(valid, z, -1e30)
            sink = sinks_ref[head]
            m = jnp.maximum(jnp.max(z, axis=1, keepdims=True), sink)
            p = jnp.where(valid, jnp.exp(z - m), 0.0)
            denom = jnp.sum(p, axis=1, keepdims=True) + jnp.exp(sink - m)
            o = _dot(p.astype(_MXU), vcat) / denom
            res = o if kv == 0 else jnp.where(lane_kv == 0, res, o)
        o_ref[0, :, g * LANES:(g + 1) * LANES] = res.astype(o_ref.dtype)


def _sw_attention(sw, sinks):
    b, s, _ = sw.shape
    t = SW_TILE
    qb = SW_WIDTH // LANES
    cur = lambda c: (lambda bi, i: (bi, i, c))
    prev = lambda c: (lambda bi, i: (bi, jnp.maximum(i - 1, 0), c))
    return pl.pallas_call(
        _sw_body,
        grid=(b, s // t),
        in_specs=[
            pl.BlockSpec(memory_space=pltpu.SMEM),
            pl.BlockSpec((1, t, SW_WIDTH), lambda bi, i: (bi, i, 0)),
            pl.BlockSpec((1, t, LANES), prev(qb)),
            pl.BlockSpec((1, t, LANES), cur(qb)),
            pl.BlockSpec((1, t, LANES), prev(qb + 1)),
            pl.BlockSpec((1, t, LANES), cur(qb + 1)),
        ],
        out_specs=pl.BlockSpec((1, t, SW_WIDTH), lambda bi, i: (bi, i, 0)),
        out_shape=jax.ShapeDtypeStruct((b, s, SW_WIDTH), _MXU),
        compiler_params=_params("parallel", "arbitrary"),
        name="sw_attn",
    )(sinks, sw, sw, sw, sw, sw)


def _head_block_ones():
    r = lax.broadcasted_iota(jnp.int32, (RW_WIDTH, RW_WIDTH), 0) // HEAD_DIM
    c = lax.broadcasted_iota(jnp.int32, (RW_WIDTH, RW_WIDTH), 1) // HEAD_DIM
    return r == c


def _rw_prep_body(x_ref, xp_ref, mu_ref, w0_ref, wup_ref, a0_ref, aup_ref, gup_ref,
                  kk_ref, ka_ref, r_o, kp_o, v_o, kkn_o, b_o, lw_o, g_o):
    i = pl.program_id(1)
    x = x_ref[0]
    n = x.shape[0]
    prev_last = jnp.where(i > 0, xp_ref[0][7:8, :], 0.0)
    first = lax.broadcasted_iota(jnp.int32, (n, 1), 0) == 0
    xprev = jnp.where(first, prev_last, pltpu.roll(x, 1, 0))
    xm = x + (xprev - x) * mu_ref[...]
    c = RW_WIDTH
    r, k, v = xm[:, :c], xm[:, c:2 * c], xm[:, 2 * c:3 * c]
    xw = xm[:, 3 * c:3 * c + LANES]
    xa = xm[:, 3 * c + LANES:3 * c + 2 * LANES]
    xg = xm[:, 3 * c + 2 * LANES:]
    log_w = -_softplus(-(w0_ref[...] + _dot(jnp.tanh(xw), wup_ref[...], _HI))) - 0.5
    a = _sigmoid(a0_ref[...] + _dot(xa, aup_ref[...], _HI))
    g = _dot(_sigmoid(xg), gup_ref[...], _HI)
    kk = k * kk_ref[...]
    ones = jnp.where(_head_block_ones(), 1.0, 0.0)
    n2 = _dot(kk * kk, ones, _HI)
    kk = kk * lax.rsqrt(jnp.maximum(n2, 1e-24))
    r_o[0] = r
    kp_o[0] = k * (1.0 + (a - 1.0) * ka_ref[...])
    v_o[0] = v
    kkn_o[0] = kk
    b_o[0] = kk * a
    lw_o[0] = -jnp.exp(log_w)
    g_o[0] = g


def _rw_prep(rw, mu, w0, wup, a0, aup, gup, k_k, k_a):
    b, s, _ = rw.shape
    n = RW_ROWS
    blk = pl.BlockSpec((1, n, RW_WIDTH), lambda bi, i: (bi, i, 0))
    return pl.pallas_call(
        _rw_prep_body,
        grid=(b, s // n),
        in_specs=[
            pl.BlockSpec((1, n, RW_PAD_WIDTH), lambda bi, i: (bi, i, 0)),
            pl.BlockSpec((1, 8, RW_PAD_WIDTH),
                         lambda bi, i: (bi, jnp.maximum(i * (n // 8) - 1, 0), 0)),
            _const_spec((1, RW_PAD_WIDTH)),
            _const_spec((1, RW_WIDTH)),
            _const_spec((LANES, RW_WIDTH)),
            _const_spec((1, RW_WIDTH)),
            _const_spec((LANES, RW_WIDTH)),
            _const_spec((2 * LANES, RW_WIDTH)),
            _const_spec((1, RW_WIDTH)),
            _const_spec((1, RW_WIDTH)),
        ],
        out_specs=[blk] * 7,
        out_shape=[jax.ShapeDtypeStruct((b, s, RW_WIDTH), _F32)] * 7,
        compiler_params=_params("parallel", "arbitrary"),
        name="rw_prep",
    )(rw, rw, mu, w0, wup, a0, aup, gup, k_k, k_a)


def _rw_mm(a, b):
    return _dot(a.astype(_MXU), b.astype(_MXU))


def _rw_mm_nt(a, b):
    return _dot_nt(a.astype(_MXU), b.astype(_MXU))


def _head_stack(x, lane_head):
    return jnp.concatenate(
        [jnp.where(lane_head == h, x, 0.0) for h in range(RW_HEADS)], axis=0)


def _rw_chunk_body(r_ref, kp_ref, v_ref, kk_ref, b_ref, lw_ref,
                   w1r_o, w2_o, mkv_o, mb_o, gt_o, pct_o):
    cl = RW_CHUNK
    chunks = range(RW_STEP_CHUNKS)
    rows = [slice(c * cl, (c + 1) * cl) for c in chunks]
    each = lambda f, *xs: [f(*a) for a in zip(*xs)]
    load = lambda ref: [ref[0, rw] for rw in rows]
    r, kp, v, kk, b, lw = (load(x) for x in (r_ref, kp_ref, v_ref, kk_ref, b_ref, lw_ref))

    lane_head = lax.broadcasted_iota(jnp.int32, (1, RW_WIDTH), 1) // HEAD_DIM
    tr = lax.broadcasted_iota(jnp.int32, (cl, cl), 0)
    tc = lax.broadcasted_iota(jnp.int32, (cl, cl), 1)
    lower = jnp.where(tr >= tc, 1.0, 0.0)
    cum = each(lambda x: _dot(lower, x, _HI), lw)
    cum_end = each(lambda x: x[cl - 1:cl, :], cum)
    kk_t = each(lambda x, c, l: x * jnp.exp(c - l), kk, cum, lw)
    r_t = each(lambda x, c: x * jnp.exp(c), r, cum)
    inv = each(lambda c: jnp.exp(-c), cum)
    k_h = each(jnp.multiply, kp, inv)
    b_h = each(jnp.multiply, b, inv)
    to_end = each(lambda e, c: jnp.exp(e - c), cum_end, cum)
    k_end = each(jnp.multiply, kp, to_end)
    b_end = each(jnp.multiply, b, to_end)

    stack = lambda x: _head_stack(x, lane_head)
    khs, bhs, vs = each(stack, k_h), each(stack, b_h), each(stack, v)
    t_i = lax.broadcasted_iota(jnp.int32, (cl, RW_WIDTH), 0)
    s_i = lax.broadcasted_iota(jnp.int32, (cl, RW_WIDTH), 1) % cl
    strict, incl = t_i > s_i, t_i >= s_i
    lhs = each(lambda x, y: jnp.concatenate([x, y], axis=0), kk_t, r_t)
    pk = each(_rw_mm_nt, lhs, khs)
    pb = each(_rw_mm_nt, lhs, bhs)
    a_b = each(lambda p: jnp.where(strict, p[:cl], 0.0), pb)
    m_b = each(lambda p: jnp.where(incl, p[cl:], 0.0), pb)
    akm = each(lambda p: jnp.concatenate(
        [jnp.where(strict, p[:cl], 0.0), jnp.where(incl, p[cl:], 0.0)], axis=0), pk)
    kv = each(_rw_mm, akm, vs)

    same_head = _head_block_ones()

    def block_diag(x_cat):
        return jnp.where(same_head, jnp.concatenate([x_cat] * RW_HEADS, axis=0), 0.0)

    eye = jnp.where(t_i == s_i, 1.0, 0.0)
    t_cat = [eye for _ in chunks]
    m = 1
    while m < cl:
        below = (((t_i ^ s_i) & -(2 * m)) == 0) & ((t_i & m) != 0) & ((s_i & m) == 0)
        x = each(lambda t, a: _rw_mm(t, block_diag(jnp.where(below, a, 0.0))), t_cat, a_b)
        t_cat = each(lambda t, xx: t - _rw_mm(xx, block_diag(t)), t_cat, x)
        m *= 2
    w = each(lambda t, k, p: _rw_mm(t, jnp.concatenate([stack(k), stack(p[:cl])], axis=1)),
             t_cat, kk_t, kv)

    for c in chunks:
        wide = slice(c * RW_WIDTH, (c + 1) * RW_WIDTH)
        w1r_o[0, 2 * c * cl:2 * (c + 1) * cl] = jnp.concatenate(
            [w[c][:, :RW_WIDTH], r_t[c]], axis=0).astype(w1r_o.dtype)
        w2_o[0, rows[c]] = w[c][:, RW_WIDTH:]
        mkv_o[0, rows[c]] = kv[c][cl:]
        mb_o[0, rows[c]] = m_b[c].astype(mb_o.dtype)
        gt_o[0, wide] = jnp.concatenate([k_end[c], -b_end[c]], axis=0).T.astype(gt_o.dtype)
        pct_o[0, wide] = jnp.broadcast_to(jnp.exp(cum_end[c]), (2 * cl, RW_WIDTH)).T


def _rw_chunk(r, kp, v, kk, b, lw):
    bsz, s, _ = r.shape
    n = RW_STEP_CHUNKS * RW_CHUNK
    nc = s // RW_CHUNK
    spec = lambda rows, cols: pl.BlockSpec((1, rows, cols), lambda bi, i: (bi, i, 0))
    return pl.pallas_call(
        _rw_chunk_body,
        grid=(bsz, s // n),
        in_specs=[spec(n, RW_WIDTH)] * 6,
        out_specs=[spec(2 * n, RW_WIDTH), spec(n, RW_WIDTH), spec(n, RW_WIDTH), spec(n, RW_WIDTH),
                   spec(RW_STEP_CHUNKS * RW_WIDTH, 2 * RW_CHUNK),
                   spec(RW_STEP_CHUNKS * RW_WIDTH, 2 * RW_CHUNK)],
        out_shape=[
            jax.ShapeDtypeStruct((bsz, 2 * s, RW_WIDTH), _MXU),
            jax.ShapeDtypeStruct((bsz, s, RW_WIDTH), _F32),
            jax.ShapeDtypeStruct((bsz, s, RW_WIDTH), _F32),
            jax.ShapeDtypeStruct((bsz, s, RW_WIDTH), _MXU),
            jax.ShapeDtypeStruct((bsz, nc * RW_WIDTH, 2 * RW_CHUNK), _MXU),
            jax.ShapeDtypeStruct((bsz, nc * RW_WIDTH, 2 * RW_CHUNK), _F32),
        ],
        compiler_params=_params("parallel", "parallel"),
        name="rw_chunk",
    )(r, kp, v, kk, b, lw)


def _rw_scan_body(w1r_ref, w2_ref, mkv_ref, mb_ref, gt_ref, pct_ref, v_ref, y_ref, h_ref):
    cl = RW_CHUNK

    @pl.when(pl.program_id(0) == 0)
    def _():
        h_ref[...] = jnp.zeros_like(h_ref)

    lane_head = lax.broadcasted_iota(jnp.int32, (1, RW_WIDTH), 1) // HEAD_DIM
    same_head = _head_block_ones()
    for c in range(RW_STEP_CHUNKS):
        rows = slice(c * cl, (c + 1) * cl)
        wide = slice(c * RW_WIDTH, (c + 1) * RW_WIDTH)
        batch = range(h_ref.shape[0])
        h = [h_ref[bi] for bi in batch]
        uy = [_rw_mm(w1r_ref[bi, 2 * c * cl:2 * (c + 1) * cl], h[bi]) for bi in batch]
        u = [uy[bi][:cl] + w2_ref[bi, rows] for bi in batch]
        vu = [jnp.concatenate([v_ref[bi, rows], u[bi]], axis=0) for bi in batch]
        upd = [_rw_mm(gt_ref[bi, wide], vu[bi]) for bi in batch]
        for bi in batch:
            pct = pct_ref[bi, wide]
            h_ref[bi] = (jnp.concatenate([pct, pct], axis=1) * h[bi]
                         + jnp.where(same_head, upd[bi], 0.0))
        for bi in batch:
            y_ref[bi, rows] = (uy[bi][cl:] + mkv_ref[bi, rows]
                               - _rw_mm(mb_ref[bi, rows], _head_stack(u[bi], lane_head)))


def _rw_scan(w1r, w2, mkv, mb, gt, pct, v):
    bsz, s, _ = v.shape
    n = RW_STEP_CHUNKS * RW_CHUNK
    spec = lambda rows, cols: pl.BlockSpec((bsz, rows, cols), lambda i: (0, i, 0))
    tall = spec(RW_STEP_CHUNKS * RW_WIDTH, 2 * RW_CHUNK)
    return pl.pallas_call(
        _rw_scan_body,
        grid=(s // n,),
        in_specs=[spec(2 * n, RW_WIDTH), spec(n, RW_WIDTH), spec(n, RW_WIDTH), spec(n, RW_WIDTH),
                  tall, tall, spec(n, RW_WIDTH)],
        out_specs=spec(n, RW_WIDTH),
        out_shape=jax.ShapeDtypeStruct((bsz, s, RW_WIDTH), _F32),
        scratch_shapes=[pltpu.VMEM((bsz, RW_WIDTH, RW_WIDTH), _F32)],
        compiler_params=_params("arbitrary"),
        name="rw_scan",
    )(w1r, w2, mkv, mb, gt, pct, v)


def _rw_post_body(y_ref, r_ref, kp_ref, v_ref, g_ref, gng_ref, gnb_ref, rk_ref, o_ref):
    y = y_ref[0]
    mean_mat = jnp.where(_head_block_ones(), 1.0 / HEAD_DIM, 0.0)
    mu = _dot(y, mean_mat, _HI)
    d = y - mu
    var = _dot(d * d, mean_mat, _HI)
    yn = d * lax.rsqrt(var + RW_GN_EPS) * gng_ref[...] + gnb_ref[...]
    ones = jnp.where(_head_block_ones(), 1.0, 0.0)
    bonus = _dot(r_ref[0] * kp_ref[0] * rk_ref[...], ones, _HI) * v_ref[0]
    o_ref[0] = ((yn + bonus) * g_ref[0]).astype(o_ref.dtype)


def _rw_post(y, r, kp, v, g, gn_g, gn_b, r_k):
    b, s, _ = y.shape
    n = RW_ROWS
    blk = pl.BlockSpec((1, n, RW_WIDTH), lambda bi, i: (bi, i, 0))
    vec = _const_spec((1, RW_WIDTH))
    return pl.pallas_call(
        _rw_post_body,
        grid=(b, s // n),
        in_specs=[blk] * 5 + [vec] * 3,
        out_specs=blk,
        out_shape=jax.ShapeDtypeStruct((b, s, RW_WIDTH), _MXU),
        compiler_params=_params("parallel", "parallel"),
        name="rw_post",
    )(y, r, kp, v, g, gn_g, gn_b, r_k)


def _rwkv7(rw, mu, w0, wup, a0, aup, gup, k_k, k_a, r_k, gn_g, gn_b):
    r, kp, v, kk, b, lw, g = _rw_prep(rw, mu, w0, wup, a0, aup, gup, k_k, k_a)
    w1r, w2, mkv, mb, gt, pct = _rw_chunk(r, kp, v, kk, b, lw)
    y = _rw_scan(w1r, w2, mkv, mb, gt, pct, v)
    return _rw_post(y, r, kp, v, g, gn_g, gn_b, r_k)


_SW_PAIR_ORDER = tuple(h for g in range(SW_HEADS // SW_KV_HEADS)
                       for h in (g, g + SW_HEADS // SW_KV_HEADS))


def _pad_cols(w, width):
    return jnp.pad(w, ((0, 0), (0, width - w.shape[1])))


def _pad_rows(w, rows):
    return jnp.pad(w, ((0, rows - w.shape[0]), (0, 0)))


def _pair_heads(w, axis):
    shape = w.shape
    w = w.reshape(shape[:axis] + (SW_HEADS, HEAD_DIM) + shape[axis + 1:])
    w = jnp.take(w, jnp.array(_SW_PAIR_ORDER), axis=axis)
    return w.reshape(shape)


def _mix_in_layout(w_in, mu):
    o = 0
    sb = w_in[:, o:o + SB_IN_WIDTH]; o += SB_IN_WIDTH
    rkv = w_in[:, o:o + 3 * RW_WIDTH]; o += 3 * RW_WIDTH
    xw = w_in[:, o:o + DECAY_LORA]; o += DECAY_LORA
    xa = w_in[:, o:o + AAA_LORA]; o += AAA_LORA
    xg = w_in[:, o:o + GATE_LORA]; o += GATE_LORA
    swq = w_in[:, o:o + SW_WIDTH]; o += SW_WIDTH
    swkv = w_in[:, o:]
    w = jnp.concatenate([
        sb, rkv, _pad_cols(xw, LANES), _pad_cols(xa, LANES), _pad_cols(xg, 2 * LANES),
        _pair_heads(swq, 1), swkv], axis=1).astype(_MXU)
    m = mu[None, :]
    c = 3 * RW_WIDTH
    mu_p = jnp.concatenate([
        m[:, :c], _pad_cols(m[:, c:c + DECAY_LORA], LANES),
        _pad_cols(m[:, c + DECAY_LORA:c + DECAY_LORA + AAA_LORA], LANES),
        _pad_cols(m[:, c + DECAY_LORA + AAA_LORA:], 2 * LANES)], axis=1)
    return w, mu_p


def kernel(x, ffn1_norm, ffn1_w_in, ffn1_w_out, mix_norm, mix_w_in, mix_w_out, rw_mu, rw_w0,
           rw_w_up, rw_a0, rw_a_up, rw_g_up, rw_k_k, rw_k_a, rw_r_k, rw_gn_g, rw_gn_b, sw_sinks,
           ffn2_norm, ffn2_w_in, ffn2_w_out, final_norm):
    bsz, s, d = x.shape
    xf = x.reshape(bsz * s, d)
    row = lambda t: t[None, :]
    final_g = row(final_norm)
    for l in range(DEPTH):
        xf = _ffn(xf, row(ffn1_norm[l]), ffn1_w_in[l].astype(_MXU), ffn1_w_out[l].astype(_MXU),
                  final_g, False)
        w_in, mu = _mix_in_layout(mix_w_in[l], rw_mu[l])
        sb, rw, sw = _inproj(xf, row(mix_norm[l]), w_in)
        sb = sb.reshape(bsz, s, SB_IN_WIDTH)
        rw = rw.reshape(bsz, s, RW_PAD_WIDTH)
        sw = sw.reshape(bsz, s, SW_IN_WIDTH)
        kt = sb[:, :, SB_WIDTH:2 * SB_WIDTH].reshape(bsz, s // SB_TILE, SB_TILE, SB_WIDTH)
        kt = jnp.swapaxes(kt, 2, 3)
        sb_out = _sb_attention(sb, kt)
        rw_out = _rwkv7(
            rw, mu, row(rw_w0[l]), _pad_rows(rw_w_up[l], LANES), row(rw_a0[l]),
            _pad_rows(rw_a_up[l], LANES), _pad_rows(rw_g_up[l], 2 * LANES),
            row(rw_k_k[l]), row(rw_k_a[l]), rw_r_k[l].reshape(1, RW_WIDTH),
            row(rw_gn_g[l]), row(rw_gn_b[l]))
        sw_out = _sw_attention(sw, sw_sinks[l])
        w_out = mix_w_out[l].astype(_MXU)
        xf = _outproj(
            xf, sb_out.reshape(bsz * s, SB_WIDTH), rw_out.reshape(bsz * s, RW_WIDTH),
            sw_out.reshape(bsz * s, SW_WIDTH), w_out[:SB_WIDTH],
            w_out[SB_WIDTH:SB_WIDTH + RW_WIDTH], _pair_heads(w_out[SB_WIDTH + RW_WIDTH:], 0))
        xf = _ffn(xf, row(ffn2_norm[l]), ffn2_w_in[l].astype(_MXU), ffn2_w_out[l].astype(_MXU),
                  final_g, l == DEPTH - 1)
    return xf.reshape(bsz, s, d)
```

```python
import functools

import jax
import jax.numpy as jnp
from jax import lax
from jax.experimental import pallas as pl
from jax.experimental.pallas import tpu as pltpu

D_MODEL = 1024
DEPTH = 4
HEAD_DIM = 64
SB_HEADS = 4
SB_WIDTH = SB_HEADS * HEAD_DIM
RW_HEADS = 4
RW_WIDTH = RW_HEADS * HEAD_DIM
DECAY_LORA = 64
AAA_LORA = 64
GATE_LORA = 160
RW_GN_EPS = 64e-5
SW_HEADS = 8
SW_KV_HEADS = 2
SW_WIDTH = SW_HEADS * HEAD_DIM
SW_KV_WIDTH = SW_KV_HEADS * HEAD_DIM
WINDOW = 128
D_FF = 2816
NORM_EPS = 1e-6

LANES = 128
MXU_DIM = 256
RW_PAD_WIDTH = 3 * RW_WIDTH + 2 * LANES + 2 * LANES
SB_IN_WIDTH = 3 * SB_WIDTH
SW_IN_WIDTH = SW_WIDTH + 2 * SW_KV_WIDTH
IN_PAD_WIDTH = SB_IN_WIDTH + RW_PAD_WIDTH + SW_IN_WIDTH

FFN_ROWS = 1024
PROJ_ROWS = 512
FF_TILE = MXU_DIM
SB_TILE = 256
SW_TILE = WINDOW
SW_STEP_TILES = 2
RW_CHUNK = 64
RW_STEP_CHUNKS = 4

VMEM_LIMIT_BYTES = 56 * 1024 * 1024

_MXU = jnp.bfloat16
_F32 = jnp.float32
_HI = lax.Precision.HIGHEST


def _dot(a, b, precision=None):
    return jnp.dot(a, b, preferred_element_type=_F32, precision=precision)


def _dot_nt(a, b, precision=None):
    return lax.dot_general(a, b, (((1,), (1,)), ((), ())),
                           preferred_element_type=_F32, precision=precision)


def _sigmoid(x):
    return 1.0 / (1.0 + jnp.exp(-x))


def _softplus(x):
    return jnp.maximum(x, 0.0) + jnp.log(1.0 + jnp.exp(-jnp.abs(x)))


def _params(*sem):
    return pltpu.CompilerParams(dimension_semantics=sem, vmem_limit_bytes=VMEM_LIMIT_BYTES)


def _const_spec(shape):
    nd = len(shape)
    return pl.BlockSpec(shape, lambda *_: (0,) * nd, pipeline_mode=pl.Buffered(1))


def _ffn_body(x_ref, g_ref, win_ref, wout_ref, fg_ref, o_ref, act_ref, *, final):
    x = x_ref[...]
    ms = jnp.mean(x * x, axis=-1, keepdims=True)
    hn = (x * lax.rsqrt(ms + NORM_EPS) * g_ref[...]).astype(_MXU)
    for c in range(D_FF // FF_TILE):
        lo, hi = c * FF_TILE, (c + 1) * FF_TILE
        gate = _dot(hn, win_ref[:, lo:hi])
        up = _dot(hn, win_ref[:, D_FF + lo:D_FF + hi])
        act_ref[:, lo:hi] = (gate * _sigmoid(gate) * up).astype(_MXU)
    y = x + 0.5 * _dot(act_ref[...], wout_ref[...])
    if final:
        ms = jnp.mean(y * y, axis=-1, keepdims=True)
        y = y * lax.rsqrt(ms + NORM_EPS) * fg_ref[...]
    o_ref[...] = y


def _ffn(x, g, w_in, w_out, final_g, final):
    m = x.shape[0]
    row = lambda i: (i, 0)
    return pl.pallas_call(
        functools.partial(_ffn_body, final=final),
        grid=(m // FFN_ROWS,),
        in_specs=[
            pl.BlockSpec((FFN_ROWS, D_MODEL), row),
            _const_spec((1, D_MODEL)),
            _const_spec((D_MODEL, 2 * D_FF)),
            _const_spec((D_FF, D_MODEL)),
            _const_spec((1, D_MODEL)),
        ],
        out_specs=pl.BlockSpec((FFN_ROWS, D_MODEL), row),
        out_shape=jax.ShapeDtypeStruct((m, D_MODEL), _F32),
        scratch_shapes=[pltpu.VMEM((FFN_ROWS, D_FF), _MXU)],
        compiler_params=_params("parallel"),
        name="ffn",
    )(x, g, w_in, w_out, final_g)


def _head_block_ones():
    r = lax.broadcasted_iota(jnp.int32, (RW_WIDTH, RW_WIDTH), 0) // HEAD_DIM
    c = lax.broadcasted_iota(jnp.int32, (RW_WIDTH, RW_WIDTH), 1) // HEAD_DIM
    return r == c


def _head_sum(x, mat):
    hi = x.astype(_MXU)
    lo = (x - hi.astype(_F32)).astype(_MXU)
    return _dot(hi, mat) + _dot(lo, mat)


def _inproj_body(x_ref, g_ref, w_ref, mu_ref, w0_ref, wup_ref, a0_ref, aup_ref, gup_ref,
                 kk_ref, ka_ref, sb_ref, sw_ref, r_o, kp_o, v_o, kkn_o, b_o, lw_o, g_o,
                 last_ref, *, seq_tiles):
    i = pl.program_id(0)
    x = x_ref[...]
    ms = jnp.mean(x * x, axis=-1, keepdims=True)
    hn = (x * lax.rsqrt(ms + NORM_EPS) * g_ref[...]).astype(_MXU)
    sb_ref[...] = _dot(hn, w_ref[:, :SB_IN_WIDTH]).astype(sb_ref.dtype)
    sw_ref[...] = _dot(hn, w_ref[:, SB_IN_WIDTH + RW_PAD_WIDTH:]).astype(sw_ref.dtype)
    p = _dot(hn, w_ref[:, SB_IN_WIDTH:SB_IN_WIDTH + RW_PAD_WIDTH])

    @pl.when(i == 0)
    def _():
        last_ref[...] = jnp.zeros_like(last_ref)

    n = p.shape[0]
    prev_last = jnp.where(i % seq_tiles == 0, 0.0, last_ref[7:8, :])
    last_ref[...] = p[n - 8:, :]
    first = lax.broadcasted_iota(jnp.int32, (n, 1), 0) == 0
    pprev = jnp.where(first, prev_last, pltpu.roll(p, 1, 0))
    xm = p + (pprev - p) * mu_ref[...]
    c = RW_WIDTH
    r, k, v = xm[:, :c], xm[:, c:2 * c], xm[:, 2 * c:3 * c]
    xw = xm[:, 3 * c:3 * c + LANES]
    xa = xm[:, 3 * c + LANES:3 * c + 2 * LANES]
    xg = xm[:, 3 * c + 2 * LANES:]
    log_w = -_softplus(-(w0_ref[...] + _dot(jnp.tanh(xw), wup_ref[...], _HI))) - 0.5
    a = _sigmoid(a0_ref[...] + _dot(xa, aup_ref[...], _HI))
    g = _dot(_sigmoid(xg), gup_ref[...], _HI)
    kk = k * kk_ref[...]
    ones = jnp.where(_head_block_ones(), 1.0, 0.0).astype(_MXU)
    n2 = _head_sum(kk * kk, ones)
    kk = kk * lax.rsqrt(jnp.maximum(n2, 1e-24))
    r_o[...] = r
    kp_o[...] = k * (1.0 + (a - 1.0) * ka_ref[...])
    v_o[...] = v
    kkn_o[...] = kk
    b_o[...] = kk * a
    lw_o[...] = -jnp.exp(log_w)
    g_o[...] = g


def _inproj(x, g, w, mu, w0, wup, a0, aup, gup, k_k, k_a, seq_len):
    m = x.shape[0]
    n = PROJ_ROWS
    row = lambda i: (i, 0)
    vec = _const_spec((1, RW_WIDTH))
    rw_out = pl.BlockSpec((n, RW_WIDTH), row)
    return pl.pallas_call(
        functools.partial(_inproj_body, seq_tiles=seq_len // n),
        grid=(m // n,),
        in_specs=[
            pl.BlockSpec((n, D_MODEL), row),
            _const_spec((1, D_MODEL)),
            _const_spec((D_MODEL, IN_PAD_WIDTH)),
            _const_spec((1, RW_PAD_WIDTH)),
            vec, _const_spec((LANES, RW_WIDTH)),
            vec, _const_spec((LANES, RW_WIDTH)),
            _const_spec((2 * LANES, RW_WIDTH)),
            vec, vec,
        ],
        out_specs=[pl.BlockSpec((n, SB_IN_WIDTH), row), pl.BlockSpec((n, SW_IN_WIDTH), row)]
        + [rw_out] * 7,
        out_shape=[jax.ShapeDtypeStruct((m, SB_IN_WIDTH), _MXU),
                   jax.ShapeDtypeStruct((m, SW_IN_WIDTH), _MXU)]
        + [jax.ShapeDtypeStruct((m, RW_WIDTH), _F32)] * 7,
        scratch_shapes=[pltpu.VMEM((8, RW_PAD_WIDTH), _F32)],
        compiler_params=_params("arbitrary"),
        name="inproj",
    )(x, g, w, mu, w0, wup, a0, aup, gup, k_k, k_a)


def _outproj_body(x_ref, sb_ref, rw_ref, sw_ref, wsb_ref, wrw_ref, wsw_ref, o_ref):
    y = _dot(sb_ref[...], wsb_ref[...])
    y = y + _dot(rw_ref[...], wrw_ref[...])
    y = y + _dot(sw_ref[...], wsw_ref[...])
    o_ref[...] = x_ref[...] + y


def _outproj(x, sb, rw, sw, w_sb, w_rw, w_sw):
    m = x.shape[0]
    row = lambda i: (i, 0)
    return pl.pallas_call(
        _outproj_body,
        grid=(m // FFN_ROWS,),
        in_specs=[
            pl.BlockSpec((FFN_ROWS, D_MODEL), row),
            pl.BlockSpec((FFN_ROWS, SB_WIDTH), row),
            pl.BlockSpec((FFN_ROWS, RW_WIDTH), row),
            pl.BlockSpec((FFN_ROWS, SW_WIDTH), row),
            _const_spec((SB_WIDTH, D_MODEL)),
            _const_spec((RW_WIDTH, D_MODEL)),
            _const_spec((SW_WIDTH, D_MODEL)),
        ],
        out_specs=pl.BlockSpec((FFN_ROWS, D_MODEL), row),
        out_shape=jax.ShapeDtypeStruct((m, D_MODEL), _F32),
        compiler_params=_params("parallel"),
        name="outproj",
    )(x, sb, rw, sw, w_sb, w_rw, w_sw)


SB_SKIP_LOG = -88.0


def _sb_body(q_ref, kt_ref, v_ref, o_ref):
    t = SB_TILE
    i = pl.program_id(1)
    q = q_ref[0]
    lane_head = lax.broadcasted_iota(jnp.int32, (1, SB_WIDTH), 1) // HEAD_DIM
    row = lax.broadcasted_iota(jnp.int32, (t, t), 0)
    col = lax.broadcasted_iota(jnp.int32, (t, t), 1)
    later = jnp.where(row > col, 1.0, 0.0).astype(_MXU)
    causal = col < row
    scale = HEAD_DIM ** -0.5
    q_heads = [jnp.where(lane_head == h, q, jnp.zeros_like(q)) for h in range(SB_HEADS)]

    def block(jb, acc, carries, diagonal):
        ktb = kt_ref[0, jb]
        vb = v_ref[0, pl.ds(pl.multiple_of(jb * t, t), t), :]
        weights, new_carries = [], []
        for h in range(SB_HEADS):
            z = _dot(q_heads[h], ktb) * scale
            lom = -_softplus(z)
            if diagonal:
                lom = jnp.where(causal, lom, 0.0)
            hi = lom.astype(_MXU)
            lo = (lom - hi.astype(_F32)).astype(_MXU)
            tail = _dot(hi, later) + _dot(lo, later) + carries[h]
            a = jnp.exp(z + lom + tail)
            if diagonal:
                a = jnp.where(causal, a, 0.0)
            weights.append(a.astype(_MXU))
            new_carries.append(carries[h] + jnp.sum(lom, axis=1, keepdims=True))
        v_heads = jnp.concatenate(
            [jnp.where(lane_head == h, vb, jnp.zeros_like(vb)) for h in range(SB_HEADS)], axis=0)
        acc = acc + _dot(jnp.concatenate(weights, axis=1), v_heads)
        return acc, tuple(new_carries)

    zero = jnp.zeros((t, 1), _F32)
    acc, carries = block(i, jnp.zeros((t, SB_WIDTH), _F32), (zero,) * SB_HEADS, True)

    def live(state):
        jb, _, carries = state
        top = functools.reduce(jnp.maximum, carries)
        return (jb >= 0) & (jnp.max(top) > SB_SKIP_LOG)

    def step(state):
        jb, acc, carries = state
        acc, carries = block(jb, acc, carries, False)
        return jb - 1, acc, carries

    _, acc, _ = lax.while_loop(live, step, (i - 1, acc, carries))
    o_ref[0] = acc.astype(o_ref.dtype)


def _sb_attention(sb, kt):
    b, s, _ = sb.shape
    t = SB_TILE
    return pl.pallas_call(
        _sb_body,
        grid=(b, s // t),
        in_specs=[
            pl.BlockSpec((1, t, SB_WIDTH), lambda bi, i: (bi, i, 0)),
            pl.BlockSpec((1, s // t, SB_WIDTH, t), lambda bi, i: (bi, 0, 0, 0)),
            pl.BlockSpec((1, s, SB_WIDTH), lambda bi, i: (bi, 0, 2)),
        ],
        out_specs=pl.BlockSpec((1, t, SB_WIDTH), lambda bi, i: (bi, i, 0)),
        out_shape=jax.ShapeDtypeStruct((b, s, SB_WIDTH), _MXU),
        compiler_params=_params("parallel", "arbitrary"),
        name="sb_attn",
    )(sb, kt, sb)


def _sw_body(sinks_ref, q_ref, kp_ref, kc_ref, vp_ref, vc_ref, o_ref):
    t = SW_TILE
    assert t == WINDOW
    i = pl.program_id(1)
    kcat = jnp.concatenate([kp_ref[0], kc_ref[0]], axis=0)
    vcat = jnp.concatenate([vp_ref[0], vc_ref[0]], axis=0)
    r_i = lax.broadcasted_iota(jnp.int32, (t, t), 0)
    c_i = lax.broadcasted_iota(jnp.int32, (t, t), 1)
    own = c_i <= r_i
    distf = jnp.where(own, r_i - c_i, r_i - c_i + t).astype(_F32)
    lane_kv = lax.broadcasted_iota(jnp.int32, (1, LANES), 1) // HEAD_DIM
    scale = HEAD_DIM ** -0.5
    grp = SW_HEADS // SW_KV_HEADS
    for j in range(SW_STEP_TILES):
        rows = slice(j * t, (j + 1) * t)
        kj, vj = kcat[j * t:(j + 2) * t], vcat[j * t:(j + 2) * t]
        valid = own | (i * SW_STEP_TILES + j > 0)
        for g in range(grp):
            qp = q_ref[0, rows, g * LANES:(g + 1) * LANES]
            res = None
            for kv in range(SW_KV_HEADS):
                head = kv * grp + g
                slope = 2.0 ** (-8.0 * (head + 1.0) / SW_HEADS)
                qh = jnp.where(lane_kv == kv, qp, jnp.zeros_like(qp))
                zz = _dot_nt(qh, kj)
                z = jnp.where(own, zz[:, t:], zz[:, :t]) * scale - slope * distf
                z = jnp.where(valid, z, -1e30)
                sink = sinks_ref[head]
                m = jnp.maximum(jnp.max(z, axis=1, keepdims=True), sink)
                p = jnp.where(valid, jnp.exp(z - m), 0.0)
                denom = jnp.sum(p, axis=1, keepdims=True) + jnp.exp(sink - m)
                p = p.astype(_MXU)
                zero = jnp.zeros_like(p)
                pp = jnp.concatenate([jnp.where(own, zero, p), jnp.where(own, p, zero)], axis=1)
                o = _dot(pp, vj) / denom
                res = o if kv == 0 else jnp.where(lane_kv == 0, res, o)
            o_ref[0, rows, g * LANES:(g + 1) * LANES] = res.astype(o_ref.dtype)


def _sw_attention(sw, sinks):
    b, s, _ = sw.shape
    t = SW_TILE
    n = SW_STEP_TILES * t
    qb = SW_WIDTH // LANES
    cur = lambda c: (lambda bi, i: (bi, i, c))
    prev = lambda c: (lambda bi, i: (bi, jnp.maximum(i * SW_STEP_TILES - 1, 0), c))
    return pl.pallas_call(
        _sw_body,
        grid=(b, s // n),
        in_specs=[
            pl.BlockSpec(memory_space=pltpu.SMEM),
            pl.BlockSpec((1, n, SW_WIDTH), lambda bi, i: (bi, i, 0)),
            pl.BlockSpec((1, t, LANES), prev(qb)),
            pl.BlockSpec((1, n, LANES), cur(qb)),
            pl.BlockSpec((1, t, LANES), prev(qb + 1)),
            pl.BlockSpec((1, n, LANES), cur(qb + 1)),
        ],
        out_specs=pl.BlockSpec((1, n, SW_WIDTH), lambda bi, i: (bi, i, 0)),
        out_shape=jax.ShapeDtypeStruct((b, s, SW_WIDTH), _MXU),
        compiler_params=_params("parallel", "arbitrary"),
        name="sw_attn",
    )(sinks, sw, sw, sw, sw, sw)


def _rw_mm(a, b):
    return _dot(a.astype(_MXU), b.astype(_MXU))


def _rw_mm_nt(a, b):
    return _dot_nt(a.astype(_MXU), b.astype(_MXU))


def _head_stack(x, lane_head):
    return jnp.concatenate(
        [jnp.where(lane_head == h, x, 0.0) for h in range(RW_HEADS)], axis=0)


def _rw_chunk_body(r_ref, kp_ref, v_ref, kk_ref, b_ref, lw_ref,
                   w1r_o, w2_o, mkv_o, mb_o, gt_o, pct_o):
    cl = RW_CHUNK
    chunks = range(RW_STEP_CHUNKS)
    rows = [slice(c * cl, (c + 1) * cl) for c in chunks]
    each = lambda f, *xs: [f(*a) for a in zip(*xs)]
    load = lambda ref: [ref[0, rw] for rw in rows]
    r, kp, v, kk, b, lw = (load(x) for x in (r_ref, kp_ref, v_ref, kk_ref, b_ref, lw_ref))

    lane_head = lax.broadcasted_iota(jnp.int32, (1, RW_WIDTH), 1) // HEAD_DIM
    tr = lax.broadcasted_iota(jnp.int32, (cl, cl), 0)
    tc = lax.broadcasted_iota(jnp.int32, (cl, cl), 1)
    lower = jnp.where(tr >= tc, 1.0, 0.0)
    cum = each(lambda x: _dot(lower, x, _HI), lw)
    cum_end = each(lambda x: x[cl - 1:cl, :], cum)
    kk_t = each(lambda x, c, l: x * jnp.exp(c - l), kk, cum, lw)
    r_t = each(lambda x, c: x * jnp.exp(c), r, cum)
    inv = each(lambda c: jnp.exp(-c), cum)
    k_h = each(jnp.multiply, kp, inv)
    b_h = each(jnp.multiply, b, inv)
    to_end = each(lambda e, c: jnp.exp(e - c), cum_end, cum)
    k_end = each(jnp.multiply, kp, to_end)
    b_end = each(jnp.multiply, b, to_end)

    stack = lambda x: _head_stack(x, lane_head)
    khs, bhs, vs = each(stack, k_h), each(stack, b_h), each(stack, v)
    t_i = lax.broadcasted_iota(jnp.int32, (cl, RW_WIDTH), 0)
    s_i = lax.broadcasted_iota(jnp.int32, (cl, RW_WIDTH), 1) % cl
    strict, incl = t_i > s_i, t_i >= s_i
    lhs = each(lambda x, y: jnp.concatenate([x, y], axis=0), kk_t, r_t)
    pk = each(_rw_mm_nt, lhs, khs)
    pb = each(_rw_mm_nt, lhs, bhs)
    a_b = each(lambda p: jnp.where(strict, p[:cl], 0.0), pb)
    m_b = each(lambda p: jnp.where(incl, p[cl:], 0.0), pb)
    akm = each(lambda p: jnp.concatenate(
        [jnp.where(strict, p[:cl], 0.0), jnp.where(incl, p[cl:], 0.0)], axis=0), pk)
    kv = each(_rw_mm, akm, vs)

    same_head = _head_block_ones()

    def block_diag(x_cat):
        return jnp.where(same_head, jnp.concatenate([x_cat] * RW_HEADS, axis=0), 0.0)

    eye = jnp.where(t_i == s_i, 1.0, 0.0)
    t_cat = [eye for _ in chunks]
    m = 1
    while m < cl:
        below = (((t_i ^ s_i) & -(2 * m)) == 0) & ((t_i & m) != 0) & ((s_i & m) == 0)
        x = each(lambda t, a: _rw_mm(t, block_diag(jnp.where(below, a, 0.0))), t_cat, a_b)
        t_cat = each(lambda t, xx: t - _rw_mm(xx, block_diag(t)), t_cat, x)
        m *= 2
    w = each(lambda t, k, p: _rw_mm(t, jnp.concatenate([stack(k), stack(p[:cl])], axis=1)),
             t_cat, kk_t, kv)

    for c in chunks:
        wide = slice(c * RW_WIDTH, (c + 1) * RW_WIDTH)
        w1r_o[0, 2 * c * cl:2 * (c + 1) * cl] = jnp.concatenate(
            [w[c][:, :RW_WIDTH], r_t[c]], axis=0).astype(w1r_o.dtype)
        w2_o[0, rows[c]] = w[c][:, RW_WIDTH:]
        mkv_o[0, rows[c]] = kv[c][cl:]
        mb_o[0, rows[c]] = m_b[c].astype(mb_o.dtype)
        gt_o[0, wide] = jnp.concatenate([k_end[c], -b_end[c]], axis=0).T.astype(gt_o.dtype)
        pct_o[0, wide] = jnp.broadcast_to(jnp.exp(cum_end[c]), (2 * cl, RW_WIDTH)).T


def _rw_chunk(r, kp, v, kk, b, lw):
    bsz, s, _ = r.shape
    n = RW_STEP_CHUNKS * RW_CHUNK
    nc = s // RW_CHUNK
    spec = lambda rows, cols: pl.BlockSpec((1, rows, cols), lambda bi, i: (bi, i, 0))
    return pl.pallas_call(
        _rw_chunk_body,
        grid=(bsz, s // n),
        in_specs=[spec(n, RW_WIDTH)] * 6,
        out_specs=[spec(2 * n, RW_WIDTH), spec(n, RW_WIDTH), spec(n, RW_WIDTH), spec(n, RW_WIDTH),
                   spec(RW_STEP_CHUNKS * RW_WIDTH, 2 * RW_CHUNK),
                   spec(RW_STEP_CHUNKS * RW_WIDTH, 2 * RW_CHUNK)],
        out_shape=[
            jax.ShapeDtypeStruct((bsz, 2 * s, RW_WIDTH), _MXU),
            jax.ShapeDtypeStruct((bsz, s, RW_WIDTH), _F32),
            jax.ShapeDtypeStruct((bsz, s, RW_WIDTH), _F32),
            jax.ShapeDtypeStruct((bsz, s, RW_WIDTH), _MXU),
            jax.ShapeDtypeStruct((bsz, nc * RW_WIDTH, 2 * RW_CHUNK), _MXU),
            jax.ShapeDtypeStruct((bsz, nc * RW_WIDTH, 2 * RW_CHUNK), _F32),
        ],
        compiler_params=_params("parallel", "parallel"),
        name="rw_chunk",
    )(r, kp, v, kk, b, lw)


def _rw_scan_body(w1r_ref, w2_ref, mkv_ref, mb_ref, gt_ref, pct_ref, v_ref, r_ref, kp_ref, g_ref,
                  gng_ref, gnb_ref, rk_ref, o_ref, h_ref, y_ref):
    cl = RW_CHUNK

    @pl.when(pl.program_id(0) == 0)
    def _():
        h_ref[...] = jnp.zeros_like(h_ref)

    lane_head = lax.broadcasted_iota(jnp.int32, (1, RW_WIDTH), 1) // HEAD_DIM
    same_head = _head_block_ones()
    batch = range(h_ref.shape[0])
    for c in range(RW_STEP_CHUNKS):
        rows = slice(c * cl, (c + 1) * cl)
        wide = slice(c * RW_WIDTH, (c + 1) * RW_WIDTH)
        h = [h_ref[bi] for bi in batch]
        uy = [_rw_mm(w1r_ref[bi, 2 * c * cl:2 * (c + 1) * cl], h[bi]) for bi in batch]
        u = [uy[bi][:cl] + w2_ref[bi, rows] for bi in batch]
        vu = [jnp.concatenate([v_ref[bi, rows], u[bi]], axis=0) for bi in batch]
        upd = [_rw_mm(gt_ref[bi, wide], vu[bi]) for bi in batch]
        for bi in batch:
            pct = pct_ref[bi, wide]
            h_ref[bi] = (jnp.concatenate([pct, pct], axis=1) * h[bi]
                         + jnp.where(same_head, upd[bi], 0.0))
        for bi in batch:
            y_ref[bi, rows] = (uy[bi][cl:] + mkv_ref[bi, rows]
                               - _rw_mm(mb_ref[bi, rows], _head_stack(u[bi], lane_head)))

    mean_mat = jnp.where(same_head, 1.0 / HEAD_DIM, 0.0).astype(_MXU)
    ones = jnp.where(same_head, 1.0, 0.0).astype(_MXU)
    for bi in batch:
        y = y_ref[bi]
        d = y - _head_sum(y, mean_mat)
        var = _head_sum(d * d, mean_mat)
        yn = d * lax.rsqrt(var + RW_GN_EPS) * gng_ref[...] + gnb_ref[...]
        bonus = _head_sum(r_ref[bi] * kp_ref[bi] * rk_ref[...], ones) * v_ref[bi]
        o_ref[bi] = ((yn + bonus) * g_ref[bi]).astype(o_ref.dtype)


def _rw_scan(w1r, w2, mkv, mb, gt, pct, v, r, kp, g, gn_g, gn_b, r_k):
    bsz, s, _ = v.shape
    n = RW_STEP_CHUNKS * RW_CHUNK
    spec = lambda rows, cols: pl.BlockSpec((bsz, rows, cols), lambda i: (0, i, 0))
    tall = spec(RW_STEP_CHUNKS * RW_WIDTH, 2 * RW_CHUNK)
    blk = spec(n, RW_WIDTH)
    vec = _const_spec((1, RW_WIDTH))
    return pl.pallas_call(
        _rw_scan_body,
        grid=(s // n,),
        in_specs=[spec(2 * n, RW_WIDTH), blk, blk, blk, tall, tall, blk, blk, blk, blk,
                  vec, vec, vec],
        out_specs=blk,
        out_shape=jax.ShapeDtypeStruct((bsz, s, RW_WIDTH), _MXU),
        scratch_shapes=[pltpu.VMEM((bsz, RW_WIDTH, RW_WIDTH), _F32),
                        pltpu.VMEM((bsz, n, RW_WIDTH), _F32)],
        compiler_params=_params("arbitrary"),
        name="rw_scan",
    )(w1r, w2, mkv, mb, gt, pct, v, r, kp, g, gn_g, gn_b, r_k)


def _rwkv7(r, kp, v, kk, b, lw, g, r_k, gn_g, gn_b):
    w1r, w2, mkv, mb, gt, pct = _rw_chunk(r, kp, v, kk, b, lw)
    return _rw_scan(w1r, w2, mkv, mb, gt, pct, v, r, kp, g, gn_g, gn_b, r_k)


_SW_PAIR_ORDER = tuple(h for g in range(SW_HEADS // SW_KV_HEADS)
                       for h in (g, g + SW_HEADS // SW_KV_HEADS))


def _pad_cols(w, width):
    return jnp.pad(w, ((0, 0), (0, width - w.shape[1])))


def _pad_rows(w, rows):
    return jnp.pad(w, ((0, rows - w.shape[0]), (0, 0)))


def _pair_heads(w, axis):
    shape = w.shape
    w = w.reshape(shape[:axis] + (SW_HEADS, HEAD_DIM) + shape[axis + 1:])
    w = jnp.take(w, jnp.array(_SW_PAIR_ORDER), axis=axis)
    return w.reshape(shape)


def _mix_in_layout(w_in, mu):
    o = 0
    sb = w_in[:, o:o + SB_IN_WIDTH]; o += SB_IN_WIDTH
    rkv = w_in[:, o:o + 3 * RW_WIDTH]; o += 3 * RW_WIDTH
    xw = w_in[:, o:o + DECAY_LORA]; o += DECAY_LORA
    xa = w_in[:, o:o + AAA_LORA]; o += AAA_LORA
    xg = w_in[:, o:o + GATE_LORA]; o += GATE_LORA
    swq = w_in[:, o:o + SW_WIDTH]; o += SW_WIDTH
    swkv = w_in[:, o:]
    w = jnp.concatenate([
        sb, rkv, _pad_cols(xw, LANES), _pad_cols(xa, LANES), _pad_cols(xg, 2 * LANES),
        _pair_heads(swq, 1), swkv], axis=1).astype(_MXU)
    m = mu[None, :]
    c = 3 * RW_WIDTH
    mu_p = jnp.concatenate([
        m[:, :c], _pad_cols(m[:, c:c + DECAY_LORA], LANES),
        _pad_cols(m[:, c + DECAY_LORA:c + DECAY_LORA + AAA_LORA], LANES),
        _pad_cols(m[:, c + DECAY_LORA + AAA_LORA:], 2 * LANES)], axis=1)
    return w, mu_p


def kernel(x, ffn1_norm, ffn1_w_in, ffn1_w_out, mix_norm, mix_w_in, mix_w_out, rw_mu, rw_w0,
           rw_w_up, rw_a0, rw_a_up, rw_g_up, rw_k_k, rw_k_a, rw_r_k, rw_gn_g, rw_gn_b, sw_sinks,
           ffn2_norm, ffn2_w_in, ffn2_w_out, final_norm):
    bsz, s, d = x.shape
    xf = x.reshape(bsz * s, d)
    row = lambda t: t[None, :]
    final_g = row(final_norm)
    for l in range(DEPTH):
        xf = _ffn(xf, row(ffn1_norm[l]), ffn1_w_in[l].astype(_MXU), ffn1_w_out[l].astype(_MXU),
                  final_g, False)
        w_in, mu = _mix_in_layout(mix_w_in[l], rw_mu[l])
        sb, sw, *rw_parts = _inproj(
            xf, row(mix_norm[l]), w_in, mu, row(rw_w0[l]), _pad_rows(rw_w_up[l], LANES),
            row(rw_a0[l]), _pad_rows(rw_a_up[l], LANES), _pad_rows(rw_g_up[l], 2 * LANES),
            row(rw_k_k[l]), row(rw_k_a[l]), s)
        sb = sb.reshape(bsz, s, SB_IN_WIDTH)
        sw = sw.reshape(bsz, s, SW_IN_WIDTH)
        kt = sb[:, :, SB_WIDTH:2 * SB_WIDTH].reshape(bsz, s // SB_TILE, SB_TILE, SB_WIDTH)
        kt = jnp.swapaxes(kt, 2, 3)
        sb_out = _sb_attention(sb, kt)
        rw_out = _rwkv7(
            *(t.reshape(bsz, s, RW_WIDTH) for t in rw_parts),
            rw_r_k[l].reshape(1, RW_WIDTH), row(rw_gn_g[l]), row(rw_gn_b[l]))
        sw_out = _sw_attention(sw, sw_sinks[l])
        w_out = mix_w_out[l].astype(_MXU)
        xf = _outproj(
            xf, sb_out.reshape(bsz * s, SB_WIDTH), rw_out.reshape(bsz * s, RW_WIDTH),
            sw_out.reshape(bsz * s, SW_WIDTH), w_out[:SB_WIDTH],
            w_out[SB_WIDTH:SB_WIDTH + RW_WIDTH], _pair_heads(w_out[SB_WIDTH + RW_WIDTH:], 0))
        xf = _ffn(xf, row(ffn2_norm[l]), ffn2_w_in[l].astype(_MXU), ffn2_w_out[l].astype(_MXU),
                  final_g, l == DEPTH - 1)
    return xf.reshape(bsz, s, d)
```

```python
import functools

import jax
import jax.numpy as jnp
from jax import lax
from jax.experimental import pallas as pl
from jax.experimental.pallas import tpu as pltpu

D_MODEL = 1024
DEPTH = 4
HEAD_DIM = 64
SB_HEADS = 4
SB_WIDTH = SB_HEADS * HEAD_DIM
RW_HEADS = 4
RW_WIDTH = RW_HEADS * HEAD_DIM
DECAY_LORA = 64
AAA_LORA = 64
GATE_LORA = 160
RW_GN_EPS = 64e-5
SW_HEADS = 8
SW_KV_HEADS = 2
SW_WIDTH = SW_HEADS * HEAD_DIM
SW_KV_WIDTH = SW_KV_HEADS * HEAD_DIM
WINDOW = 128
D_FF = 2816
NORM_EPS = 1e-6

LANES = 128
MXU_DIM = 256
RW_PAD_WIDTH = 3 * RW_WIDTH + 2 * LANES + 2 * LANES
SB_IN_WIDTH = 3 * SB_WIDTH
SW_IN_WIDTH = SW_WIDTH + 2 * SW_KV_WIDTH
IN_PAD_WIDTH = SB_IN_WIDTH + RW_PAD_WIDTH + SW_IN_WIDTH

FFN_ROWS = 1024
PROJ_ROWS = 1024
PROJ_SUB = 256
FF_TILE = MXU_DIM
SB_TILE = 256
SW_TILE = WINDOW
SW_STEP_TILES = 2
RW_CHUNK = 64
RW_OPS_CHUNKS = 8
RW_STEP_CHUNKS = 4

VMEM_LIMIT_BYTES = 56 * 1024 * 1024

_MXU = jnp.bfloat16
_F32 = jnp.float32
_HI = lax.Precision.HIGHEST


def _dot(a, b, precision=None):
    return jnp.dot(a, b, preferred_element_type=_F32, precision=precision)


def _dot_nt(a, b, precision=None):
    return lax.dot_general(a, b, (((1,), (1,)), ((), ())),
                           preferred_element_type=_F32, precision=precision)


def _sigmoid(x):
    return 1.0 / (1.0 + jnp.exp(-x))


def _softplus(x):
    return jnp.maximum(x, 0.0) + jnp.log(1.0 + jnp.exp(-jnp.abs(x)))


def _params(*sem):
    return pltpu.CompilerParams(dimension_semantics=sem, vmem_limit_bytes=VMEM_LIMIT_BYTES)


def _const_spec(shape):
    nd = len(shape)
    return pl.BlockSpec(shape, lambda *_: (0,) * nd, pipeline_mode=pl.Buffered(1))


def _ffn_body(*refs, final, mixed):
    if mixed:
        (x_ref, sb_ref, rw_ref, sw_ref, wsb_ref, wrw_ref, wsw_ref,
         g_ref, win_ref, wout_ref, fg_ref, o_ref, act_ref) = refs
        x = (x_ref[...] + _dot(sb_ref[...], wsb_ref[...]) + _dot(rw_ref[...], wrw_ref[...])
             + _dot(sw_ref[...], wsw_ref[...]))
    else:
        x_ref, g_ref, win_ref, wout_ref, fg_ref, o_ref, act_ref = refs
        x = x_ref[...]
    ms = jnp.mean(x * x, axis=-1, keepdims=True)
    hn = (x * lax.rsqrt(ms + NORM_EPS) * g_ref[...]).astype(_MXU)
    for c in range(D_FF // FF_TILE):
        lo, hi = c * FF_TILE, (c + 1) * FF_TILE
        gate = _dot(hn, win_ref[:, lo:hi])
        up = _dot(hn, win_ref[:, D_FF + lo:D_FF + hi])
        act_ref[:, lo:hi] = (gate * _sigmoid(gate) * up).astype(_MXU)
    y = x + 0.5 * _dot(act_ref[...], wout_ref[...])
    if final:
        ms = jnp.mean(y * y, axis=-1, keepdims=True)
        y = y * lax.rsqrt(ms + NORM_EPS) * fg_ref[...]
    o_ref[...] = y


def _ffn(x, g, w_in, w_out, final_g, final, mix=None):
    m = x.shape[0]
    row = lambda i: (i, 0)
    mix_specs = []
    if mix is not None:
        mix_specs = [pl.BlockSpec((FFN_ROWS, t.shape[1]), row) for t in mix[:3]]
        mix_specs += [_const_spec(w.shape) for w in mix[3:]]
    return pl.pallas_call(
        functools.partial(_ffn_body, final=final, mixed=mix is not None),
        grid=(m // FFN_ROWS,),
        in_specs=[pl.BlockSpec((FFN_ROWS, D_MODEL), row)] + mix_specs + [
            _const_spec((1, D_MODEL)),
            _const_spec((D_MODEL, 2 * D_FF)),
            _const_spec((D_FF, D_MODEL)),
            _const_spec((1, D_MODEL)),
        ],
        out_specs=pl.BlockSpec((FFN_ROWS, D_MODEL), row),
        out_shape=jax.ShapeDtypeStruct((m, D_MODEL), _F32),
        scratch_shapes=[pltpu.VMEM((FFN_ROWS, D_FF), _MXU)],
        compiler_params=_params("parallel"),
        name="ffn_mix" if mix is not None else "ffn",
    )(x, *(mix or ()), g, w_in, w_out, final_g)


def _head_block_ones():
    r = lax.broadcasted_iota(jnp.int32, (RW_WIDTH, RW_WIDTH), 0) // HEAD_DIM
    c = lax.broadcasted_iota(jnp.int32, (RW_WIDTH, RW_WIDTH), 1) // HEAD_DIM
    return r == c


def _head_sum(x, mat):
    hi = x.astype(_MXU)
    lo = (x - hi.astype(_F32)).astype(_MXU)
    return _dot(hi, mat) + _dot(lo, mat)


def _inproj_body(x_ref, g_ref, w_ref, mu_ref, w0_ref, wup_ref, a0_ref, aup_ref, gup_ref,
                 kk_ref, ka_ref, sb_ref, sw_ref, r_o, kp_o, v_o, kkn_o, b_o, lw_o, g_o,
                 last_ref, *, seq_tiles):
    i = pl.program_id(0)
    sub = PROJ_SUB
    n_sub = x_ref.shape[0] // sub
    c = RW_WIDTH
    ones = jnp.where(_head_block_ones(), 1.0, 0.0).astype(_MXU)
    first = lax.broadcasted_iota(jnp.int32, (sub, 1), 0) == 0

    @pl.when(i == 0)
    def _():
        last_ref[...] = jnp.zeros_like(last_ref)

    def project(j):
        rows = slice(j * sub, (j + 1) * sub)
        x = x_ref[rows]
        ms = jnp.mean(x * x, axis=-1, keepdims=True)
        hn = (x * lax.rsqrt(ms + NORM_EPS) * g_ref[...]).astype(_MXU)
        sb_ref[rows] = _dot(hn, w_ref[:, :SB_IN_WIDTH]).astype(sb_ref.dtype)
        sw_ref[rows] = _dot(hn, w_ref[:, SB_IN_WIDTH + RW_PAD_WIDTH:]).astype(sw_ref.dtype)
        return _dot(hn, w_ref[:, SB_IN_WIDTH:SB_IN_WIDTH + RW_PAD_WIDTH])

    def prepare(j, p, prev_last):
        rows = slice(j * sub, (j + 1) * sub)
        pprev = jnp.where(first, prev_last, pltpu.roll(p, 1, 0))
        xm = p + (pprev - p) * mu_ref[...]
        r, k, v = xm[:, :c], xm[:, c:2 * c], xm[:, 2 * c:3 * c]
        xw = xm[:, 3 * c:3 * c + LANES]
        xa = xm[:, 3 * c + LANES:3 * c + 2 * LANES]
        xg = xm[:, 3 * c + 2 * LANES:]
        log_w = -_softplus(-(w0_ref[...] + _rw_mm(jnp.tanh(xw), wup_ref[...]))) - 0.5
        a = _sigmoid(a0_ref[...] + _rw_mm(xa, aup_ref[...]))
        g = _rw_mm(_sigmoid(xg), gup_ref[...])
        kk = k * kk_ref[...]
        n2 = _head_sum(kk * kk, ones)
        kk = kk * lax.rsqrt(jnp.maximum(n2, 1e-24))
        r_o[rows] = r
        kp_o[rows] = k * (1.0 + (a - 1.0) * ka_ref[...])
        v_o[rows] = v
        kkn_o[rows] = kk
        b_o[rows] = kk * a
        lw_o[rows] = -jnp.exp(log_w)
        g_o[rows] = g

    prev_last = jnp.where(i % seq_tiles == 0, 0.0, last_ref[7:8, :])
    ps = [project(0)]
    for j in range(n_sub):
        if j + 1 < n_sub:
            ps.append(project(j + 1))
        prepare(j, ps[j], prev_last)
        prev_last = ps[j][sub - 1:sub, :]
    last_ref[...] = ps[-1][sub - 8:, :]


def _inproj(x, g, w, mu, w0, wup, a0, aup, gup, k_k, k_a, seq_len):
    m = x.shape[0]
    n = PROJ_ROWS
    row = lambda i: (i, 0)
    vec = _const_spec((1, RW_WIDTH))
    rw_out = pl.BlockSpec((n, RW_WIDTH), row)
    return pl.pallas_call(
        functools.partial(_inproj_body, seq_tiles=seq_len // n),
        grid=(m // n,),
        in_specs=[
            pl.BlockSpec((n, D_MODEL), row),
            _const_spec((1, D_MODEL)),
            _const_spec((D_MODEL, IN_PAD_WIDTH)),
            _const_spec((1, RW_PAD_WIDTH)),
            vec, _const_spec((LANES, RW_WIDTH)),
            vec, _const_spec((LANES, RW_WIDTH)),
            _const_spec((2 * LANES, RW_WIDTH)),
            vec, vec,
        ],
        out_specs=[pl.BlockSpec((n, SB_IN_WIDTH), row), pl.BlockSpec((n, SW_IN_WIDTH), row)]
        + [rw_out] * 7,
        out_shape=[jax.ShapeDtypeStruct((m, SB_IN_WIDTH), _MXU),
                   jax.ShapeDtypeStruct((m, SW_IN_WIDTH), _MXU)]
        + [jax.ShapeDtypeStruct((m, RW_WIDTH), _F32)] * 7,
        scratch_shapes=[pltpu.VMEM((8, RW_PAD_WIDTH), _F32)],
        compiler_params=_params("arbitrary"),
        name="inproj",
    )(x, g, w, mu, w0, wup, a0, aup, gup, k_k, k_a)


SB_SKIP_LOG = -88.0


def _sb_body(q_ref, kt_ref, v_ref, o_ref):
    t = SB_TILE
    i = pl.program_id(1)
    q = q_ref[0]
    lane_head = lax.broadcasted_iota(jnp.int32, (1, SB_WIDTH), 1) // HEAD_DIM
    row = lax.broadcasted_iota(jnp.int32, (t, t), 0)
    col = lax.broadcasted_iota(jnp.int32, (t, t), 1)
    later = jnp.where(row > col, 1.0, 0.0).astype(_MXU)
    causal = col < row
    scale = HEAD_DIM ** -0.5
    q_heads = [jnp.where(lane_head == h, q, jnp.zeros_like(q)) for h in range(SB_HEADS)]

    def block(jb, acc, carries, diagonal):
        ktb = kt_ref[0, jb]
        vb = v_ref[0, pl.ds(pl.multiple_of(jb * t, t), t), :]
        weights, new_carries = [], []
        for h in range(SB_HEADS):
            z = _dot(q_heads[h], ktb) * scale
            lom = -_softplus(z)
            if diagonal:
                lom = jnp.where(causal, lom, 0.0)
            hi = lom.astype(_MXU)
            lo = (lom - hi.astype(_F32)).astype(_MXU)
            tail = _dot(hi, later) + _dot(lo, later) + carries[h]
            a = jnp.exp(z + lom + tail)
            if diagonal:
                a = jnp.where(causal, a, 0.0)
            weights.append(a.astype(_MXU))
            new_carries.append(carries[h] + jnp.sum(lom, axis=1, keepdims=True))
        v_heads = jnp.concatenate(
            [jnp.where(lane_head == h, vb, jnp.zeros_like(vb)) for h in range(SB_HEADS)], axis=0)
        acc = acc + _dot(jnp.concatenate(weights, axis=1), v_heads)
        return acc, tuple(new_carries)

    zero = jnp.zeros((t, 1), _F32)
    acc, carries = block(i, jnp.zeros((t, SB_WIDTH), _F32), (zero,) * SB_HEADS, True)

    def live(state):
        jb, _, carries = state
        top = functools.reduce(jnp.maximum, carries)
        return (jb >= 0) & (jnp.max(top) > SB_SKIP_LOG)

    def step(state):
        jb, acc, carries = state
        acc, carries = block(jb, acc, carries, False)
        return jb - 1, acc, carries

    _, acc, _ = lax.while_loop(live, step, (i - 1, acc, carries))
    o_ref[0] = acc.astype(o_ref.dtype)


def _sb_attention(sb, kt):
    b, s, _ = sb.shape
    t = SB_TILE
    return pl.pallas_call(
        _sb_body,
        grid=(b, s // t),
        in_specs=[
            pl.BlockSpec((1, t, SB_WIDTH), lambda bi, i: (bi, i, 0)),
            pl.BlockSpec((1, s // t, SB_WIDTH, t), lambda bi, i: (bi, 0, 0, 0)),
            pl.BlockSpec((1, s, SB_WIDTH), lambda bi, i: (bi, 0, 2)),
        ],
        out_specs=pl.BlockSpec((1, t, SB_WIDTH), lambda bi, i: (bi, i, 0)),
        out_shape=jax.ShapeDtypeStruct((b, s, SB_WIDTH), _MXU),
        compiler_params=_params("parallel", "arbitrary"),
        name="sb_attn",
    )(sb, kt, sb)


def _sw_body(sinks_ref, q_ref, kp_ref, kc_ref, vp_ref, vc_ref, o_ref):
    t = SW_TILE
    assert t == WINDOW
    i = pl.program_id(1)
    kcat = jnp.concatenate([kp_ref[0], kc_ref[0]], axis=0)
    vcat = jnp.concatenate([vp_ref[0], vc_ref[0]], axis=0)
    r_i = lax.broadcasted_iota(jnp.int32, (t, t), 0)
    c_i = lax.broadcasted_iota(jnp.int32, (t, t), 1)
    own = c_i <= r_i
    distf = jnp.where(own, r_i - c_i, r_i - c_i + t).astype(_F32)
    lane_kv = lax.broadcasted_iota(jnp.int32, (1, LANES), 1) // HEAD_DIM
    scale = HEAD_DIM ** -0.5
    grp = SW_HEADS // SW_KV_HEADS
    for j in range(SW_STEP_TILES):
        rows = slice(j * t, (j + 1) * t)
        kj, vj = kcat[j * t:(j + 2) * t], vcat[j * t:(j + 2) * t]
        valid = own | (i * SW_STEP_TILES + j > 0)
        for g in range(grp):
            qp = q_ref[0, rows, g * LANES:(g + 1) * LANES]
            res = None
            for kv in range(SW_KV_HEADS):
                head = kv * grp + g
                slope = 2.0 ** (-8.0 * (head + 1.0) / SW_HEADS)
                qh = jnp.where(lane_kv == kv, qp, jnp.zeros_like(qp))
                zz = _dot_nt(qh, kj)
                z = jnp.where(own, zz[:, t:], zz[:, :t]) * scale - slope * distf
                z = jnp.where(valid, z, -1e30)
                sink = sinks_ref[head]
                m = jnp.maximum(jnp.max(z, axis=1, keepdims=True), sink)
                p = jnp.where(valid, jnp.exp(z - m), 0.0)
                denom = jnp.sum(p, axis=1, keepdims=True) + jnp.exp(sink - m)
                p = p.astype(_MXU)
                zero = jnp.zeros_like(p)
                pp = jnp.concatenate([jnp.where(own, zero, p), jnp.where(own, p, zero)], axis=1)
                o = _dot(pp, vj) / denom
                res = o if kv == 0 else jnp.where(lane_kv == 0, res, o)
            o_ref[0, rows, g * LANES:(g + 1) * LANES] = res.astype(o_ref.dtype)


def _sw_attention(sw, sinks):
    b, s, _ = sw.shape
    t = SW_TILE
    n = SW_STEP_TILES * t
    qb = SW_WIDTH // LANES
    cur = lambda c: (lambda bi, i: (bi, i, c))
    prev = lambda c: (lambda bi, i: (bi, jnp.maximum(i * SW_STEP_TILES - 1, 0), c))
    return pl.pallas_call(
        _sw_body,
        grid=(b, s // n),
        in_specs=[
            pl.BlockSpec(memory_space=pltpu.SMEM),
            pl.BlockSpec((1, n, SW_WIDTH), lambda bi, i: (bi, i, 0)),
            pl.BlockSpec((1, t, LANES), prev(qb)),
            pl.BlockSpec((1, n, LANES), cur(qb)),
            pl.BlockSpec((1, t, LANES), prev(qb + 1)),
            pl.BlockSpec((1, n, LANES), cur(qb + 1)),
        ],
        out_specs=pl.BlockSpec((1, n, SW_WIDTH), lambda bi, i: (bi, i, 0)),
        out_shape=jax.ShapeDtypeStruct((b, s, SW_WIDTH), _MXU),
        compiler_params=_params("parallel", "arbitrary"),
        name="sw_attn",
    )(sinks, sw, sw, sw, sw, sw)


def _rw_mm(a, b):
    return _dot(a.astype(_MXU), b.astype(_MXU))


def _rw_mm_nt(a, b):
    return _dot_nt(a.astype(_MXU), b.astype(_MXU))


def _head_stack(x, lane_head):
    return jnp.concatenate(
        [jnp.where(lane_head == h, x, 0.0) for h in range(RW_HEADS)], axis=0)


def _rw_chunk_body(r_ref, kp_ref, v_ref, kk_ref, b_ref, lw_ref,
                   w1r_o, w2_o, mkv_o, mb_o, gt_o, pct_o):
    cl = RW_CHUNK
    chunks = range(RW_OPS_CHUNKS)
    rows = [slice(c * cl, (c + 1) * cl) for c in chunks]
    each = lambda f, *xs: [f(*a) for a in zip(*xs)]
    load = lambda ref: [ref[0, rw] for rw in rows]
    r, kp, v, kk, b, lw = (load(x) for x in (r_ref, kp_ref, v_ref, kk_ref, b_ref, lw_ref))

    lane_head = lax.broadcasted_iota(jnp.int32, (1, RW_WIDTH), 1) // HEAD_DIM
    tr = lax.broadcasted_iota(jnp.int32, (cl, cl), 0)
    tc = lax.broadcasted_iota(jnp.int32, (cl, cl), 1)
    lower = jnp.where(tr >= tc, 1.0, 0.0)
    cum = each(lambda x: _dot(lower, x, _HI), lw)
    cum_end = each(lambda x: x[cl - 1:cl, :], cum)
    kk_t = each(lambda x, c, l: x * jnp.exp(c - l), kk, cum, lw)
    r_t = each(lambda x, c: x * jnp.exp(c), r, cum)
    inv = each(lambda c: jnp.exp(-c), cum)
    k_h = each(jnp.multiply, kp, inv)
    b_h = each(jnp.multiply, b, inv)
    to_end = each(lambda e, c: jnp.exp(e - c), cum_end, cum)
    k_end = each(jnp.multiply, kp, to_end)
    b_end = each(jnp.multiply, b, to_end)

    stack = lambda x: _head_stack(x, lane_head)
    khs, bhs, vs = each(stack, k_h), each(stack, b_h), each(stack, v)
    t_i = lax.broadcasted_iota(jnp.int32, (cl, RW_WIDTH), 0)
    s_i = lax.broadcasted_iota(jnp.int32, (cl, RW_WIDTH), 1) % cl
    strict, incl = t_i > s_i, t_i >= s_i
    lhs = each(lambda x, y: jnp.concatenate([x, y], axis=0), kk_t, r_t)
    pk = each(_rw_mm_nt, lhs, khs)
    pb = each(_rw_mm_nt, lhs, bhs)
    a_b = each(lambda p: jnp.where(strict, p[:cl], 0.0), pb)
    m_b = each(lambda p: jnp.where(incl, p[cl:], 0.0), pb)
    akm = each(lambda p: jnp.concatenate(
        [jnp.where(strict, p[:cl], 0.0), jnp.where(incl, p[cl:], 0.0)], axis=0), pk)
    kv = each(_rw_mm, akm, vs)

    same_head = _head_block_ones()

    def block_diag(x_cat):
        return jnp.where(same_head, jnp.concatenate([x_cat] * RW_HEADS, axis=0), 0.0)

    eye = jnp.where(t_i == s_i, 1.0, 0.0)
    pair = ((t_i ^ s_i) == 1) & ((t_i & 1) != 0)
    t_cat = each(lambda a: eye - jnp.where(pair, a, 0.0), a_b)
    m = 2
    while m < cl:
        below = (((t_i ^ s_i) & -(2 * m)) == 0) & ((t_i & m) != 0) & ((s_i & m) == 0)
        x = each(lambda t, a: _rw_mm(t, block_diag(jnp.where(below, a, 0.0))), t_cat, a_b)
        t_cat = each(lambda t, xx: t - _rw_mm(xx, block_diag(t)), t_cat, x)
        m *= 2
    w = each(lambda t, k, p: _rw_mm(t, jnp.concatenate([stack(k), stack(p[:cl])], axis=1)),
             t_cat, kk_t, kv)

    for c in chunks:
        wide = slice(c * RW_WIDTH, (c + 1) * RW_WIDTH)
        w1r_o[0, 2 * c * cl:2 * (c + 1) * cl] = jnp.concatenate(
            [w[c][:, :RW_WIDTH], r_t[c]], axis=0).astype(w1r_o.dtype)
        w2_o[0, rows[c]] = w[c][:, RW_WIDTH:]
        mkv_o[0, rows[c]] = kv[c][cl:]
        mb_o[0, rows[c]] = m_b[c].astype(mb_o.dtype)
        gt_o[0, wide] = jnp.concatenate([k_end[c], -b_end[c]], axis=0).T.astype(gt_o.dtype)
        pct_o[0, wide] = jnp.broadcast_to(jnp.exp(cum_end[c]), (2 * cl, RW_WIDTH)).T


def _rw_chunk(r, kp, v, kk, b, lw):
    bsz, s, _ = r.shape
    n = RW_OPS_CHUNKS * RW_CHUNK
    nc = s // RW_CHUNK
    spec = lambda rows, cols: pl.BlockSpec((1, rows, cols), lambda bi, i: (bi, i, 0))
    return pl.pallas_call(
        _rw_chunk_body,
        grid=(bsz, s // n),
        in_specs=[spec(n, RW_WIDTH)] * 6,
        out_specs=[spec(2 * n, RW_WIDTH), spec(n, RW_WIDTH), spec(n, RW_WIDTH), spec(n, RW_WIDTH),
                   spec(RW_OPS_CHUNKS * RW_WIDTH, 2 * RW_CHUNK),
                   spec(RW_OPS_CHUNKS * RW_WIDTH, 2 * RW_CHUNK)],
        out_shape=[
            jax.ShapeDtypeStruct((bsz, 2 * s, RW_WIDTH), _MXU),
            jax.ShapeDtypeStruct((bsz, s, RW_WIDTH), _F32),
            jax.ShapeDtypeStruct((bsz, s, RW_WIDTH), _F32),
            jax.ShapeDtypeStruct((bsz, s, RW_WIDTH), _MXU),
            jax.ShapeDtypeStruct((bsz, nc * RW_WIDTH, 2 * RW_CHUNK), _MXU),
            jax.ShapeDtypeStruct((bsz, nc * RW_WIDTH, 2 * RW_CHUNK), _F32),
        ],
        compiler_params=_params("parallel", "parallel"),
        name="rw_chunk",
    )(r, kp, v, kk, b, lw)


def _rw_scan_body(w1r_ref, w2_ref, mkv_ref, mb_ref, gt_ref, pct_ref, v_ref, r_ref, kp_ref, g_ref,
                  gng_ref, gnb_ref, rk_ref, o_ref, h_ref, y_ref):
    cl = RW_CHUNK

    @pl.when(pl.program_id(0) == 0)
    def _():
        h_ref[...] = jnp.zeros_like(h_ref)

    lane_head = lax.broadcasted_iota(jnp.int32, (1, RW_WIDTH), 1) // HEAD_DIM
    same_head = _head_block_ones()
    batch = range(h_ref.shape[0])
    for c in range(RW_STEP_CHUNKS):
        rows = slice(c * cl, (c + 1) * cl)
        wide = slice(c * RW_WIDTH, (c + 1) * RW_WIDTH)
        h = [h_ref[bi] for bi in batch]
        uy = [_rw_mm(w1r_ref[bi, 2 * c * cl:2 * (c + 1) * cl], h[bi]) for bi in batch]
        u = [uy[bi][:cl] + w2_ref[bi, rows] for bi in batch]
        vu = [jnp.concatenate([v_ref[bi, rows], u[bi]], axis=0) for bi in batch]
        upd = [_rw_mm(gt_ref[bi, wide], vu[bi]) for bi in batch]
        for bi in batch:
            pct = pct_ref[bi, wide]
            h_ref[bi] = (jnp.concatenate([pct, pct], axis=1) * h[bi]
                         + jnp.where(same_head, upd[bi], 0.0))
        for bi in batch:
            y_ref[bi, rows] = (uy[bi][cl:] + mkv_ref[bi, rows]
                               - _rw_mm(mb_ref[bi, rows], _head_stack(u[bi], lane_head)))

    mean_mat = jnp.where(same_head, 1.0 / HEAD_DIM, 0.0).astype(_MXU)
    ones = jnp.where(same_head, 1.0, 0.0).astype(_MXU)
    for bi in batch:
        y = y_ref[bi]
        d = y - _head_sum(y, mean_mat)
        var = _head_sum(d * d, mean_mat)
        yn = d * lax.rsqrt(var + RW_GN_EPS) * gng_ref[...] + gnb_ref[...]
        bonus = _head_sum(r_ref[bi] * kp_ref[bi] * rk_ref[...], ones) * v_ref[bi]
        o_ref[bi] = ((yn + bonus) * g_ref[bi]).astype(o_ref.dtype)


def _rw_scan(w1r, w2, mkv, mb, gt, pct, v, r, kp, g, gn_g, gn_b, r_k):
    bsz, s, _ = v.shape
    n = RW_STEP_CHUNKS * RW_CHUNK
    spec = lambda rows, cols: pl.BlockSpec((bsz, rows, cols), lambda i: (0, i, 0))
    tall = spec(RW_STEP_CHUNKS * RW_WIDTH, 2 * RW_CHUNK)
    blk = spec(n, RW_WIDTH)
    vec = _const_spec((1, RW_WIDTH))
    return pl.pallas_call(
        _rw_scan_body,
        grid=(s // n,),
        in_specs=[spec(2 * n, RW_WIDTH), blk, blk, blk, tall, tall, blk, blk, blk, blk,
                  vec, vec, vec],
        out_specs=blk,
        out_shape=jax.ShapeDtypeStruct((bsz, s, RW_WIDTH), _MXU),
        scratch_shapes=[pltpu.VMEM((bsz, RW_WIDTH, RW_WIDTH), _F32),
                        pltpu.VMEM((bsz, n, RW_WIDTH), _F32)],
        compiler_params=_params("arbitrary"),
        name="rw_scan",
    )(w1r, w2, mkv, mb, gt, pct, v, r, kp, g, gn_g, gn_b, r_k)


def _rwkv7(r, kp, v, kk, b, lw, g, r_k, gn_g, gn_b):
    w1r, w2, mkv, mb, gt, pct = _rw_chunk(r, kp, v, kk, b, lw)
    return _rw_scan(w1r, w2, mkv, mb, gt, pct, v, r, kp, g, gn_g, gn_b, r_k)


_SW_PAIR_ORDER = tuple(h for g in range(SW_HEADS // SW_KV_HEADS)
                       for h in (g, g + SW_HEADS // SW_KV_HEADS))


def _pad_cols(w, width):
    return jnp.pad(w, ((0, 0), (0, width - w.shape[1])))


def _pad_rows(w, rows):
    return jnp.pad(w, ((0, rows - w.shape[0]), (0, 0)))


def _pair_heads(w, axis):
    shape = w.shape
    w = w.reshape(shape[:axis] + (SW_HEADS, HEAD_DIM) + shape[axis + 1:])
    w = jnp.take(w, jnp.array(_SW_PAIR_ORDER), axis=axis)
    return w.reshape(shape)


def _mix_in_layout(w_in, mu):
    o = 0
    sb = w_in[:, o:o + SB_IN_WIDTH]; o += SB_IN_WIDTH
    rkv = w_in[:, o:o + 3 * RW_WIDTH]; o += 3 * RW_WIDTH
    xw = w_in[:, o:o + DECAY_LORA]; o += DECAY_LORA
    xa = w_in[:, o:o + AAA_LORA]; o += AAA_LORA
    xg = w_in[:, o:o + GATE_LORA]; o += GATE_LORA
    swq = w_in[:, o:o + SW_WIDTH]; o += SW_WIDTH
    swkv = w_in[:, o:]
    w = jnp.concatenate([
        sb, rkv, _pad_cols(xw, LANES), _pad_cols(xa, LANES), _pad_cols(xg, 2 * LANES),
        _pair_heads(swq, 1), swkv], axis=1).astype(_MXU)
    m = mu[None, :]
    c = 3 * RW_WIDTH
    mu_p = jnp.concatenate([
        m[:, :c], _pad_cols(m[:, c:c + DECAY_LORA], LANES),
        _pad_cols(m[:, c + DECAY_LORA:c + DECAY_LORA + AAA_LORA], LANES),
        _pad_cols(m[:, c + DECAY_LORA + AAA_LORA:], 2 * LANES)], axis=1)
    return w, mu_p


def kernel(x, ffn1_norm, ffn1_w_in, ffn1_w_out, mix_norm, mix_w_in, mix_w_out, rw_mu, rw_w0,
           rw_w_up, rw_a0, rw_a_up, rw_g_up, rw_k_k, rw_k_a, rw_r_k, rw_gn_g, rw_gn_b, sw_sinks,
           ffn2_norm, ffn2_w_in, ffn2_w_out, final_norm):
    bsz, s, d = x.shape
    xf = x.reshape(bsz * s, d)
    row = lambda t: t[None, :]
    final_g = row(final_norm)
    for l in range(DEPTH):
        xf = _ffn(xf, row(ffn1_norm[l]), ffn1_w_in[l].astype(_MXU), ffn1_w_out[l].astype(_MXU),
                  final_g, False)
        w_in, mu = _mix_in_layout(mix_w_in[l], rw_mu[l])
        sb, sw, *rw_parts = _inproj(
            xf, row(mix_norm[l]), w_in, mu, row(rw_w0[l]), _pad_rows(rw_w_up[l], LANES),
            row(rw_a0[l]), _pad_rows(rw_a_up[l], LANES), _pad_rows(rw_g_up[l], 2 * LANES),
            row(rw_k_k[l]), row(rw_k_a[l]), s)
        sb = sb.reshape(bsz, s, SB_IN_WIDTH)
        sw = sw.reshape(bsz, s, SW_IN_WIDTH)
        kt = sb[:, :, SB_WIDTH:2 * SB_WIDTH].reshape(bsz, s // SB_TILE, SB_TILE, SB_WIDTH)
        kt = jnp.swapaxes(kt, 2, 3)
        sb_out = _sb_attention(sb, kt)
        rw_out = _rwkv7(
            *(t.reshape(bsz, s, RW_WIDTH) for t in rw_parts),
            rw_r_k[l].reshape(1, RW_WIDTH), row(rw_gn_g[l]), row(rw_gn_b[l]))
        sw_out = _sw_attention(sw, sw_sinks[l])
        w_out = mix_w_out[l].astype(_MXU)
        mix = (sb_out.reshape(bsz * s, SB_WIDTH), rw_out.reshape(bsz * s, RW_WIDTH),
               sw_out.reshape(bsz * s, SW_WIDTH), w_out[:SB_WIDTH],
               w_out[SB_WIDTH:SB_WIDTH + RW_WIDTH], _pair_heads(w_out[SB_WIDTH + RW_WIDTH:], 0))
        xf = _ffn(xf, row(ffn2_norm[l]), ffn2_w_in[l].astype(_MXU), ffn2_w_out[l].astype(_MXU),
                  final_g, l == DEPTH - 1, mix)
    return xf.reshape(bsz, s, d)
```

```python
import functools

import jax
import jax.numpy as jnp
from jax import lax
from jax.experimental import pallas as pl
from jax.experimental.pallas import tpu as pltpu

D_MODEL = 1024
DEPTH = 4
HEAD_DIM = 64
SB_HEADS = 4
SB_WIDTH = SB_HEADS * HEAD_DIM
RW_HEADS = 4
RW_WIDTH = RW_HEADS * HEAD_DIM
DECAY_LORA = 64
AAA_LORA = 64
GATE_LORA = 160
RW_GN_EPS = 64e-5
SW_HEADS = 8
SW_KV_HEADS = 2
SW_WIDTH = SW_HEADS * HEAD_DIM
SW_KV_WIDTH = SW_KV_HEADS * HEAD_DIM
WINDOW = 128
D_FF = 2816
NORM_EPS = 1e-6

LANES = 128
MXU_DIM = 256
RW_PAD_WIDTH = 3 * RW_WIDTH + 2 * LANES + 2 * LANES
SB_IN_WIDTH = 3 * SB_WIDTH
SW_IN_WIDTH = SW_WIDTH + 2 * SW_KV_WIDTH
IN_PAD_WIDTH = SB_IN_WIDTH + RW_PAD_WIDTH + SW_IN_WIDTH

FFN_ROWS = 1024
PROJ_ROWS = 1024
PROJ_SUB = 256
FF_TILE = MXU_DIM
SB_TILE = 256
SW_TILE = WINDOW
SW_STEP_TILES = 2
RW_CHUNK = 64
RW_OPS_CHUNKS = 8
RW_STEP_CHUNKS = 4

VMEM_LIMIT_BYTES = 56 * 1024 * 1024

_MXU = jnp.bfloat16
_F32 = jnp.float32
_HI = lax.Precision.HIGHEST


def _dot(a, b, precision=None):
    return jnp.dot(a, b, preferred_element_type=_F32, precision=precision)


def _dot_nt(a, b, precision=None):
    return lax.dot_general(a, b, (((1,), (1,)), ((), ())),
                           preferred_element_type=_F32, precision=precision)


def _sigmoid(x):
    return 1.0 / (1.0 + jnp.exp(-x))


def _softplus(x):
    return jnp.maximum(x, 0.0) + jnp.log(1.0 + jnp.exp(-jnp.abs(x)))


def _params(*sem):
    return pltpu.CompilerParams(dimension_semantics=sem, vmem_limit_bytes=VMEM_LIMIT_BYTES)


def _const_spec(shape):
    nd = len(shape)
    return pl.BlockSpec(shape, lambda *_: (0,) * nd, pipeline_mode=pl.Buffered(1))


def _layer_spec(shape, layer, row_block=0):
    return pl.BlockSpec((None,) + shape, lambda *_: (layer, row_block, 0),
                        pipeline_mode=pl.Buffered(1))


def _ffn_body(*refs, final, mixed):
    if mixed:
        (x_ref, sb_ref, rw_ref, sw_ref, wsb_ref, wrw_ref, wsw_ref,
         g_ref, win_ref, wout_ref, fg_ref, o_ref, act_ref) = refs
        x = (x_ref[...] + _dot(sb_ref[...], wsb_ref[...]) + _dot(rw_ref[...], wrw_ref[...])
             + _dot(sw_ref[...], wsw_ref[...]))
    else:
        x_ref, g_ref, win_ref, wout_ref, fg_ref, o_ref, act_ref = refs
        x = x_ref[...]
    ms = jnp.mean(x * x, axis=-1, keepdims=True)
    hn = (x * lax.rsqrt(ms + NORM_EPS) * g_ref[...]).astype(_MXU)
    for c in range(D_FF // FF_TILE):
        lo, hi = c * FF_TILE, (c + 1) * FF_TILE
        gate = _dot(hn, win_ref[:, lo:hi])
        up = _dot(hn, win_ref[:, D_FF + lo:D_FF + hi])
        act_ref[:, lo:hi] = (gate * _sigmoid(gate) * up).astype(_MXU)
    y = x + 0.5 * _dot(act_ref[...], wout_ref[...])
    if final:
        ms = jnp.mean(y * y, axis=-1, keepdims=True)
        y = y * lax.rsqrt(ms + NORM_EPS) * fg_ref[...]
    o_ref[...] = y


def _ffn(x, g, w_in, w_out, final_g, final, layer, mix=None):
    m = x.shape[0]
    row = lambda i: (i, 0)
    mix_args, mix_specs = (), []
    if mix is not None:
        sb, rw, sw, w_mix = mix
        mix_args = (sb, rw, sw, w_mix, w_mix, w_mix)
        mix_specs = [pl.BlockSpec((FFN_ROWS, t.shape[1]), row) for t in (sb, rw, sw)]
        mix_specs += [_layer_spec((SB_WIDTH, D_MODEL), layer, 0),
                      _layer_spec((RW_WIDTH, D_MODEL), layer, SB_WIDTH // RW_WIDTH),
                      _layer_spec((SW_WIDTH, D_MODEL), layer, (SB_WIDTH + RW_WIDTH) // SW_WIDTH)]
    return pl.pallas_call(
        functools.partial(_ffn_body, final=final, mixed=mix is not None),
        grid=(m // FFN_ROWS,),
        in_specs=[pl.BlockSpec((FFN_ROWS, D_MODEL), row)] + mix_specs + [
            _const_spec((1, D_MODEL)),
            _layer_spec((D_MODEL, 2 * D_FF), layer),
            _layer_spec((D_FF, D_MODEL), layer),
            _const_spec((1, D_MODEL)),
        ],
        out_specs=pl.BlockSpec((FFN_ROWS, D_MODEL), row),
        out_shape=jax.ShapeDtypeStruct((m, D_MODEL), _F32),
        scratch_shapes=[pltpu.VMEM((FFN_ROWS, D_FF), _MXU)],
        compiler_params=_params("parallel"),
        name="ffn_mix" if mix is not None else "ffn",
    )(x, *mix_args, g, w_in, w_out, final_g)


def _head_block_ones():
    r = lax.broadcasted_iota(jnp.int32, (RW_WIDTH, RW_WIDTH), 0) // HEAD_DIM
    c = lax.broadcasted_iota(jnp.int32, (RW_WIDTH, RW_WIDTH), 1) // HEAD_DIM
    return r == c


def _head_sum(x, mat):
    hi = x.astype(_MXU)
    lo = (x - hi.astype(_F32)).astype(_MXU)
    return _dot(hi, mat) + _dot(lo, mat)


def _inproj_body(x_ref, g_ref, w_ref, mu_ref, w0_ref, wup_ref, a0_ref, aup_ref, gup_ref,
                 kk_ref, ka_ref, sb_ref, kt_ref, sw_ref, r_o, kp_o, v_o, kkn_o, b_o, lw_o, g_o,
                 last_ref, *, seq_tiles):
    i = pl.program_id(0)
    sub = PROJ_SUB
    assert sub == SB_TILE
    n_sub = x_ref.shape[0] // sub
    c = RW_WIDTH
    ones = jnp.where(_head_block_ones(), 1.0, 0.0).astype(_MXU)
    first = lax.broadcasted_iota(jnp.int32, (sub, 1), 0) == 0

    @pl.when(i == 0)
    def _():
        last_ref[...] = jnp.zeros_like(last_ref)

    def project(j):
        rows = slice(j * sub, (j + 1) * sub)
        x = x_ref[rows]
        ms = jnp.mean(x * x, axis=-1, keepdims=True)
        hn = (x * lax.rsqrt(ms + NORM_EPS) * g_ref[...]).astype(_MXU)
        sb = _dot(hn, w_ref[:, :SB_IN_WIDTH])
        sb_ref[rows] = sb.astype(sb_ref.dtype)
        kt_ref[j] = sb[:, SB_WIDTH:2 * SB_WIDTH].T.astype(kt_ref.dtype)
        sw_ref[rows] = _dot(hn, w_ref[:, SB_IN_WIDTH + RW_PAD_WIDTH:]).astype(sw_ref.dtype)
        return _dot(hn, w_ref[:, SB_IN_WIDTH:SB_IN_WIDTH + RW_PAD_WIDTH])

    def prepare(j, p, prev_last):
        rows = slice(j * sub, (j + 1) * sub)
        pprev = jnp.where(first, prev_last, pltpu.roll(p, 1, 0))
        xm = p + (pprev - p) * mu_ref[...]
        r, k, v = xm[:, :c], xm[:, c:2 * c], xm[:, 2 * c:3 * c]
        xw = xm[:, 3 * c:3 * c + LANES]
        xa = xm[:, 3 * c + LANES:3 * c + 2 * LANES]
        xg = xm[:, 3 * c + 2 * LANES:]
        log_w = -_softplus(-(w0_ref[...] + _rw_mm(jnp.tanh(xw), wup_ref[...]))) - 0.5
        a = _sigmoid(a0_ref[...] + _rw_mm(xa, aup_ref[...]))
        g = _rw_mm(_sigmoid(xg), gup_ref[...])
        kk = k * kk_ref[...]
        n2 = _head_sum(kk * kk, ones)
        kk = kk * lax.rsqrt(jnp.maximum(n2, 1e-24))
        r_o[rows] = r
        kp_o[rows] = k * (1.0 + (a - 1.0) * ka_ref[...])
        v_o[rows] = v
        kkn_o[rows] = kk
        b_o[rows] = kk * a
        lw_o[rows] = -jnp.exp(log_w)
        g_o[rows] = g

    prev_last = jnp.where(i % seq_tiles == 0, 0.0, last_ref[7:8, :])
    ps = [project(0)]
    for j in range(n_sub):
        if j + 1 < n_sub:
            ps.append(project(j + 1))
        prepare(j, ps[j], prev_last)
        prev_last = ps[j][sub - 1:sub, :]
    last_ref[...] = ps[-1][sub - 8:, :]


def _inproj(x, g, w, mu, w0, wup, a0, aup, gup, k_k, k_a, seq_len, layer):
    m = x.shape[0]
    n = PROJ_ROWS
    row = lambda i: (i, 0)
    vec = _const_spec((1, RW_WIDTH))
    rw_out = pl.BlockSpec((n, RW_WIDTH), row)
    return pl.pallas_call(
        functools.partial(_inproj_body, seq_tiles=seq_len // n),
        grid=(m // n,),
        in_specs=[
            pl.BlockSpec((n, D_MODEL), row),
            _const_spec((1, D_MODEL)),
            _layer_spec((D_MODEL, IN_PAD_WIDTH), layer),
            _const_spec((1, RW_PAD_WIDTH)),
            vec, _const_spec((LANES, RW_WIDTH)),
            vec, _const_spec((LANES, RW_WIDTH)),
            _const_spec((2 * LANES, RW_WIDTH)),
            vec, vec,
        ],
        out_specs=[pl.BlockSpec((n, SB_IN_WIDTH), row),
                   pl.BlockSpec((n // SB_TILE, SB_WIDTH, SB_TILE), lambda i: (i, 0, 0)),
                   pl.BlockSpec((n, SW_IN_WIDTH), row)]
        + [rw_out] * 7,
        out_shape=[jax.ShapeDtypeStruct((m, SB_IN_WIDTH), _MXU),
                   jax.ShapeDtypeStruct((m // SB_TILE, SB_WIDTH, SB_TILE), _MXU),
                   jax.ShapeDtypeStruct((m, SW_IN_WIDTH), _MXU)]
        + [jax.ShapeDtypeStruct((m, RW_WIDTH), _F32)] * 7,
        scratch_shapes=[pltpu.VMEM((8, RW_PAD_WIDTH), _F32)],
        compiler_params=_params("arbitrary"),
        name="inproj",
    )(x, g, w, mu, w0, wup, a0, aup, gup, k_k, k_a)


SB_SKIP_LOG = -88.0


def _sb_body(q_ref, kt_ref, v_ref, o_ref):
    t = SB_TILE
    i = pl.program_id(1)
    q = q_ref[0]
    lane_head = lax.broadcasted_iota(jnp.int32, (1, SB_WIDTH), 1) // HEAD_DIM
    row = lax.broadcasted_iota(jnp.int32, (t, t), 0)
    col = lax.broadcasted_iota(jnp.int32, (t, t), 1)
    later = jnp.where(row > col, 1.0, 0.0).astype(_MXU)
    later2 = jnp.concatenate([later, later], axis=0)
    causal = col < row
    q = q * (HEAD_DIM ** -0.5)
    q_heads = [jnp.where(lane_head == h, q, jnp.zeros_like(q)) for h in range(SB_HEADS)]

    def block(jb, acc, carries, diagonal):
        ktb = kt_ref[0, jb]
        vb = v_ref[0, pl.ds(pl.multiple_of(jb * t, t), t), :]
        weights, new_carries = [], []
        for h in range(SB_HEADS):
            z = _dot(q_heads[h], ktb)
            lom = -_softplus(z)
            if diagonal:
                lom = jnp.where(causal, lom, 0.0)
            hi = lom.astype(_MXU)
            lo = (lom - hi.astype(_F32)).astype(_MXU)
            tail = _dot(jnp.concatenate([hi, lo], axis=1), later2) + carries[h]
            a = jnp.exp(z + lom + tail)
            if diagonal:
                a = jnp.where(causal, a, 0.0)
            weights.append(a.astype(_MXU))
            new_carries.append(carries[h] + jnp.sum(lom, axis=1, keepdims=True))
        v_heads = jnp.concatenate(
            [jnp.where(lane_head == h, vb, jnp.zeros_like(vb)) for h in range(SB_HEADS)], axis=0)
        acc = acc + _dot(jnp.concatenate(weights, axis=1), v_heads)
        return acc, tuple(new_carries)

    zero = jnp.zeros((t, 1), _F32)
    acc, carries = block(i, jnp.zeros((t, SB_WIDTH), _F32), (zero,) * SB_HEADS, True)

    def live(state):
        jb, _, carries = state
        top = functools.reduce(jnp.maximum, carries)
        return (jb >= 0) & (jnp.max(top) > SB_SKIP_LOG)

    def step(state):
        jb, acc, carries = state
        acc, carries = block(jb, acc, carries, False)
        return jb - 1, acc, carries

    _, acc, _ = lax.while_loop(live, step, (i - 1, acc, carries))
    o_ref[0] = acc.astype(o_ref.dtype)


def _sb_attention(sb, kt):
    b, s, _ = sb.shape
    t = SB_TILE
    return pl.pallas_call(
        _sb_body,
        grid=(b, s // t),
        in_specs=[
            pl.BlockSpec((1, t, SB_WIDTH), lambda bi, i: (bi, i, 0)),
            pl.BlockSpec((1, s // t, SB_WIDTH, t), lambda bi, i: (bi, 0, 0, 0)),
            pl.BlockSpec((1, s, SB_WIDTH), lambda bi, i: (bi, 0, 2)),
        ],
        out_specs=pl.BlockSpec((1, t, SB_WIDTH), lambda bi, i: (bi, i, 0)),
        out_shape=jax.ShapeDtypeStruct((b, s, SB_WIDTH), _MXU),
        compiler_params=_params("parallel", "arbitrary"),
        name="sb_attn",
    )(sb, kt, sb)


def _sw_body(sinks_ref, q_ref, kp_ref, kc_ref, vp_ref, vc_ref, o_ref):
    t = SW_TILE
    assert t == WINDOW
    i = pl.program_id(1)
    kcat = jnp.concatenate([kp_ref[0], kc_ref[0]], axis=0)
    vcat = jnp.concatenate([vp_ref[0], vc_ref[0]], axis=0)
    r_i = lax.broadcasted_iota(jnp.int32, (t, t), 0)
    c_i = lax.broadcasted_iota(jnp.int32, (t, t), 1)
    own = c_i <= r_i
    distf = jnp.where(own, r_i - c_i, r_i - c_i + t).astype(_F32)
    lane_kv = lax.broadcasted_iota(jnp.int32, (1, LANES), 1) // HEAD_DIM
    scale = HEAD_DIM ** -0.5
    grp = SW_HEADS // SW_KV_HEADS
    for j in range(SW_STEP_TILES):
        rows = slice(j * t, (j + 1) * t)
        kj, vj = kcat[j * t:(j + 2) * t], vcat[j * t:(j + 2) * t]
        valid = own | (i * SW_STEP_TILES + j > 0)
        for g in range(grp):
            qp = q_ref[0, rows, g * LANES:(g + 1) * LANES] * scale
            res = None
            for kv in range(SW_KV_HEADS):
                head = kv * grp + g
                slope = 2.0 ** (-8.0 * (head + 1.0) / SW_HEADS)
                qh = jnp.where(lane_kv == kv, qp, jnp.zeros_like(qp))
                zz = _dot_nt(qh, kj)
                z = jnp.where(own, zz[:, t:], zz[:, :t]) - slope * distf
                z = jnp.where(valid, z, -1e30)
                sink = sinks_ref[head]
                m = jnp.maximum(jnp.max(z, axis=1, keepdims=True), sink)
                p = jnp.exp(z - m)
                denom = jnp.sum(p, axis=1, keepdims=True) + jnp.exp(sink - m)
                p = p.astype(_MXU)
                zero = jnp.zeros_like(p)
                pp = jnp.concatenate([jnp.where(own, zero, p), jnp.where(own, p, zero)], axis=1)
                o = _dot(pp, vj) / denom
                res = o if kv == 0 else jnp.where(lane_kv == 0, res, o)
            o_ref[0, rows, g * LANES:(g + 1) * LANES] = res.astype(o_ref.dtype)


def _sw_attention(sw, sinks):
    b, s, _ = sw.shape
    t = SW_TILE
    n = SW_STEP_TILES * t
    qb = SW_WIDTH // LANES
    cur = lambda c: (lambda bi, i: (bi, i, c))
    prev = lambda c: (lambda bi, i: (bi, jnp.maximum(i * SW_STEP_TILES - 1, 0), c))
    return pl.pallas_call(
        _sw_body,
        grid=(b, s // n),
        in_specs=[
            pl.BlockSpec(memory_space=pltpu.SMEM),
            pl.BlockSpec((1, n, SW_WIDTH), lambda bi, i: (bi, i, 0)),
            pl.BlockSpec((1, t, LANES), prev(qb)),
            pl.BlockSpec((1, n, LANES), cur(qb)),
            pl.BlockSpec((1, t, LANES), prev(qb + 1)),
            pl.BlockSpec((1, n, LANES), cur(qb + 1)),
        ],
        out_specs=pl.BlockSpec((1, n, SW_WIDTH), lambda bi, i: (bi, i, 0)),
        out_shape=jax.ShapeDtypeStruct((b, s, SW_WIDTH), _MXU),
        compiler_params=_params("parallel", "arbitrary"),
        name="sw_attn",
    )(sinks, sw, sw, sw, sw, sw)


def _rw_mm(a, b):
    return _dot(a.astype(_MXU), b.astype(_MXU))


def _rw_mm_nt(a, b):
    return _dot_nt(a.astype(_MXU), b.astype(_MXU))


def _head_stack(x, lane_head):
    return jnp.concatenate(
        [jnp.where(lane_head == h, x, 0.0) for h in range(RW_HEADS)], axis=0)


def _rw_chunk_body(r_ref, kp_ref, v_ref, kk_ref, b_ref, lw_ref,
                   w1r_o, w2_o, mkv_o, mb_o, gt_o, pct_o):
    cl = RW_CHUNK
    chunks = range(RW_OPS_CHUNKS)
    rows = [slice(c * cl, (c + 1) * cl) for c in chunks]
    each = lambda f, *xs: [f(*a) for a in zip(*xs)]
    load = lambda ref: [ref[0, rw] for rw in rows]
    r, kp, v, kk, b, lw = (load(x) for x in (r_ref, kp_ref, v_ref, kk_ref, b_ref, lw_ref))

    lane_head = lax.broadcasted_iota(jnp.int32, (1, RW_WIDTH), 1) // HEAD_DIM
    tr = lax.broadcasted_iota(jnp.int32, (cl, cl), 0)
    tc = lax.broadcasted_iota(jnp.int32, (cl, cl), 1)
    lower = jnp.where(tr >= tc, 1.0, 0.0)
    cum = each(lambda x: _dot(lower, x, _HI), lw)
    cum_end = each(lambda x: x[cl - 1:cl, :], cum)
    kk_t = each(lambda x, c, l: x * jnp.exp(c - l), kk, cum, lw)
    r_t = each(lambda x, c: x * jnp.exp(c), r, cum)
    inv = each(lambda c: jnp.exp(-c), cum)
    k_h = each(jnp.multiply, kp, inv)
    b_h = each(jnp.multiply, b, inv)
    to_end = each(lambda e, c: jnp.exp(e - c), cum_end, cum)
    k_end = each(jnp.multiply, kp, to_end)
    b_end = each(jnp.multiply, b, to_end)

    stack = lambda x: _head_stack(x, lane_head)
    khs, bhs, vs = each(stack, k_h), each(stack, b_h), each(stack, v)
    t_i = lax.broadcasted_iota(jnp.int32, (cl, RW_WIDTH), 0)
    s_i = lax.broadcasted_iota(jnp.int32, (cl, RW_WIDTH), 1) % cl
    strict, incl = t_i > s_i, t_i >= s_i
    lhs = each(lambda x, y: jnp.concatenate([x, y], axis=0), kk_t, r_t)
    pk = each(_rw_mm_nt, lhs, khs)
    pb = each(_rw_mm_nt, lhs, bhs)
    a_b = each(lambda p: jnp.where(strict, p[:cl], 0.0), pb)
    m_b = each(lambda p: jnp.where(incl, p[cl:], 0.0), pb)
    akm = each(lambda p: jnp.concatenate(
        [jnp.where(strict, p[:cl], 0.0), jnp.where(incl, p[cl:], 0.0)], axis=0), pk)
    kv = each(_rw_mm, akm, vs)

    same_head = _head_block_ones()

    def block_diag(x_cat):
        return jnp.where(same_head, jnp.concatenate([x_cat] * RW_HEADS, axis=0), 0.0)

    eye = jnp.where(t_i == s_i, 1.0, 0.0)
    pair = ((t_i ^ s_i) == 1) & ((t_i & 1) != 0)
    t_cat = each(lambda a: eye - jnp.where(pair, a, 0.0), a_b)
    m = 2
    while m < cl:
        below = (((t_i ^ s_i) & -(2 * m)) == 0) & ((t_i & m) != 0) & ((s_i & m) == 0)
        x = each(lambda t, a: _rw_mm(t, block_diag(jnp.where(below, a, 0.0))), t_cat, a_b)
        t_cat = each(lambda t, xx: t - _rw_mm(xx, block_diag(t)), t_cat, x)
        m *= 2
    w = each(lambda t, k, p: _rw_mm(t, jnp.concatenate([stack(k), stack(p[:cl])], axis=1)),
             t_cat, kk_t, kv)

    for c in chunks:
        wide = slice(c * RW_WIDTH, (c + 1) * RW_WIDTH)
        w1r_o[0, 2 * c * cl:2 * (c + 1) * cl] = jnp.concatenate(
            [w[c][:, :RW_WIDTH], r_t[c]], axis=0).astype(w1r_o.dtype)
        w2_o[0, rows[c]] = w[c][:, RW_WIDTH:]
        mkv_o[0, rows[c]] = kv[c][cl:]
        mb_o[0, rows[c]] = m_b[c].astype(mb_o.dtype)
        gt_o[0, wide] = jnp.concatenate([k_end[c], -b_end[c]], axis=0).T.astype(gt_o.dtype)
        pct_o[0, wide] = jnp.broadcast_to(jnp.exp(cum_end[c]), (2 * cl, RW_WIDTH)).T


def _rw_chunk(r, kp, v, kk, b, lw):
    bsz, s, _ = r.shape
    n = RW_OPS_CHUNKS * RW_CHUNK
    nc = s // RW_CHUNK
    spec = lambda rows, cols: pl.BlockSpec((1, rows, cols), lambda bi, i: (bi, i, 0))
    return pl.pallas_call(
        _rw_chunk_body,
        grid=(bsz, s // n),
        in_specs=[spec(n, RW_WIDTH)] * 6,
        out_specs=[spec(2 * n, RW_WIDTH), spec(n, RW_WIDTH), spec(n, RW_WIDTH), spec(n, RW_WIDTH),
                   spec(RW_OPS_CHUNKS * RW_WIDTH, 2 * RW_CHUNK),
                   spec(RW_OPS_CHUNKS * RW_WIDTH, 2 * RW_CHUNK)],
        out_shape=[
            jax.ShapeDtypeStruct((bsz, 2 * s, RW_WIDTH), _MXU),
            jax.ShapeDtypeStruct((bsz, s, RW_WIDTH), _F32),
            jax.ShapeDtypeStruct((bsz, s, RW_WIDTH), _F32),
            jax.ShapeDtypeStruct((bsz, s, RW_WIDTH), _MXU),
            jax.ShapeDtypeStruct((bsz, nc * RW_WIDTH, 2 * RW_CHUNK), _MXU),
            jax.ShapeDtypeStruct((bsz, nc * RW_WIDTH, 2 * RW_CHUNK), _F32),
        ],
        compiler_params=_params("parallel", "parallel"),
        name="rw_chunk",
    )(r, kp, v, kk, b, lw)


def _rw_scan_body(w1r_ref, w2_ref, mkv_ref, mb_ref, gt_ref, pct_ref, v_ref, r_ref, kp_ref, g_ref,
                  gng_ref, gnb_ref, rk_ref, o_ref, h_ref, y_ref):
    cl = RW_CHUNK

    @pl.when(pl.program_id(0) == 0)
    def _():
        h_ref[...] = jnp.zeros_like(h_ref)

    lane_head = lax.broadcasted_iota(jnp.int32, (1, RW_WIDTH), 1) // HEAD_DIM
    same_head = _head_block_ones()
    batch = range(h_ref.shape[0])
    for c in range(RW_STEP_CHUNKS):
        rows = slice(c * cl, (c + 1) * cl)
        wide = slice(c * RW_WIDTH, (c + 1) * RW_WIDTH)
        h = [h_ref[bi] for bi in batch]
        uy = [_rw_mm(w1r_ref[bi, 2 * c * cl:2 * (c + 1) * cl], h[bi]) for bi in batch]
        u = [uy[bi][:cl] + w2_ref[bi, rows] for bi in batch]
        vu = [jnp.concatenate([v_ref[bi, rows], u[bi]], axis=0) for bi in batch]
        upd = [_rw_mm(gt_ref[bi, wide], vu[bi]) for bi in batch]
        for bi in batch:
            pct = pct_ref[bi, wide]
            h_ref[bi] = (jnp.concatenate([pct, pct], axis=1) * h[bi]
                         + jnp.where(same_head, upd[bi], 0.0))
        for bi in batch:
            y_ref[bi, rows] = (uy[bi][cl:] + mkv_ref[bi, rows]
                               - _rw_mm(mb_ref[bi, rows], _head_stack(u[bi], lane_head)))

    mean_mat = jnp.where(same_head, 1.0 / HEAD_DIM, 0.0).astype(_MXU)
    ones = jnp.where(same_head, 1.0, 0.0).astype(_MXU)
    for bi in batch:
        y = y_ref[bi]
        d = y - _head_sum(y, mean_mat)
        var = _head_sum(d * d, mean_mat)
        yn = d * lax.rsqrt(var + RW_GN_EPS) * gng_ref[...] + gnb_ref[...]
        bonus = _head_sum(r_ref[bi] * kp_ref[bi] * rk_ref[...], ones) * v_ref[bi]
        o_ref[bi] = ((yn + bonus) * g_ref[bi]).astype(o_ref.dtype)


def _rw_scan(w1r, w2, mkv, mb, gt, pct, v, r, kp, g, gn_g, gn_b, r_k):
    bsz, s, _ = v.shape
    n = RW_STEP_CHUNKS * RW_CHUNK
    spec = lambda rows, cols: pl.BlockSpec((bsz, rows, cols), lambda i: (0, i, 0))
    tall = spec(RW_STEP_CHUNKS * RW_WIDTH, 2 * RW_CHUNK)
    blk = spec(n, RW_WIDTH)
    vec = _const_spec((1, RW_WIDTH))
    return pl.pallas_call(
        _rw_scan_body,
        grid=(s // n,),
        in_specs=[spec(2 * n, RW_WIDTH), blk, blk, blk, tall, tall, blk, blk, blk, blk,
                  vec, vec, vec],
        out_specs=blk,
        out_shape=jax.ShapeDtypeStruct((bsz, s, RW_WIDTH), _MXU),
        scratch_shapes=[pltpu.VMEM((bsz, RW_WIDTH, RW_WIDTH), _F32),
                        pltpu.VMEM((bsz, n, RW_WIDTH), _F32)],
        compiler_params=_params("arbitrary"),
        name="rw_scan",
    )(w1r, w2, mkv, mb, gt, pct, v, r, kp, g, gn_g, gn_b, r_k)


def _rwkv7(r, kp, v, kk, b, lw, g, r_k, gn_g, gn_b):
    w1r, w2, mkv, mb, gt, pct = _rw_chunk(r, kp, v, kk, b, lw)
    return _rw_scan(w1r, w2, mkv, mb, gt, pct, v, r, kp, g, gn_g, gn_b, r_k)


_SW_PAIR_ORDER = tuple(h for g in range(SW_HEADS // SW_KV_HEADS)
                       for h in (g, g + SW_HEADS // SW_KV_HEADS))


def _pad_rows(w, rows):
    return jnp.pad(w, ((0, rows - w.shape[0]), (0, 0)))


def _pair_heads(w, axis):
    shape = w.shape
    w = w.reshape(shape[:axis] + (SW_HEADS, HEAD_DIM) + shape[axis + 1:])
    w = jnp.take(w, jnp.array(_SW_PAIR_ORDER), axis=axis)
    return w.reshape(shape)


def _mix_in_layout(w_in, mu):
    pad = lambda t, width: jnp.pad(t, [(0, 0)] * (t.ndim - 1) + [(0, width - t.shape[-1])])

    def rw_groups(t):
        c = 3 * RW_WIDTH
        return [t[..., :c], pad(t[..., c:c + DECAY_LORA], LANES),
                pad(t[..., c + DECAY_LORA:c + DECAY_LORA + AAA_LORA], LANES),
                pad(t[..., c + DECAY_LORA + AAA_LORA:], 2 * LANES)]

    rw_end = SB_IN_WIDTH + 3 * RW_WIDTH + DECAY_LORA + AAA_LORA + GATE_LORA
    w = jnp.concatenate(
        [w_in[..., :SB_IN_WIDTH]] + rw_groups(w_in[..., SB_IN_WIDTH:rw_end])
        + [_pair_heads(w_in[..., rw_end:rw_end + SW_WIDTH], 2), w_in[..., rw_end + SW_WIDTH:]],
        axis=-1).astype(_MXU)
    return w, jnp.concatenate(rw_groups(mu[:, None, :]), axis=-1)


def kernel(x, ffn1_norm, ffn1_w_in, ffn1_w_out, mix_norm, mix_w_in, mix_w_out, rw_mu, rw_w0,
           rw_w_up, rw_a0, rw_a_up, rw_g_up, rw_k_k, rw_k_a, rw_r_k, rw_gn_g, rw_gn_b, sw_sinks,
           ffn2_norm, ffn2_w_in, ffn2_w_out, final_norm):
    bsz, s, d = x.shape
    xf = x.reshape(bsz * s, d)
    row = lambda t: t[None, :]
    final_g = row(final_norm)
    w1_in, w1_out = ffn1_w_in.astype(_MXU), ffn1_w_out.astype(_MXU)
    w2_in, w2_out = ffn2_w_in.astype(_MXU), ffn2_w_out.astype(_MXU)
    w_mix_in, mu_all = _mix_in_layout(mix_w_in, rw_mu)
    mix_rows = SB_WIDTH + RW_WIDTH
    w_mix_out = jnp.concatenate(
        [mix_w_out[:, :mix_rows], _pair_heads(mix_w_out[:, mix_rows:], 1)], axis=1).astype(_MXU)
    for l in range(DEPTH):
        xf = _ffn(xf, row(ffn1_norm[l]), w1_in, w1_out, final_g, False, l)
        sb, kt, sw, *rw_parts = _inproj(
            xf, row(mix_norm[l]), w_mix_in, mu_all[l], row(rw_w0[l]), _pad_rows(rw_w_up[l], LANES),
            row(rw_a0[l]), _pad_rows(rw_a_up[l], LANES), _pad_rows(rw_g_up[l], 2 * LANES),
            row(rw_k_k[l]), row(rw_k_a[l]), s, l)
        sb = sb.reshape(bsz, s, SB_IN_WIDTH)
        sw = sw.reshape(bsz, s, SW_IN_WIDTH)
        sb_out = _sb_attention(sb, kt.reshape(bsz, s // SB_TILE, SB_WIDTH, SB_TILE))
        rw_out = _rwkv7(
            *(t.reshape(bsz, s, RW_WIDTH) for t in rw_parts),
            rw_r_k[l].reshape(1, RW_WIDTH), row(rw_gn_g[l]), row(rw_gn_b[l]))
        sw_out = _sw_attention(sw, sw_sinks[l])
        mix = (sb_out.reshape(bsz * s, SB_WIDTH), rw_out.reshape(bsz * s, RW_WIDTH),
               sw_out.reshape(bsz * s, SW_WIDTH), w_mix_out)
        xf = _ffn(xf, row(ffn2_norm[l]), w2_in, w2_out, final_g, l == DEPTH - 1, l, mix)
    return xf.reshape(bsz, s, d)
```

```python
import functools

import jax
import jax.numpy as jnp
from jax import lax
from jax.experimental import pallas as pl
from jax.experimental.pallas import tpu as pltpu

D_MODEL = 1024
DEPTH = 4
HEAD_DIM = 64
SB_HEADS = 4
SB_WIDTH = SB_HEADS * HEAD_DIM
RW_HEADS = 4
RW_WIDTH = RW_HEADS * HEAD_DIM
DECAY_LORA = 64
AAA_LORA = 64
GATE_LORA = 160
RW_GN_EPS = 64e-5
SW_HEADS = 8
SW_KV_HEADS = 2
SW_WIDTH = SW_HEADS * HEAD_DIM
SW_KV_WIDTH = SW_KV_HEADS * HEAD_DIM
WINDOW = 128
D_FF = 2816
NORM_EPS = 1e-6

LANES = 128
MXU_DIM = 256
RW_PAD_WIDTH = 3 * RW_WIDTH + 2 * LANES + 2 * LANES
SB_IN_WIDTH = 3 * SB_WIDTH
SW_IN_WIDTH = SW_WIDTH + 2 * SW_KV_WIDTH
IN_PAD_WIDTH = SB_IN_WIDTH + RW_PAD_WIDTH + SW_IN_WIDTH

FFN_ROWS = 1024
PROJ_ROWS = 1024
PROJ_SUB = 256
FF_TILE = MXU_DIM
SB_TILE = 256
SW_TILE = WINDOW
SW_STEP_TILES = 2
RW_CHUNK = 64
RW_STEP_CHUNKS = 8
RW_GROUP = 4
RW_STAGES_PER_STATE_STEP = 4

VMEM_LIMIT_BYTES = 56 * 1024 * 1024

_MXU = jnp.bfloat16
_F32 = jnp.float32
_HI = lax.Precision.HIGHEST


def _dot(a, b, precision=None):
    return jnp.dot(a, b, preferred_element_type=_F32, precision=precision)


def _dot_nt(a, b, precision=None):
    return lax.dot_general(a, b, (((1,), (1,)), ((), ())),
                           preferred_element_type=_F32, precision=precision)


def _sigmoid(x):
    return 1.0 / (1.0 + jnp.exp(-x))


def _softplus(x):
    return jnp.maximum(x, 0.0) + jnp.log(1.0 + jnp.exp(-jnp.abs(x)))


def _params(*sem):
    return pltpu.CompilerParams(dimension_semantics=sem, vmem_limit_bytes=VMEM_LIMIT_BYTES)


def _const_spec(shape):
    nd = len(shape)
    return pl.BlockSpec(shape, lambda *_: (0,) * nd, pipeline_mode=pl.Buffered(1))


def _layer_spec(shape, layer, row_block=0):
    return pl.BlockSpec((None,) + shape, lambda *_: (layer, row_block, 0),
                        pipeline_mode=pl.Buffered(1))


def _ffn_body(*refs, final, mixed):
    if mixed:
        (x_ref, sb_ref, rw_ref, sw_ref, wsb_ref, wrw_ref, wsw_ref,
         g_ref, win_ref, wout_ref, fg_ref, o_ref, act_ref) = refs
        x = (x_ref[...] + _dot(sb_ref[...], wsb_ref[...]) + _dot(rw_ref[...], wrw_ref[...])
             + _dot(sw_ref[...], wsw_ref[...]))
    else:
        x_ref, g_ref, win_ref, wout_ref, fg_ref, o_ref, act_ref = refs
        x = x_ref[...]
    ms = jnp.mean(x * x, axis=-1, keepdims=True)
    hn = (x * lax.rsqrt(ms + NORM_EPS) * g_ref[...]).astype(_MXU)
    for c in range(D_FF // FF_TILE):
        lo, hi = c * FF_TILE, (c + 1) * FF_TILE
        gate = _dot(hn, win_ref[:, lo:hi])
        up = _dot(hn, win_ref[:, D_FF + lo:D_FF + hi])
        act_ref[:, lo:hi] = (gate * _sigmoid(gate) * up).astype(_MXU)
    y = x + 0.5 * _dot(act_ref[...], wout_ref[...])
    if final:
        ms = jnp.mean(y * y, axis=-1, keepdims=True)
        y = y * lax.rsqrt(ms + NORM_EPS) * fg_ref[...]
    o_ref[...] = y


def _ffn(x, g, w_in, w_out, final_g, final, layer, mix=None):
    m = x.shape[0]
    row = lambda i: (i, 0)
    mix_args, mix_specs = (), []
    if mix is not None:
        sb, rw, sw, w_mix = mix
        mix_args = (sb, rw, sw, w_mix, w_mix, w_mix)
        mix_specs = [pl.BlockSpec((FFN_ROWS, t.shape[1]), row) for t in (sb, rw, sw)]
        mix_specs += [_layer_spec((SB_WIDTH, D_MODEL), layer, 0),
                      _layer_spec((RW_WIDTH, D_MODEL), layer, SB_WIDTH // RW_WIDTH),
                      _layer_spec((SW_WIDTH, D_MODEL), layer, (SB_WIDTH + RW_WIDTH) // SW_WIDTH)]
    return pl.pallas_call(
        functools.partial(_ffn_body, final=final, mixed=mix is not None),
        grid=(m // FFN_ROWS,),
        in_specs=[pl.BlockSpec((FFN_ROWS, D_MODEL), row)] + mix_specs + [
            _const_spec((1, D_MODEL)),
            _layer_spec((D_MODEL, 2 * D_FF), layer),
            _layer_spec((D_FF, D_MODEL), layer),
            _const_spec((1, D_MODEL)),
        ],
        out_specs=pl.BlockSpec((FFN_ROWS, D_MODEL), row),
        out_shape=jax.ShapeDtypeStruct((m, D_MODEL), _F32),
        scratch_shapes=[pltpu.VMEM((FFN_ROWS, D_FF), _MXU)],
        compiler_params=_params("parallel"),
        name="ffn_mix" if mix is not None else "ffn",
    )(x, *mix_args, g, w_in, w_out, final_g)


def _head_block_ones():
    r = lax.broadcasted_iota(jnp.int32, (RW_WIDTH, RW_WIDTH), 0) // HEAD_DIM
    c = lax.broadcasted_iota(jnp.int32, (RW_WIDTH, RW_WIDTH), 1) // HEAD_DIM
    return r == c


def _head_sum(x, mat):
    hi = x.astype(_MXU)
    lo = (x - hi.astype(_F32)).astype(_MXU)
    return _dot(hi, mat) + _dot(lo, mat)


def _inproj_body(x_ref, g_ref, w_ref, mu_ref, w0_ref, wup_ref, a0_ref, aup_ref, gup_ref,
                 kk_ref, ka_ref, sb_ref, kt_ref, sw_ref, r_o, kp_o, v_o, kkn_o, b_o, lw_o, g_o,
                 last_ref, *, seq_tiles):
    i = pl.program_id(0)
    sub = PROJ_SUB
    assert sub == SB_TILE
    n_sub = x_ref.shape[0] // sub
    c = RW_WIDTH
    ones = jnp.where(_head_block_ones(), 1.0, 0.0).astype(_MXU)
    first = lax.broadcasted_iota(jnp.int32, (sub, 1), 0) == 0

    @pl.when(i == 0)
    def _():
        last_ref[...] = jnp.zeros_like(last_ref)

    def project(j):
        rows = slice(j * sub, (j + 1) * sub)
        x = x_ref[rows]
        ms = jnp.mean(x * x, axis=-1, keepdims=True)
        hn = (x * lax.rsqrt(ms + NORM_EPS) * g_ref[...]).astype(_MXU)
        sb = _dot(hn, w_ref[:, :SB_IN_WIDTH])
        sb_ref[rows] = sb.astype(sb_ref.dtype)
        kt_ref[j] = sb[:, SB_WIDTH:2 * SB_WIDTH].T.astype(kt_ref.dtype)
        sw_ref[rows] = _dot(hn, w_ref[:, SB_IN_WIDTH + RW_PAD_WIDTH:]).astype(sw_ref.dtype)
        return _dot(hn, w_ref[:, SB_IN_WIDTH:SB_IN_WIDTH + RW_PAD_WIDTH])

    def prepare(j, p, prev_last):
        rows = slice(j * sub, (j + 1) * sub)
        pprev = jnp.where(first, prev_last, pltpu.roll(p, 1, 0))
        xm = p + (pprev - p) * mu_ref[...]
        r, k, v = xm[:, :c], xm[:, c:2 * c], xm[:, 2 * c:3 * c]
        xw = xm[:, 3 * c:3 * c + LANES]
        xa = xm[:, 3 * c + LANES:3 * c + 2 * LANES]
        xg = xm[:, 3 * c + 2 * LANES:]
        log_w = -_softplus(-(w0_ref[...] + _rw_mm(jnp.tanh(xw), wup_ref[...]))) - 0.5
        a = _sigmoid(a0_ref[...] + _rw_mm(xa, aup_ref[...]))
        g = _rw_mm(_sigmoid(xg), gup_ref[...])
        kk = k * kk_ref[...]
        n2 = _head_sum(kk * kk, ones)
        kk = kk * lax.rsqrt(jnp.maximum(n2, 1e-24))
        r_o[rows] = r
        kp_o[rows] = k * (1.0 + (a - 1.0) * ka_ref[...])
        v_o[rows] = v
        kkn_o[rows] = kk
        b_o[rows] = kk * a
        lw_o[rows] = -jnp.exp(log_w)
        g_o[rows] = g

    prev_last = jnp.where(i % seq_tiles == 0, 0.0, last_ref[7:8, :])
    ps = [project(0)]
    for j in range(n_sub):
        if j + 1 < n_sub:
            ps.append(project(j + 1))
        prepare(j, ps[j], prev_last)
        prev_last = ps[j][sub - 1:sub, :]
    last_ref[...] = ps[-1][sub - 8:, :]


def _inproj(x, g, w, mu, w0, wup, a0, aup, gup, k_k, k_a, seq_len, layer):
    m = x.shape[0]
    n = PROJ_ROWS
    row = lambda i: (i, 0)
    vec = _const_spec((1, RW_WIDTH))
    rw_out = pl.BlockSpec((n, RW_WIDTH), row)
    return pl.pallas_call(
        functools.partial(_inproj_body, seq_tiles=seq_len // n),
        grid=(m // n,),
        in_specs=[
            pl.BlockSpec((n, D_MODEL), row),
            _const_spec((1, D_MODEL)),
            _layer_spec((D_MODEL, IN_PAD_WIDTH), layer),
            _const_spec((1, RW_PAD_WIDTH)),
            vec, _const_spec((LANES, RW_WIDTH)),
            vec, _const_spec((LANES, RW_WIDTH)),
            _const_spec((2 * LANES, RW_WIDTH)),
            vec, vec,
        ],
        out_specs=[pl.BlockSpec((n, SB_IN_WIDTH), row),
                   pl.BlockSpec((n // SB_TILE, SB_WIDTH, SB_TILE), lambda i: (i, 0, 0)),
                   pl.BlockSpec((n, SW_IN_WIDTH), row)]
        + [rw_out] * 7,
        out_shape=[jax.ShapeDtypeStruct((m, SB_IN_WIDTH), _MXU),
                   jax.ShapeDtypeStruct((m // SB_TILE, SB_WIDTH, SB_TILE), _MXU),
                   jax.ShapeDtypeStruct((m, SW_IN_WIDTH), _MXU)]
        + [jax.ShapeDtypeStruct((m, RW_WIDTH), _F32)] * 7,
        scratch_shapes=[pltpu.VMEM((8, RW_PAD_WIDTH), _F32)],
        compiler_params=_params("arbitrary"),
        name="inproj",
    )(x, g, w, mu, w0, wup, a0, aup, gup, k_k, k_a)


SB_SKIP_LOG = -88.0


def _sb_body(q_ref, kt_ref, v_ref, o_ref):
    t = SB_TILE
    i = pl.program_id(1)
    q = q_ref[0]
    lane_head = lax.broadcasted_iota(jnp.int32, (1, SB_WIDTH), 1) // HEAD_DIM
    row = lax.broadcasted_iota(jnp.int32, (t, t), 0)
    col = lax.broadcasted_iota(jnp.int32, (t, t), 1)
    later = jnp.where(row > col, 1.0, 0.0).astype(_MXU)
    later2 = jnp.concatenate([later, later], axis=0)
    causal = col < row
    q = q * (HEAD_DIM ** -0.5)
    q_heads = [jnp.where(lane_head == h, q, jnp.zeros_like(q)) for h in range(SB_HEADS)]

    def block(jb, acc, carries, diagonal):
        ktb = kt_ref[0, jb]
        vb = v_ref[0, pl.ds(pl.multiple_of(jb * t, t), t), :]
        weights, new_carries = [], []
        for h in range(SB_HEADS):
            z = _dot(q_heads[h], ktb)
            lom = -_softplus(z)
            if diagonal:
                lom = jnp.where(causal, lom, 0.0)
            hi = lom.astype(_MXU)
            lo = (lom - hi.astype(_F32)).astype(_MXU)
            tail = _dot(jnp.concatenate([hi, lo], axis=1), later2) + carries[h]
            a = jnp.exp(z + lom + tail)
            if diagonal:
                a = jnp.where(causal, a, 0.0)
            weights.append(a.astype(_MXU))
            new_carries.append(carries[h] + jnp.sum(lom, axis=1, keepdims=True))
        v_heads = jnp.concatenate(
            [jnp.where(lane_head == h, vb, jnp.zeros_like(vb)) for h in range(SB_HEADS)], axis=0)
        acc = acc + _dot(jnp.concatenate(weights, axis=1), v_heads)
        return acc, tuple(new_carries)

    zero = jnp.zeros((t, 1), _F32)
    acc, carries = block(i, jnp.zeros((t, SB_WIDTH), _F32), (zero,) * SB_HEADS, True)

    def live(state):
        jb, _, carries = state
        top = functools.reduce(jnp.maximum, carries)
        return (jb >= 0) & (jnp.max(top) > SB_SKIP_LOG)

    def step(state):
        jb, acc, carries = state
        acc, carries = block(jb, acc, carries, False)
        return jb - 1, acc, carries

    _, acc, _ = lax.while_loop(live, step, (i - 1, acc, carries))
    o_ref[0] = acc.astype(o_ref.dtype)


def _sb_attention(sb, kt):
    b, s, _ = sb.shape
    t = SB_TILE
    return pl.pallas_call(
        _sb_body,
        grid=(b, s // t),
        in_specs=[
            pl.BlockSpec((1, t, SB_WIDTH), lambda bi, i: (bi, i, 0)),
            pl.BlockSpec((1, s // t, SB_WIDTH, t), lambda bi, i: (bi, 0, 0, 0)),
            pl.BlockSpec((1, s, SB_WIDTH), lambda bi, i: (bi, 0, 2)),
        ],
        out_specs=pl.BlockSpec((1, t, SB_WIDTH), lambda bi, i: (bi, i, 0)),
        out_shape=jax.ShapeDtypeStruct((b, s, SB_WIDTH), _MXU),
        compiler_params=_params("parallel", "arbitrary"),
        name="sb_attn",
    )(sb, kt, sb)


def _sw_body(sinks_ref, q_ref, kp_ref, kc_ref, vp_ref, vc_ref, o_ref):
    t = SW_TILE
    assert t == WINDOW
    i = pl.program_id(1)
    kcat = jnp.concatenate([kp_ref[0], kc_ref[0]], axis=0)
    vcat = jnp.concatenate([vp_ref[0], vc_ref[0]], axis=0)
    r_i = lax.broadcasted_iota(jnp.int32, (t, t), 0)
    c_i = lax.broadcasted_iota(jnp.int32, (t, t), 1)
    own = c_i <= r_i
    distf = jnp.where(own, r_i - c_i, r_i - c_i + t).astype(_F32)
    lane_kv = lax.broadcasted_iota(jnp.int32, (1, LANES), 1) // HEAD_DIM
    scale = HEAD_DIM ** -0.5
    grp = SW_HEADS // SW_KV_HEADS
    for j in range(SW_STEP_TILES):
        rows = slice(j * t, (j + 1) * t)
        kj, vj = kcat[j * t:(j + 2) * t], vcat[j * t:(j + 2) * t]
        valid = own | (i * SW_STEP_TILES + j > 0)
        for g in range(grp):
            qp = q_ref[0, rows, g * LANES:(g + 1) * LANES] * scale
            res = None
            for kv in range(SW_KV_HEADS):
                head = kv * grp + g
                slope = 2.0 ** (-8.0 * (head + 1.0) / SW_HEADS)
                qh = jnp.where(lane_kv == kv, qp, jnp.zeros_like(qp))
                zz = _dot_nt(qh, kj)
                z = jnp.where(own, zz[:, t:], zz[:, :t]) - slope * distf
                z = jnp.where(valid, z, -1e30)
                sink = sinks_ref[head]
                m = jnp.maximum(jnp.max(z, axis=1, keepdims=True), sink)
                p = jnp.exp(z - m)
                denom = jnp.sum(p, axis=1, keepdims=True) + jnp.exp(sink - m)
                p = p.astype(_MXU)
                zero = jnp.zeros_like(p)
                pp = jnp.concatenate([jnp.where(own, zero, p), jnp.where(own, p, zero)], axis=1)
                o = _dot(pp, vj) / denom
                res = o if kv == 0 else jnp.where(lane_kv == 0, res, o)
            o_ref[0, rows, g * LANES:(g + 1) * LANES] = res.astype(o_ref.dtype)


def _sw_attention(sw, sinks):
    b, s, _ = sw.shape
    t = SW_TILE
    n = SW_STEP_TILES * t
    qb = SW_WIDTH // LANES
    cur = lambda c: (lambda bi, i: (bi, i, c))
    prev = lambda c: (lambda bi, i: (bi, jnp.maximum(i * SW_STEP_TILES - 1, 0), c))
    return pl.pallas_call(
        _sw_body,
        grid=(b, s // n),
        in_specs=[
            pl.BlockSpec(memory_space=pltpu.SMEM),
            pl.BlockSpec((1, n, SW_WIDTH), lambda bi, i: (bi, i, 0)),
            pl.BlockSpec((1, t, LANES), prev(qb)),
            pl.BlockSpec((1, n, LANES), cur(qb)),
            pl.BlockSpec((1, t, LANES), prev(qb + 1)),
            pl.BlockSpec((1, n, LANES), cur(qb + 1)),
        ],
        out_specs=pl.BlockSpec((1, n, SW_WIDTH), lambda bi, i: (bi, i, 0)),
        out_shape=jax.ShapeDtypeStruct((b, s, SW_WIDTH), _MXU),
        compiler_params=_params("parallel", "arbitrary"),
        name="sw_attn",
    )(sinks, sw, sw, sw, sw, sw)


def _rw_mm(a, b):
    return _dot(a.astype(_MXU), b.astype(_MXU))


def _rw_mm_nt(a, b):
    return _dot_nt(a.astype(_MXU), b.astype(_MXU))


def _head_stack(x, lane_head):
    return jnp.concatenate(
        [jnp.where(lane_head == h, x, 0.0) for h in range(RW_HEADS)], axis=0)


def _rw_chunk_ops(r, kp, v, kk, b, lw):
    cl = RW_CHUNK
    chunks = range(len(r))
    each = lambda f, *xs: [f(*a) for a in zip(*xs)]

    lane_head = lax.broadcasted_iota(jnp.int32, (1, RW_WIDTH), 1) // HEAD_DIM
    tr = lax.broadcasted_iota(jnp.int32, (cl, cl), 0)
    tc = lax.broadcasted_iota(jnp.int32, (cl, cl), 1)
    lower = jnp.where(tr >= tc, 1.0, 0.0)
    cum = each(lambda x: _dot(lower, x, _HI), lw)
    yield
    cum_end = each(lambda x: x[cl - 1:cl, :], cum)
    kk_t = each(lambda x, c, l: x * jnp.exp(c - l), kk, cum, lw)
    r_t = each(lambda x, c: x * jnp.exp(c), r, cum)
    inv = each(lambda c: jnp.exp(-c), cum)
    k_h = each(jnp.multiply, kp, inv)
    b_h = each(jnp.multiply, b, inv)
    yield
    to_end = each(lambda e, c: jnp.exp(e - c), cum_end, cum)
    gt = each(lambda k, bb, e: jnp.concatenate([k * e, -(bb * e)], axis=0).T.astype(_MXU),
              kp, b, to_end)
    pct = each(lambda e: jnp.broadcast_to(jnp.exp(e), (2 * cl, RW_WIDTH)).T, cum_end)
    yield

    stack = lambda x: _head_stack(x, lane_head)
    khs, bhs, vs = each(stack, k_h), each(stack, b_h), each(stack, v)
    t_i = lax.broadcasted_iota(jnp.int32, (cl, RW_WIDTH), 0)
    s_i = lax.broadcasted_iota(jnp.int32, (cl, RW_WIDTH), 1) % cl
    strict, incl = t_i > s_i, t_i >= s_i
    lhs = each(lambda x, y: jnp.concatenate([x, y], axis=0), kk_t, r_t)
    pk = each(_rw_mm_nt, lhs, khs)
    yield
    pb = each(_rw_mm_nt, lhs, bhs)
    yield
    a_b = each(lambda p: jnp.where(strict, p[:cl], 0.0), pb)
    m_b = each(lambda p: jnp.where(incl, p[cl:], 0.0).astype(_MXU), pb)
    akm = each(lambda p: jnp.concatenate(
        [jnp.where(strict, p[:cl], 0.0), jnp.where(incl, p[cl:], 0.0)], axis=0), pk)
    kv = each(_rw_mm, akm, vs)
    yield

    same_head = _head_block_ones()

    def block_diag(x_cat):
        return jnp.where(same_head, jnp.concatenate([x_cat] * RW_HEADS, axis=0), 0.0)

    eye = jnp.where(t_i == s_i, 1.0, 0.0)
    pair = ((t_i ^ s_i) == 1) & ((t_i & 1) != 0)
    t_cat = each(lambda a: eye - jnp.where(pair, a, 0.0), a_b)
    m = 2
    while m < cl:
        below = (((t_i ^ s_i) & -(2 * m)) == 0) & ((t_i & m) != 0) & ((s_i & m) == 0)
        x = each(lambda t, a: _rw_mm(t, block_diag(jnp.where(below, a, 0.0))), t_cat, a_b)
        yield
        t_cat = each(lambda t, xx: t - _rw_mm(xx, block_diag(t)), t_cat, x)
        yield
        m *= 2
    w = each(lambda t, k, p: _rw_mm(t, jnp.concatenate([stack(k), stack(p[:cl])], axis=1)),
             t_cat, kk_t, kv)
    return [(jnp.concatenate([w[c][:, :RW_WIDTH], r_t[c]], axis=0).astype(_MXU),
             w[c][:, RW_WIDTH:], kv[c][cl:], m_b[c], gt[c], pct[c]) for c in chunks]


def _advance(staged, stages=None):
    try:
        while stages is None or stages > 0:
            next(staged)
            stages = None if stages is None else stages - 1
    except StopIteration as done:
        return done.value
    return None


def _rw_core_body(r_ref, kp_ref, v_ref, kk_ref, b_ref, lw_ref, g_ref, gng_ref, gnb_ref, rk_ref,
                  o_ref, h_ref, y_ref):
    cl = RW_CHUNK

    @pl.when(pl.program_id(0) == 0)
    def _():
        h_ref[...] = jnp.zeros_like(h_ref)

    lane_head = lax.broadcasted_iota(jnp.int32, (1, RW_WIDTH), 1) // HEAD_DIM
    same_head = _head_block_ones()
    batch = range(h_ref.shape[0])
    rows = lambda c: slice(c * cl, (c + 1) * cl)
    groups = [range(g, g + RW_GROUP) for g in range(0, RW_STEP_CHUNKS, RW_GROUP)]

    def staged_ops(group):
        load = lambda ref: [ref[bi, rows(c)] for c in group for bi in batch]
        return _rw_chunk_ops(*(load(x) for x in (r_ref, kp_ref, v_ref, kk_ref, b_ref, lw_ref)))

    def state_step(c, ops):
        h = [h_ref[bi] for bi in batch]
        uy = [_dot(ops[bi][0], h[bi].astype(_MXU)) for bi in batch]
        u = [uy[bi][:cl] + ops[bi][1] for bi in batch]
        vu = [jnp.concatenate([v_ref[bi, rows(c)], u[bi]], axis=0) for bi in batch]
        upd = [_dot(ops[bi][4], vu[bi].astype(_MXU)) for bi in batch]
        for bi in batch:
            pct = ops[bi][5]
            h_ref[bi] = (jnp.concatenate([pct, pct], axis=1) * h[bi]
                         + jnp.where(same_head, upd[bi], 0.0))
        for bi in batch:
            y_ref[bi, rows(c)] = (uy[bi][cl:] + ops[bi][2]
                                  - _rw_mm(ops[bi][3], _head_stack(u[bi], lane_head)))

    ops = _advance(staged_ops(groups[0]))
    for gi, group in enumerate(groups):
        following = staged_ops(groups[gi + 1]) if gi + 1 < len(groups) else None
        ready = None
        for k, c in enumerate(group):
            if following is not None and ready is None:
                ready = _advance(following, RW_STAGES_PER_STATE_STEP)
            state_step(c, ops[k * len(batch):(k + 1) * len(batch)])
        if following is not None:
            ops = ready if ready is not None else _advance(following)

    mean_mat = jnp.where(same_head, 1.0 / HEAD_DIM, 0.0).astype(_MXU)
    ones = jnp.where(same_head, 1.0, 0.0).astype(_MXU)
    for bi in batch:
        y = y_ref[bi]
        d = y - _head_sum(y, mean_mat)
        var = _head_sum(d * d, mean_mat)
        yn = d * lax.rsqrt(var + RW_GN_EPS) * gng_ref[...] + gnb_ref[...]
        bonus = _head_sum(r_ref[bi] * kp_ref[bi] * rk_ref[...], ones) * v_ref[bi]
        o_ref[bi] = ((yn + bonus) * g_ref[bi]).astype(o_ref.dtype)


def _rwkv7(r, kp, v, kk, b, lw, g, r_k, gn_g, gn_b):
    bsz, s, _ = v.shape
    n = RW_STEP_CHUNKS * RW_CHUNK
    blk = pl.BlockSpec((bsz, n, RW_WIDTH), lambda i: (0, i, 0))
    vec = _const_spec((1, RW_WIDTH))
    return pl.pallas_call(
        _rw_core_body,
        grid=(s // n,),
        in_specs=[blk] * 7 + [vec] * 3,
        out_specs=blk,
        out_shape=jax.ShapeDtypeStruct((bsz, s, RW_WIDTH), _MXU),
        scratch_shapes=[pltpu.VMEM((bsz, RW_WIDTH, RW_WIDTH), _F32),
                        pltpu.VMEM((bsz, n, RW_WIDTH), _F32)],
        compiler_params=_params("arbitrary"),
        name="rw_core",
    )(r, kp, v, kk, b, lw, g, gn_g, gn_b, r_k)


_SW_PAIR_ORDER = tuple(h for g in range(SW_HEADS // SW_KV_HEADS)
                       for h in (g, g + SW_HEADS // SW_KV_HEADS))


def _pad_rows(w, rows):
    return jnp.pad(w, ((0, rows - w.shape[0]), (0, 0)))


def _pair_heads(w, axis):
    shape = w.shape
    w = w.reshape(shape[:axis] + (SW_HEADS, HEAD_DIM) + shape[axis + 1:])
    w = jnp.take(w, jnp.array(_SW_PAIR_ORDER), axis=axis)
    return w.reshape(shape)


def _mix_in_layout(w_in, mu):
    pad = lambda t, width: jnp.pad(t, [(0, 0)] * (t.ndim - 1) + [(0, width - t.shape[-1])])

    def rw_groups(t):
        c = 3 * RW_WIDTH
        return [t[..., :c], pad(t[..., c:c + DECAY_LORA], LANES),
                pad(t[..., c + DECAY_LORA:c + DECAY_LORA + AAA_LORA], LANES),
                pad(t[..., c + DECAY_LORA + AAA_LORA:], 2 * LANES)]

    rw_end = SB_IN_WIDTH + 3 * RW_WIDTH + DECAY_LORA + AAA_LORA + GATE_LORA
    w = jnp.concatenate(
        [w_in[..., :SB_IN_WIDTH]] + rw_groups(w_in[..., SB_IN_WIDTH:rw_end])
        + [_pair_heads(w_in[..., rw_end:rw_end + SW_WIDTH], 2), w_in[..., rw_end + SW_WIDTH:]],
        axis=-1).astype(_MXU)
    return w, jnp.concatenate(rw_groups(mu[:, None, :]), axis=-1)


def kernel(x, ffn1_norm, ffn1_w_in, ffn1_w_out, mix_norm, mix_w_in, mix_w_out, rw_mu, rw_w0,
           rw_w_up, rw_a0, rw_a_up, rw_g_up, rw_k_k, rw_k_a, rw_r_k, rw_gn_g, rw_gn_b, sw_sinks,
           ffn2_norm, ffn2_w_in, ffn2_w_out, final_norm):
    bsz, s, d = x.shape
    xf = x.reshape(bsz * s, d)
    row = lambda t: t[None, :]
    final_g = row(final_norm)
    w1_in, w1_out = ffn1_w_in.astype(_MXU), ffn1_w_out.astype(_MXU)
    w2_in, w2_out = ffn2_w_in.astype(_MXU), ffn2_w_out.astype(_MXU)
    w_mix_in, mu_all = _mix_in_layout(mix_w_in, rw_mu)
    mix_rows = SB_WIDTH + RW_WIDTH
    w_mix_out = jnp.concatenate(
        [mix_w_out[:, :mix_rows], _pair_heads(mix_w_out[:, mix_rows:], 1)], axis=1).astype(_MXU)
    for l in range(DEPTH):
        xf = _ffn(xf, row(ffn1_norm[l]), w1_in, w1_out, final_g, False, l)
        sb, kt, sw, *rw_parts = _inproj(
            xf, row(mix_norm[l]), w_mix_in, mu_all[l], row(rw_w0[l]), _pad_rows(rw_w_up[l], LANES),
            row(rw_a0[l]), _pad_rows(rw_a_up[l], LANES), _pad_rows(rw_g_up[l], 2 * LANES),
            row(rw_k_k[l]), row(rw_k_a[l]), s, l)
        sb = sb.reshape(bsz, s, SB_IN_WIDTH)
        sw = sw.reshape(bsz, s, SW_IN_WIDTH)
        sb_out = _sb_attention(sb, kt.reshape(bsz, s // SB_TILE, SB_WIDTH, SB_TILE))
        rw_out = _rwkv7(
            *(t.reshape(bsz, s, RW_WIDTH) for t in rw_parts),
            rw_r_k[l].reshape(1, RW_WIDTH), row(rw_gn_g[l]), row(rw_gn_b[l]))
        sw_out = _sw_attention(sw, sw_sinks[l])
        mix = (sb_out.reshape(bsz * s, SB_WIDTH), rw_out.reshape(bsz * s, RW_WIDTH),
               sw_out.reshape(bsz * s, SW_WIDTH), w_mix_out)
        xf = _ffn(xf, row(ffn2_norm[l]), w2_in, w2_out, final_g, l == DEPTH - 1, l, mix)
    return xf.reshape(bsz, s, d)
```

```python
import functools

import jax
import jax.numpy as jnp
from jax import lax
from jax.experimental import pallas as pl
from jax.experimental.pallas import tpu as pltpu

D_MODEL = 1024
DEPTH = 4
HEAD_DIM = 64
SB_HEADS = 4
SB_WIDTH = SB_HEADS * HEAD_DIM
RW_HEADS = 4
RW_WIDTH = RW_HEADS * HEAD_DIM
DECAY_LORA = 64
AAA_LORA = 64
GATE_LORA = 160
RW_GN_EPS = 64e-5
SW_HEADS = 8
SW_KV_HEADS = 2
SW_WIDTH = SW_HEADS * HEAD_DIM
SW_KV_WIDTH = SW_KV_HEADS * HEAD_DIM
WINDOW = 128
D_FF = 2816
NORM_EPS = 1e-6

LANES = 128
MXU_DIM = 256
RW_PAD_WIDTH = 3 * RW_WIDTH + 2 * LANES + 2 * LANES
SB_IN_WIDTH = 3 * SB_WIDTH
SW_IN_WIDTH = SW_WIDTH + 2 * SW_KV_WIDTH
IN_PAD_WIDTH = SB_IN_WIDTH + RW_PAD_WIDTH + SW_IN_WIDTH

FFN_ROWS = 1024
PROJ_ROWS = 1024
PROJ_SUB = 256
FF_TILE = MXU_DIM
SB_TILE = 256
SW_TILE = WINDOW
SW_STEP_TILES = 2
RW_CHUNK = 64
RW_STEP_CHUNKS = 8
RW_GROUP = 4
RW_STAGES_PER_STATE_STEP = 4

VMEM_LIMIT_BYTES = 56 * 1024 * 1024

_MXU = jnp.bfloat16
_F32 = jnp.float32
_HI = lax.Precision.HIGHEST


def _dot(a, b, precision=None):
    return jnp.dot(a, b, preferred_element_type=_F32, precision=precision)


def _dot_nt(a, b, precision=None):
    return lax.dot_general(a, b, (((1,), (1,)), ((), ())),
                           preferred_element_type=_F32, precision=precision)


def _sigmoid(x):
    return 1.0 / (1.0 + jnp.exp(-x))


def _softplus(x):
    return jnp.maximum(x, 0.0) + jnp.log(1.0 + jnp.exp(-jnp.abs(x)))


def _params(*sem):
    return pltpu.CompilerParams(dimension_semantics=sem, vmem_limit_bytes=VMEM_LIMIT_BYTES)


def _const_spec(shape):
    nd = len(shape)
    return pl.BlockSpec(shape, lambda *_: (0,) * nd, pipeline_mode=pl.Buffered(1))


def _layer_spec(shape, layer, row_block=0):
    return pl.BlockSpec((None,) + shape, lambda *_: (layer, row_block, 0),
                        pipeline_mode=pl.Buffered(1))


def _ffn_body(*refs, final, mixed):
    if mixed:
        (x_ref, sb_ref, rw_ref, sw_ref, wsb_ref, wrw_ref, wsw_ref,
         g_ref, win_ref, wout_ref, fg_ref, o_ref, act_ref) = refs
        x = (x_ref[...] + _dot(sb_ref[...], wsb_ref[...]) + _dot(rw_ref[...], wrw_ref[...])
             + _dot(sw_ref[...], wsw_ref[...]))
    else:
        x_ref, g_ref, win_ref, wout_ref, fg_ref, o_ref, act_ref = refs
        x = x_ref[...]
    ms = jnp.mean(x * x, axis=-1, keepdims=True)
    hn = (x * lax.rsqrt(ms + NORM_EPS) * g_ref[...]).astype(_MXU)
    for c in range(D_FF // FF_TILE):
        lo, hi = c * FF_TILE, (c + 1) * FF_TILE
        gate = _dot(hn, win_ref[:, lo:hi])
        up = _dot(hn, win_ref[:, D_FF + lo:D_FF + hi])
        act_ref[:, lo:hi] = (gate * _sigmoid(gate) * up).astype(_MXU)
    y = x + 0.5 * _dot(act_ref[...], wout_ref[...])
    if final:
        ms = jnp.mean(y * y, axis=-1, keepdims=True)
        y = y * lax.rsqrt(ms + NORM_EPS) * fg_ref[...]
    o_ref[...] = y


def _ffn(x, g, w_in, w_out, final_g, final, layer, mix=None):
    m = x.shape[0]
    row = lambda i: (i, 0)
    mix_args, mix_specs = (), []
    if mix is not None:
        sb, rw, sw, w_mix = mix
        mix_args = (sb, rw, sw, w_mix, w_mix, w_mix)
        mix_specs = [pl.BlockSpec((FFN_ROWS, t.shape[1]), row) for t in (sb, rw, sw)]
        mix_specs += [_layer_spec((SB_WIDTH, D_MODEL), layer, 0),
                      _layer_spec((RW_WIDTH, D_MODEL), layer, SB_WIDTH // RW_WIDTH),
                      _layer_spec((SW_WIDTH, D_MODEL), layer, (SB_WIDTH + RW_WIDTH) // SW_WIDTH)]
    return pl.pallas_call(
        functools.partial(_ffn_body, final=final, mixed=mix is not None),
        grid=(m // FFN_ROWS,),
        in_specs=[pl.BlockSpec((FFN_ROWS, D_MODEL), row)] + mix_specs + [
            _const_spec((1, D_MODEL)),
            _layer_spec((D_MODEL, 2 * D_FF), layer),
            _layer_spec((D_FF, D_MODEL), layer),
            _const_spec((1, D_MODEL)),
        ],
        out_specs=pl.BlockSpec((FFN_ROWS, D_MODEL), row),
        out_shape=jax.ShapeDtypeStruct((m, D_MODEL), _F32),
        scratch_shapes=[pltpu.VMEM((FFN_ROWS, D_FF), _MXU)],
        compiler_params=_params("parallel"),
        name="ffn_mix" if mix is not None else "ffn",
    )(x, *mix_args, g, w_in, w_out, final_g)


def _head_block_ones():
    r = lax.broadcasted_iota(jnp.int32, (RW_WIDTH, RW_WIDTH), 0) // HEAD_DIM
    c = lax.broadcasted_iota(jnp.int32, (RW_WIDTH, RW_WIDTH), 1) // HEAD_DIM
    return r == c


def _head_sum(x, mat):
    hi = x.astype(_MXU)
    lo = (x - hi.astype(_F32)).astype(_MXU)
    return _dot(hi, mat) + _dot(lo, mat)


def _inproj_body(x_ref, g_ref, w_ref, mu_ref, w0_ref, wup_ref, a0_ref, aup_ref, gup_ref,
                 kk_ref, ka_ref, sb_ref, kt_ref, sw_ref, r_o, kp_o, v_o, kkn_o, b_o, lw_o, g_o,
                 last_ref, *, seq_tiles):
    i = pl.program_id(0)
    sub = PROJ_SUB
    assert sub == SB_TILE
    n_sub = x_ref.shape[0] // sub
    c = RW_WIDTH
    ones = jnp.where(_head_block_ones(), 1.0, 0.0).astype(_MXU)
    first = lax.broadcasted_iota(jnp.int32, (sub, 1), 0) == 0

    @pl.when(i == 0)
    def _():
        last_ref[...] = jnp.zeros_like(last_ref)

    def project(j):
        rows = slice(j * sub, (j + 1) * sub)
        x = x_ref[rows]
        ms = jnp.mean(x * x, axis=-1, keepdims=True)
        hn = (x * lax.rsqrt(ms + NORM_EPS) * g_ref[...]).astype(_MXU)
        sb = _dot(hn, w_ref[:, :SB_IN_WIDTH])
        sb_ref[rows] = sb.astype(sb_ref.dtype)
        kt_ref[j] = sb[:, SB_WIDTH:2 * SB_WIDTH].T.astype(kt_ref.dtype)
        sw_ref[rows] = _dot(hn, w_ref[:, SB_IN_WIDTH + RW_PAD_WIDTH:]).astype(sw_ref.dtype)
        return _dot(hn, w_ref[:, SB_IN_WIDTH:SB_IN_WIDTH + RW_PAD_WIDTH])

    def prepare(j, p, prev_last):
        rows = slice(j * sub, (j + 1) * sub)
        pprev = jnp.where(first, prev_last, pltpu.roll(p, 1, 0))
        xm = p + (pprev - p) * mu_ref[...]
        r, k, v = xm[:, :c], xm[:, c:2 * c], xm[:, 2 * c:3 * c]
        xw = xm[:, 3 * c:3 * c + LANES]
        xa = xm[:, 3 * c + LANES:3 * c + 2 * LANES]
        xg = xm[:, 3 * c + 2 * LANES:]
        log_w = -_softplus(-(w0_ref[...] + _rw_mm(jnp.tanh(xw), wup_ref[...]))) - 0.5
        a = _sigmoid(a0_ref[...] + _rw_mm(xa, aup_ref[...]))
        g = _rw_mm(_sigmoid(xg), gup_ref[...])
        kk = k * kk_ref[...]
        n2 = _head_sum(kk * kk, ones)
        kk = kk * lax.rsqrt(jnp.maximum(n2, 1e-24))
        r_o[rows] = r
        kp_o[rows] = k * (1.0 + (a - 1.0) * ka_ref[...])
        v_o[rows] = v
        kkn_o[rows] = kk
        b_o[rows] = kk * a
        lw_o[rows] = -jnp.exp(log_w)
        g_o[rows] = g

    prev_last = jnp.where(i % seq_tiles == 0, 0.0, last_ref[7:8, :])
    ps = [project(0)]
    for j in range(n_sub):
        if j + 1 < n_sub:
            ps.append(project(j + 1))
        prepare(j, ps[j], prev_last)
        prev_last = ps[j][sub - 1:sub, :]
    last_ref[...] = ps[-1][sub - 8:, :]


def _inproj(x, g, w, mu, w0, wup, a0, aup, gup, k_k, k_a, seq_len, layer):
    m = x.shape[0]
    n = PROJ_ROWS
    row = lambda i: (i, 0)
    vec = _const_spec((1, RW_WIDTH))
    rw_out = pl.BlockSpec((n, RW_WIDTH), row)
    return pl.pallas_call(
        functools.partial(_inproj_body, seq_tiles=seq_len // n),
        grid=(m // n,),
        in_specs=[
            pl.BlockSpec((n, D_MODEL), row),
            _const_spec((1, D_MODEL)),
            _layer_spec((D_MODEL, IN_PAD_WIDTH), layer),
            _const_spec((1, RW_PAD_WIDTH)),
            vec, _const_spec((LANES, RW_WIDTH)),
            vec, _const_spec((LANES, RW_WIDTH)),
            _const_spec((2 * LANES, RW_WIDTH)),
            vec, vec,
        ],
        out_specs=[pl.BlockSpec((n, SB_IN_WIDTH), row),
                   pl.BlockSpec((n // SB_TILE, SB_WIDTH, SB_TILE), lambda i: (i, 0, 0)),
                   pl.BlockSpec((n, SW_IN_WIDTH), row)]
        + [rw_out] * 7,
        out_shape=[jax.ShapeDtypeStruct((m, SB_IN_WIDTH), _MXU),
                   jax.ShapeDtypeStruct((m // SB_TILE, SB_WIDTH, SB_TILE), _MXU),
                   jax.ShapeDtypeStruct((m, SW_IN_WIDTH), _MXU)]
        + [jax.ShapeDtypeStruct((m, RW_WIDTH), _F32)] * 7,
        scratch_shapes=[pltpu.VMEM((8, RW_PAD_WIDTH), _F32)],
        compiler_params=_params("arbitrary"),
        name="inproj",
    )(x, g, w, mu, w0, wup, a0, aup, gup, k_k, k_a)


SB_SKIP_LOG = -88.0


def _sb_body(q_ref, kt_ref, v_ref, o_ref):
    t = SB_TILE
    i = pl.program_id(1)
    q = q_ref[0]
    lane_head = lax.broadcasted_iota(jnp.int32, (1, SB_WIDTH), 1) // HEAD_DIM
    row = lax.broadcasted_iota(jnp.int32, (t, t), 0)
    col = lax.broadcasted_iota(jnp.int32, (t, t), 1)
    later = jnp.where(row > col, 1.0, 0.0).astype(_MXU)
    causal = col < row
    q = q * (HEAD_DIM ** -0.5)
    heads = range(SB_HEADS)
    head_rows = [slice(h * t, (h + 1) * t) for h in heads]
    q_stack = jnp.concatenate([jnp.where(lane_head == h, q, jnp.zeros_like(q)) for h in heads], axis=0)

    def block(jb, acc, carries, diagonal):
        ktb = kt_ref[0, jb]
        vb = v_ref[0, pl.ds(pl.multiple_of(jb * t, t), t), :]
        zz = _dot(q_stack, ktb)
        zs = [zz[rows] for rows in head_rows]
        loms = [-_softplus(z) for z in zs]
        if diagonal:
            loms = [jnp.where(causal, lom, 0.0) for lom in loms]
        tails = _dot(jnp.concatenate([lom.astype(_MXU) for lom in loms], axis=0), later)
        weights, new_carries = [], []
        for h in heads:
            tail = tails[head_rows[h]] + carries[h]
            a = jnp.exp(zs[h] + loms[h] + tail)
            if diagonal:
                a = jnp.where(causal, a, 0.0)
            weights.append(a.astype(_MXU))
            new_carries.append(carries[h] + jnp.sum(loms[h], axis=1, keepdims=True))
        v_heads = jnp.concatenate(
            [jnp.where(lane_head == h, vb, jnp.zeros_like(vb)) for h in range(SB_HEADS)], axis=0)
        acc = acc + _dot(jnp.concatenate(weights, axis=1), v_heads)
        return acc, tuple(new_carries)

    zero = jnp.zeros((t, 1), _F32)
    acc, carries = block(i, jnp.zeros((t, SB_WIDTH), _F32), (zero,) * SB_HEADS, True)

    def live(state):
        jb, _, carries = state
        top = functools.reduce(jnp.maximum, carries)
        return (jb >= 0) & (jnp.max(top) > SB_SKIP_LOG)

    def step(state):
        jb, acc, carries = state
        acc, carries = block(jb, acc, carries, False)
        return jb - 1, acc, carries

    _, acc, _ = lax.while_loop(live, step, (i - 1, acc, carries))
    o_ref[0] = acc.astype(o_ref.dtype)


def _sb_attention(sb, kt):
    b, s, _ = sb.shape
    t = SB_TILE
    return pl.pallas_call(
        _sb_body,
        grid=(b, s // t),
        in_specs=[
            pl.BlockSpec((1, t, SB_WIDTH), lambda bi, i: (bi, i, 0)),
            pl.BlockSpec((1, s // t, SB_WIDTH, t), lambda bi, i: (bi, 0, 0, 0)),
            pl.BlockSpec((1, s, SB_WIDTH), lambda bi, i: (bi, 0, 2)),
        ],
        out_specs=pl.BlockSpec((1, t, SB_WIDTH), lambda bi, i: (bi, i, 0)),
        out_shape=jax.ShapeDtypeStruct((b, s, SB_WIDTH), _MXU),
        compiler_params=_params("parallel", "arbitrary"),
        name="sb_attn",
    )(sb, kt, sb)


def _sw_body(sinks_ref, q_ref, kp_ref, kc_ref, vp_ref, vc_ref, o_ref):
    t = SW_TILE
    assert t == WINDOW
    i = pl.program_id(1)
    kcat = jnp.concatenate([kp_ref[0], kc_ref[0]], axis=0)
    vcat = jnp.concatenate([vp_ref[0], vc_ref[0]], axis=0)
    r_i = lax.broadcasted_iota(jnp.int32, (t, t), 0)
    c_i = lax.broadcasted_iota(jnp.int32, (t, t), 1)
    own = c_i <= r_i
    distf = jnp.where(own, r_i - c_i, r_i - c_i + t).astype(_F32)
    lane_kv = lax.broadcasted_iota(jnp.int32, (1, LANES), 1) // HEAD_DIM
    scale = HEAD_DIM ** -0.5
    grp = SW_HEADS // SW_KV_HEADS
    pairs = [(g, kv) for g in range(grp) for kv in range(SW_KV_HEADS)]
    head_rows = [slice(n * t, (n + 1) * t) for n in range(len(pairs))]
    for j in range(SW_STEP_TILES):
        rows = slice(j * t, (j + 1) * t)
        kj, vj = kcat[j * t:(j + 2) * t], vcat[j * t:(j + 2) * t]
        valid = own | (i * SW_STEP_TILES + j > 0)
        q_stack = []
        for g, kv in pairs:
            qp = q_ref[0, rows, g * LANES:(g + 1) * LANES] * scale
            q_stack.append(jnp.where(lane_kv == kv, qp, jnp.zeros_like(qp)))
        zz = _dot_nt(jnp.concatenate(q_stack, axis=0), kj)
        probs, denoms = [], []
        for n, (g, kv) in enumerate(pairs):
            head = kv * grp + g
            slope = 2.0 ** (-8.0 * (head + 1.0) / SW_HEADS)
            zh = zz[head_rows[n]]
            z = jnp.where(own, zh[:, t:], zh[:, :t]) - slope * distf
            z = jnp.where(valid, z, -1e30)
            sink = sinks_ref[head]
            m = jnp.maximum(jnp.max(z, axis=1, keepdims=True), sink)
            p = jnp.exp(z - m)
            denoms.append(jnp.sum(p, axis=1, keepdims=True) + jnp.exp(sink - m))
            p = p.astype(_MXU)
            zero = jnp.zeros_like(p)
            probs.append(jnp.concatenate([jnp.where(own, zero, p), jnp.where(own, p, zero)], axis=1))
        oo = _dot(jnp.concatenate(probs, axis=0), vj)
        for g in range(grp):
            first, second = (pairs.index((g, kv)) for kv in range(SW_KV_HEADS))
            res = jnp.where(lane_kv == 0, oo[head_rows[first]] / denoms[first],
                            oo[head_rows[second]] / denoms[second])
            o_ref[0, rows, g * LANES:(g + 1) * LANES] = res.astype(o_ref.dtype)


def _sw_attention(sw, sinks):
    b, s, _ = sw.shape
    t = SW_TILE
    n = SW_STEP_TILES * t
    qb = SW_WIDTH // LANES
    cur = lambda c: (lambda bi, i: (bi, i, c))
    prev = lambda c: (lambda bi, i: (bi, jnp.maximum(i * SW_STEP_TILES - 1, 0), c))
    return pl.pallas_call(
        _sw_body,
        grid=(b, s // n),
        in_specs=[
            pl.BlockSpec(memory_space=pltpu.SMEM),
            pl.BlockSpec((1, n, SW_WIDTH), lambda bi, i: (bi, i, 0)),
            pl.BlockSpec((1, t, LANES), prev(qb)),
            pl.BlockSpec((1, n, LANES), cur(qb)),
            pl.BlockSpec((1, t, LANES), prev(qb + 1)),
            pl.BlockSpec((1, n, LANES), cur(qb + 1)),
        ],
        out_specs=pl.BlockSpec((1, n, SW_WIDTH), lambda bi, i: (bi, i, 0)),
        out_shape=jax.ShapeDtypeStruct((b, s, SW_WIDTH), _MXU),
        compiler_params=_params("parallel", "arbitrary"),
        name="sw_attn",
    )(sinks, sw, sw, sw, sw, sw)


def _rw_mm(a, b):
    return _dot(a.astype(_MXU), b.astype(_MXU))


def _rw_mm_nt(a, b):
    return _dot_nt(a.astype(_MXU), b.astype(_MXU))


def _head_stack(x, lane_head):
    return jnp.concatenate(
        [jnp.where(lane_head == h, x, 0.0) for h in range(RW_HEADS)], axis=0)


def _rw_chunk_ops(r, kp, v, kk, b, lw):
    cl = RW_CHUNK
    chunks = range(len(r))
    each = lambda f, *xs: [f(*a) for a in zip(*xs)]

    lane_head = lax.broadcasted_iota(jnp.int32, (1, RW_WIDTH), 1) // HEAD_DIM
    tr = lax.broadcasted_iota(jnp.int32, (cl, cl), 0)
    tc = lax.broadcasted_iota(jnp.int32, (cl, cl), 1)
    lower = jnp.where(tr >= tc, 1.0, 0.0)
    cum = each(lambda x: _dot(lower, x, _HI), lw)
    yield
    cum_end = each(lambda x: x[cl - 1:cl, :], cum)
    kk_t = each(lambda x, c, l: x * jnp.exp(c - l), kk, cum, lw)
    r_t = each(lambda x, c: x * jnp.exp(c), r, cum)
    inv = each(lambda c: jnp.exp(-c), cum)
    k_h = each(jnp.multiply, kp, inv)
    b_h = each(jnp.multiply, b, inv)
    yield
    to_end = each(lambda e, c: jnp.exp(e - c), cum_end, cum)
    gt = each(lambda k, bb, e: jnp.concatenate([k * e, -(bb * e)], axis=0).T.astype(_MXU),
              kp, b, to_end)
    pct = each(lambda e: jnp.broadcast_to(jnp.exp(e), (2 * cl, RW_WIDTH)).T, cum_end)
    yield

    stack = lambda x: _head_stack(x, lane_head)
    khs, bhs, vs = each(stack, k_h), each(stack, b_h), each(stack, v)
    t_i = lax.broadcasted_iota(jnp.int32, (cl, RW_WIDTH), 0)
    s_i = lax.broadcasted_iota(jnp.int32, (cl, RW_WIDTH), 1) % cl
    strict, incl = t_i > s_i, t_i >= s_i
    lhs = each(lambda x, y: jnp.concatenate([x, y], axis=0), kk_t, r_t)
    pk = each(_rw_mm_nt, lhs, khs)
    yield
    pb = each(_rw_mm_nt, lhs, bhs)
    yield
    a_b = each(lambda p: jnp.where(strict, p[:cl], 0.0), pb)
    m_b = each(lambda p: jnp.where(incl, p[cl:], 0.0).astype(_MXU), pb)
    akm = each(lambda p: jnp.concatenate(
        [jnp.where(strict, p[:cl], 0.0), jnp.where(incl, p[cl:], 0.0)], axis=0), pk)
    kv = each(_rw_mm, akm, vs)
    yield

    same_head = _head_block_ones()

    def block_diag(x_cat):
        return jnp.where(same_head, jnp.concatenate([x_cat] * RW_HEADS, axis=0), 0.0)

    eye = jnp.where(t_i == s_i, 1.0, 0.0)
    pair = ((t_i ^ s_i) == 1) & ((t_i & 1) != 0)
    t_cat = each(lambda a: eye - jnp.where(pair, a, 0.0), a_b)
    m = 2
    while m < cl:
        below = (((t_i ^ s_i) & -(2 * m)) == 0) & ((t_i & m) != 0) & ((s_i & m) == 0)
        x = each(lambda t, a: _rw_mm(t, block_diag(jnp.where(below, a, 0.0))), t_cat, a_b)
        yield
        t_cat = each(lambda t, xx: t - _rw_mm(xx, block_diag(t)), t_cat, x)
        yield
        m *= 2
    w = each(lambda t, k, p: _rw_mm(t, jnp.concatenate([stack(k), stack(p[:cl])], axis=1)),
             t_cat, kk_t, kv)
    return [(jnp.concatenate([w[c][:, :RW_WIDTH], r_t[c]], axis=0).astype(_MXU),
             w[c][:, RW_WIDTH:], kv[c][cl:], m_b[c], gt[c], pct[c]) for c in chunks]


def _advance(staged, stages=None):
    try:
        while stages is None or stages > 0:
            next(staged)
            stages = None if stages is None else stages - 1
    except StopIteration as done:
        return done.value
    return None


def _rw_core_body(r_ref, kp_ref, v_ref, kk_ref, b_ref, lw_ref, g_ref, gng_ref, gnb_ref, rk_ref,
                  o_ref, h_ref, y_ref):
    cl = RW_CHUNK

    @pl.when(pl.program_id(0) == 0)
    def _():
        h_ref[...] = jnp.zeros_like(h_ref)

    lane_head = lax.broadcasted_iota(jnp.int32, (1, RW_WIDTH), 1) // HEAD_DIM
    same_head = _head_block_ones()
    batch = range(h_ref.shape[0])
    rows = lambda c: slice(c * cl, (c + 1) * cl)
    groups = [range(g, g + RW_GROUP) for g in range(0, RW_STEP_CHUNKS, RW_GROUP)]

    def staged_ops(group):
        load = lambda ref: [ref[bi, rows(c)] for c in group for bi in batch]
        return _rw_chunk_ops(*(load(x) for x in (r_ref, kp_ref, v_ref, kk_ref, b_ref, lw_ref)))

    def state_step(c, ops):
        h = [h_ref[bi] for bi in batch]
        uy = [_dot(ops[bi][0], h[bi].astype(_MXU)) for bi in batch]
        u = [uy[bi][:cl] + ops[bi][1] for bi in batch]
        vu = [jnp.concatenate([v_ref[bi, rows(c)], u[bi]], axis=0) for bi in batch]
        upd = [_dot(ops[bi][4], vu[bi].astype(_MXU)) for bi in batch]
        for bi in batch:
            pct = ops[bi][5]
            h_ref[bi] = (jnp.concatenate([pct, pct], axis=1) * h[bi]
                         + jnp.where(same_head, upd[bi], 0.0))
        for bi in batch:
            y_ref[bi, rows(c)] = (uy[bi][cl:] + ops[bi][2]
                                  - _rw_mm(ops[bi][3], _head_stack(u[bi], lane_head)))

    ops = _advance(staged_ops(groups[0]))
    for gi, group in enumerate(groups):
        following = staged_ops(groups[gi + 1]) if gi + 1 < len(groups) else None
        ready = None
        for k, c in enumerate(group):
            if following is not None and ready is None:
                ready = _advance(following, RW_STAGES_PER_STATE_STEP)
            state_step(c, ops[k * len(batch):(k + 1) * len(batch)])
        if following is not None:
            ops = ready if ready is not None else _advance(following)

    mean_mat = jnp.where(same_head, 1.0 / HEAD_DIM, 0.0).astype(_MXU)
    ones = jnp.where(same_head, 1.0, 0.0).astype(_MXU)
    for bi in batch:
        y = y_ref[bi]
        d = y - _head_sum(y, mean_mat)
        var = _head_sum(d * d, mean_mat)
        yn = d * lax.rsqrt(var + RW_GN_EPS) * gng_ref[...] + gnb_ref[...]
        bonus = _head_sum(r_ref[bi] * kp_ref[bi] * rk_ref[...], ones) * v_ref[bi]
        o_ref[bi] = ((yn + bonus) * g_ref[bi]).astype(o_ref.dtype)


def _rwkv7(r, kp, v, kk, b, lw, g, r_k, gn_g, gn_b):
    bsz, s, _ = v.shape
    n = RW_STEP_CHUNKS * RW_CHUNK
    blk = pl.BlockSpec((bsz, n, RW_WIDTH), lambda i: (0, i, 0))
    vec = _const_spec((1, RW_WIDTH))
    return pl.pallas_call(
        _rw_core_body,
        grid=(s // n,),
        in_specs=[blk] * 7 + [vec] * 3,
        out_specs=blk,
        out_shape=jax.ShapeDtypeStruct((bsz, s, RW_WIDTH), _MXU),
        scratch_shapes=[pltpu.VMEM((bsz, RW_WIDTH, RW_WIDTH), _F32),
                        pltpu.VMEM((bsz, n, RW_WIDTH), _F32)],
        compiler_params=_params("arbitrary"),
        name="rw_core",
    )(r, kp, v, kk, b, lw, g, gn_g, gn_b, r_k)


_SW_PAIR_ORDER = tuple(h for g in range(SW_HEADS // SW_KV_HEADS)
                       for h in (g, g + SW_HEADS // SW_KV_HEADS))


def _pad_rows(w, rows):
    return jnp.pad(w, ((0, rows - w.shape[0]), (0, 0)))


def _pair_heads(w, axis):
    shape = w.shape
    w = w.reshape(shape[:axis] + (SW_HEADS, HEAD_DIM) + shape[axis + 1:])
    w = jnp.take(w, jnp.array(_SW_PAIR_ORDER), axis=axis)
    return w.reshape(shape)


def _mix_in_layout(w_in, mu):
    pad = lambda t, width: jnp.pad(t, [(0, 0)] * (t.ndim - 1) + [(0, width - t.shape[-1])])

    def rw_groups(t):
        c = 3 * RW_WIDTH
        return [t[..., :c], pad(t[..., c:c + DECAY_LORA], LANES),
                pad(t[..., c + DECAY_LORA:c + DECAY_LORA + AAA_LORA], LANES),
                pad(t[..., c + DECAY_LORA + AAA_LORA:], 2 * LANES)]

    rw_end = SB_IN_WIDTH + 3 * RW_WIDTH + DECAY_LORA + AAA_LORA + GATE_LORA
    w = jnp.concatenate(
        [w_in[..., :SB_IN_WIDTH]] + rw_groups(w_in[..., SB_IN_WIDTH:rw_end])
        + [_pair_heads(w_in[..., rw_end:rw_end + SW_WIDTH], 2), w_in[..., rw_end + SW_WIDTH:]],
        axis=-1).astype(_MXU)
    return w, jnp.concatenate(rw_groups(mu[:, None, :]), axis=-1)


def kernel(x, ffn1_norm, ffn1_w_in, ffn1_w_out, mix_norm, mix_w_in, mix_w_out, rw_mu, rw_w0,
           rw_w_up, rw_a0, rw_a_up, rw_g_up, rw_k_k, rw_k_a, rw_r_k, rw_gn_g, rw_gn_b, sw_sinks,
           ffn2_norm, ffn2_w_in, ffn2_w_out, final_norm):
    bsz, s, d = x.shape
    xf = x.reshape(bsz * s, d)
    row = lambda t: t[None, :]
    final_g = row(final_norm)
    w1_in, w1_out = ffn1_w_in.astype(_MXU), ffn1_w_out.astype(_MXU)
    w2_in, w2_out = ffn2_w_in.astype(_MXU), ffn2_w_out.astype(_MXU)
    w_mix_in, mu_all = _mix_in_layout(mix_w_in, rw_mu)
    mix_rows = SB_WIDTH + RW_WIDTH
    w_mix_out = jnp.concatenate(
        [mix_w_out[:, :mix_rows], _pair_heads(mix_w_out[:, mix_rows:], 1)], axis=1).astype(_MXU)
    for l in range(DEPTH):
        xf = _ffn(xf, row(ffn1_norm[l]), w1_in, w1_out, final_g, False, l)
        sb, kt, sw, *rw_parts = _inproj(
            xf, row(mix_norm[l]), w_mix_in, mu_all[l], row(rw_w0[l]), _pad_rows(rw_w_up[l], LANES),
            row(rw_a0[l]), _pad_rows(rw_a_up[l], LANES), _pad_rows(rw_g_up[l], 2 * LANES),
            row(rw_k_k[l]), row(rw_k_a[l]), s, l)
        sb = sb.reshape(bsz, s, SB_IN_WIDTH)
        sw = sw.reshape(bsz, s, SW_IN_WIDTH)
        sb_out = _sb_attention(sb, kt.reshape(bsz, s // SB_TILE, SB_WIDTH, SB_TILE))
        rw_out = _rwkv7(
            *(t.reshape(bsz, s, RW_WIDTH) for t in rw_parts),
            rw_r_k[l].reshape(1, RW_WIDTH), row(rw_gn_g[l]), row(rw_gn_b[l]))
        sw_out = _sw_attention(sw, sw_sinks[l])
        mix = (sb_out.reshape(bsz * s, SB_WIDTH), rw_out.reshape(bsz * s, RW_WIDTH),
               sw_out.reshape(bsz * s, SW_WIDTH), w_mix_out)
        xf = _ffn(xf, row(ffn2_norm[l]), w2_in, w2_out, final_g, l == DEPTH - 1, l, mix)
    return xf.reshape(bsz, s, d)
```

```python
import functools

import jax
import jax.numpy as jnp
from jax import lax
from jax.experimental import pallas as pl
from jax.experimental.pallas import tpu as pltpu

D_MODEL = 1024
DEPTH = 4
HEAD_DIM = 64
SB_HEADS = 4
SB_WIDTH = SB_HEADS * HEAD_DIM
RW_HEADS = 4
RW_WIDTH = RW_HEADS * HEAD_DIM
DECAY_LORA = 64
AAA_LORA = 64
GATE_LORA = 160
RW_GN_EPS = 64e-5
SW_HEADS = 8
SW_KV_HEADS = 2
SW_WIDTH = SW_HEADS * HEAD_DIM
SW_KV_WIDTH = SW_KV_HEADS * HEAD_DIM
WINDOW = 128
D_FF = 2816
NORM_EPS = 1e-6

LANES = 128
MXU_DIM = 256
RW_PAD_WIDTH = 3 * RW_WIDTH + 2 * LANES + 2 * LANES
SB_IN_WIDTH = 3 * SB_WIDTH
SW_IN_WIDTH = SW_WIDTH + 2 * SW_KV_WIDTH
IN_PAD_WIDTH = SB_IN_WIDTH + RW_PAD_WIDTH + SW_IN_WIDTH

FFN_ROWS = 1024
PROJ_ROWS = 1024
PROJ_SUB = 256
FF_TILE = MXU_DIM
SB_TILE = 256
SW_TILE = WINDOW
MIX_ROWS = 512
RW_CHUNK = 64
RW_STEP_CHUNKS = 8
RW_GROUP = 4
RW_STAGES_PER_STATE_STEP = 4

VMEM_LIMIT_BYTES = 56 * 1024 * 1024

_MXU = jnp.bfloat16
_F32 = jnp.float32
_HI = lax.Precision.HIGHEST


def _dot(a, b, precision=None):
    return jnp.dot(a, b, preferred_element_type=_F32, precision=precision)


def _dot_nt(a, b, precision=None):
    return lax.dot_general(a, b, (((1,), (1,)), ((), ())),
                           preferred_element_type=_F32, precision=precision)


def _sigmoid(x):
    return 1.0 / (1.0 + jnp.exp(-x))


def _softplus(x):
    return jnp.maximum(x, 0.0) + jnp.log(1.0 + jnp.exp(-jnp.abs(x)))


def _params(*sem):
    return pltpu.CompilerParams(dimension_semantics=sem, vmem_limit_bytes=VMEM_LIMIT_BYTES)


def _const_spec(shape):
    nd = len(shape)
    return pl.BlockSpec(shape, lambda *_: (0,) * nd, pipeline_mode=pl.Buffered(1))


def _layer_spec(shape, layer, row_block=0):
    return pl.BlockSpec((None,) + shape, lambda *_: (layer, row_block, 0),
                        pipeline_mode=pl.Buffered(1))


def _ffn_math(x, g_ref, win_ref, wout_ref, fg_ref, act_ref, final, between=()):
    between = list(between)
    ms = jnp.mean(x * x, axis=-1, keepdims=True)
    hn = (x * lax.rsqrt(ms + NORM_EPS) * g_ref[...]).astype(_MXU)
    for c in range(D_FF // FF_TILE):
        lo, hi = c * FF_TILE, (c + 1) * FF_TILE
        gate = _dot(hn, win_ref[:, lo:hi])
        up = _dot(hn, win_ref[:, D_FF + lo:D_FF + hi])
        act_ref[:, lo:hi] = (gate * _sigmoid(gate) * up).astype(_MXU)
        if between:
            between.pop(0)()
    for thunk in between:
        thunk()
    y = x + 0.5 * _dot(act_ref[...], wout_ref[...])
    if final:
        ms = jnp.mean(y * y, axis=-1, keepdims=True)
        y = y * lax.rsqrt(ms + NORM_EPS) * fg_ref[...]
    return y


def _ffn_body(x_ref, g_ref, win_ref, wout_ref, fg_ref, o_ref, act_ref):
    o_ref[...] = _ffn_math(x_ref[...], g_ref, win_ref, wout_ref, fg_ref, act_ref, False)


def _ffn(x, g, w_in, w_out, final_g, layer):
    m = x.shape[0]
    row = lambda i: (i, 0)
    return pl.pallas_call(
        _ffn_body,
        grid=(m // FFN_ROWS,),
        in_specs=[
            pl.BlockSpec((FFN_ROWS, D_MODEL), row),
            _const_spec((1, D_MODEL)),
            _layer_spec((D_MODEL, 2 * D_FF), layer),
            _layer_spec((D_FF, D_MODEL), layer),
            _const_spec((1, D_MODEL)),
        ],
        out_specs=pl.BlockSpec((FFN_ROWS, D_MODEL), row),
        out_shape=jax.ShapeDtypeStruct((m, D_MODEL), _F32),
        scratch_shapes=[pltpu.VMEM((FFN_ROWS, D_FF), _MXU)],
        compiler_params=_params("parallel"),
        name="ffn",
    )(x, g, w_in, w_out, final_g)


def _ffn_mix_body(sinks_ref, x_ref, sb_ref, rw_ref, swq_ref, swkv_ref, swkvp_ref,
                  wsb_ref, wrw_ref, wsw_ref, g_ref, win_ref, wout_ref, fg_ref,
                  o_ref, act_ref, swo_ref, *, final, seq_tiles, last_tile):
    s = pl.program_id(0)
    t = SW_TILE
    n_sub = swq_ref.shape[0] // t
    tile = jnp.minimum(s, last_tile)
    consts = _sw_consts()

    def attend(slot, j):
        rows = slice(j * t, (j + 1) * t)
        before = swkvp_ref[...] if j == 0 else swkv_ref[(j - 1) * t:j * t]
        kv = jnp.concatenate([before, swkv_ref[rows]], axis=0)
        has_prev = (tile * n_sub + j) % seq_tiles != 0
        swo_ref[slot, rows] = _sw_tile(swq_ref[rows], kv[:, :SW_KV_WIDTH], kv[:, SW_KV_WIDTH:],
                                       has_prev, sinks_ref, consts)

    @pl.when(s == 0)
    def _():
        for j in range(n_sub):
            attend(0, j)

    @pl.when(s > 0)
    def _():
        x = (x_ref[...] + _dot(sb_ref[...], wsb_ref[...]) + _dot(rw_ref[...], wrw_ref[...])
             + _dot(swo_ref[(s - 1) % 2], wsw_ref[...]))
        o_ref[...] = _ffn_math(
            x, g_ref, win_ref, wout_ref, fg_ref, act_ref, final,
            between=[functools.partial(attend, s % 2, j) for j in range(n_sub)])


def _ffn_mix(x, g, w_in, w_out, final_g, final, layer, sb, rw, sw, w_mix, sinks, seq_len):
    m = x.shape[0]
    r = MIX_ROWS
    n = m // r
    cur = lambda s: (jnp.maximum(s - 1, 0), 0)
    ahead = lambda s: jnp.minimum(s, n - 1)
    kv_block = SW_WIDTH // (2 * SW_KV_WIDTH)
    return pl.pallas_call(
        functools.partial(_ffn_mix_body, final=final, seq_tiles=seq_len // SW_TILE, last_tile=n - 1),
        grid=(n + 1,),
        in_specs=[
            pl.BlockSpec(memory_space=pltpu.SMEM),
            pl.BlockSpec((r, D_MODEL), cur),
            pl.BlockSpec((r, SB_WIDTH), cur),
            pl.BlockSpec((r, RW_WIDTH), cur),
            pl.BlockSpec((r, SW_WIDTH), lambda s: (ahead(s), 0)),
            pl.BlockSpec((r, 2 * SW_KV_WIDTH), lambda s: (ahead(s), kv_block)),
            pl.BlockSpec((SW_TILE, 2 * SW_KV_WIDTH),
                         lambda s: (jnp.maximum(ahead(s) * (r // SW_TILE) - 1, 0), kv_block)),
            _layer_spec((SB_WIDTH, D_MODEL), layer, 0),
            _layer_spec((RW_WIDTH, D_MODEL), layer, SB_WIDTH // RW_WIDTH),
            _layer_spec((SW_WIDTH, D_MODEL), layer, (SB_WIDTH + RW_WIDTH) // SW_WIDTH),
            _const_spec((1, D_MODEL)),
            _layer_spec((D_MODEL, 2 * D_FF), layer),
            _layer_spec((D_FF, D_MODEL), layer),
            _const_spec((1, D_MODEL)),
        ],
        out_specs=pl.BlockSpec((r, D_MODEL), cur),
        out_shape=jax.ShapeDtypeStruct((m, D_MODEL), _F32),
        scratch_shapes=[pltpu.VMEM((r, D_FF), _MXU), pltpu.VMEM((2, r, SW_WIDTH), _MXU)],
        compiler_params=_params("arbitrary"),
        name="ffn_mix",
    )(sinks, x, sb, rw, sw, sw, sw, w_mix, w_mix, w_mix, g, w_in, w_out, final_g)


def _head_block_ones():
    r = lax.broadcasted_iota(jnp.int32, (RW_WIDTH, RW_WIDTH), 0) // HEAD_DIM
    c = lax.broadcasted_iota(jnp.int32, (RW_WIDTH, RW_WIDTH), 1) // HEAD_DIM
    return r == c


def _head_sum(x, mat):
    hi = x.astype(_MXU)
    lo = (x - hi.astype(_F32)).astype(_MXU)
    return _dot(hi, mat) + _dot(lo, mat)


def _inproj_body(x_ref, g_ref, w_ref, mu_ref, w0_ref, wup_ref, a0_ref, aup_ref, gup_ref,
                 kk_ref, ka_ref, sb_ref, kt_ref, sw_ref, r_o, kp_o, v_o, kkn_o, b_o, lw_o, g_o,
                 last_ref, *, seq_tiles):
    i = pl.program_id(0)
    sub = PROJ_SUB
    assert sub == SB_TILE
    n_sub = x_ref.shape[0] // sub
    c = RW_WIDTH
    ones = jnp.where(_head_block_ones(), 1.0, 0.0).astype(_MXU)
    first = lax.broadcasted_iota(jnp.int32, (sub, 1), 0) == 0

    @pl.when(i == 0)
    def _():
        last_ref[...] = jnp.zeros_like(last_ref)

    def project(j):
        rows = slice(j * sub, (j + 1) * sub)
        x = x_ref[rows]
        ms = jnp.mean(x * x, axis=-1, keepdims=True)
        hn = (x * lax.rsqrt(ms + NORM_EPS) * g_ref[...]).astype(_MXU)
        sb = _dot(hn, w_ref[:, :SB_IN_WIDTH])
        sb_ref[rows] = sb.astype(sb_ref.dtype)
        kt_ref[j] = sb[:, SB_WIDTH:2 * SB_WIDTH].T.astype(kt_ref.dtype)
        sw_ref[rows] = _dot(hn, w_ref[:, SB_IN_WIDTH + RW_PAD_WIDTH:]).astype(sw_ref.dtype)
        return _dot(hn, w_ref[:, SB_IN_WIDTH:SB_IN_WIDTH + RW_PAD_WIDTH])

    def prepare(j, p, prev_last):
        rows = slice(j * sub, (j + 1) * sub)
        pprev = jnp.where(first, prev_last, pltpu.roll(p, 1, 0))
        xm = p + (pprev - p) * mu_ref[...]
        r, k, v = xm[:, :c], xm[:, c:2 * c], xm[:, 2 * c:3 * c]
        xw = xm[:, 3 * c:3 * c + LANES]
        xa = xm[:, 3 * c + LANES:3 * c + 2 * LANES]
        xg = xm[:, 3 * c + 2 * LANES:]
        log_w = -_softplus(-(w0_ref[...] + _rw_mm(jnp.tanh(xw), wup_ref[...]))) - 0.5
        a = _sigmoid(a0_ref[...] + _rw_mm(xa, aup_ref[...]))
        g = _rw_mm(_sigmoid(xg), gup_ref[...])
        kk = k * kk_ref[...]
        n2 = _head_sum(kk * kk, ones)
        kk = kk * lax.rsqrt(jnp.maximum(n2, 1e-24))
        r_o[rows] = r
        kp_o[rows] = k * (1.0 + (a - 1.0) * ka_ref[...])
        v_o[rows] = v
        kkn_o[rows] = kk
        b_o[rows] = kk * a
        lw_o[rows] = -jnp.exp(log_w)
        g_o[rows] = g

    prev_last = jnp.where(i % seq_tiles == 0, 0.0, last_ref[7:8, :])
    ps = [project(0)]
    for j in range(n_sub):
        if j + 1 < n_sub:
            ps.append(project(j + 1))
        prepare(j, ps[j], prev_last)
        prev_last = ps[j][sub - 1:sub, :]
    last_ref[...] = ps[-1][sub - 8:, :]


def _inproj(x, g, w, mu, w0, wup, a0, aup, gup, k_k, k_a, seq_len, layer):
    m = x.shape[0]
    n = PROJ_ROWS
    row = lambda i: (i, 0)
    vec = _const_spec((1, RW_WIDTH))
    rw_out = pl.BlockSpec((n, RW_WIDTH), row)
    return pl.pallas_call(
        functools.partial(_inproj_body, seq_tiles=seq_len // n),
        grid=(m // n,),
        in_specs=[
            pl.BlockSpec((n, D_MODEL), row),
            _const_spec((1, D_MODEL)),
            _layer_spec((D_MODEL, IN_PAD_WIDTH), layer),
            _const_spec((1, RW_PAD_WIDTH)),
            vec, _const_spec((LANES, RW_WIDTH)),
            vec, _const_spec((LANES, RW_WIDTH)),
            _const_spec((2 * LANES, RW_WIDTH)),
            vec, vec,
        ],
        out_specs=[pl.BlockSpec((n, SB_IN_WIDTH), row),
                   pl.BlockSpec((n // SB_TILE, SB_WIDTH, SB_TILE), lambda i: (i, 0, 0)),
                   pl.BlockSpec((n, SW_IN_WIDTH), row)]
        + [rw_out] * 7,
        out_shape=[jax.ShapeDtypeStruct((m, SB_IN_WIDTH), _MXU),
                   jax.ShapeDtypeStruct((m // SB_TILE, SB_WIDTH, SB_TILE), _MXU),
                   jax.ShapeDtypeStruct((m, SW_IN_WIDTH), _MXU)]
        + [jax.ShapeDtypeStruct((m, RW_WIDTH), _F32)] * 7,
        scratch_shapes=[pltpu.VMEM((8, RW_PAD_WIDTH), _F32)],
        compiler_params=_params("arbitrary"),
        name="inproj",
    )(x, g, w, mu, w0, wup, a0, aup, gup, k_k, k_a)


SB_SKIP_LOG = -88.0


def _sb_body(q_ref, kt_ref, v_ref, o_ref):
    t = SB_TILE
    i = pl.program_id(1)
    q = q_ref[0]
    lane_head = lax.broadcasted_iota(jnp.int32, (1, SB_WIDTH), 1) // HEAD_DIM
    row = lax.broadcasted_iota(jnp.int32, (t, t), 0)
    col = lax.broadcasted_iota(jnp.int32, (t, t), 1)
    later = jnp.where(row > col, 1.0, 0.0).astype(_MXU)
    causal = col < row
    q = q * (HEAD_DIM ** -0.5)
    heads = range(SB_HEADS)
    head_rows = [slice(h * t, (h + 1) * t) for h in heads]
    q_stack = jnp.concatenate([jnp.where(lane_head == h, q, jnp.zeros_like(q)) for h in heads], axis=0)

    def block(jb, acc, carries, diagonal):
        ktb = kt_ref[0, jb]
        vb = v_ref[0, pl.ds(pl.multiple_of(jb * t, t), t), :]
        zz = _dot(q_stack, ktb)
        zs = [zz[rows] for rows in head_rows]
        loms = [-_softplus(z) for z in zs]
        if diagonal:
            loms = [jnp.where(causal, lom, 0.0) for lom in loms]
        tails = _dot(jnp.concatenate([lom.astype(_MXU) for lom in loms], axis=0), later)
        weights, new_carries = [], []
        for h in heads:
            tail = tails[head_rows[h]] + carries[h]
            a = jnp.exp(zs[h] + loms[h] + tail)
            if diagonal:
                a = jnp.where(causal, a, 0.0)
            weights.append(a.astype(_MXU))
            new_carries.append(carries[h] + jnp.sum(loms[h], axis=1, keepdims=True))
        v_heads = jnp.concatenate(
            [jnp.where(lane_head == h, vb, jnp.zeros_like(vb)) for h in range(SB_HEADS)], axis=0)
        acc = acc + _dot(jnp.concatenate(weights, axis=1), v_heads)
        return acc, tuple(new_carries)

    zero = jnp.zeros((t, 1), _F32)
    acc, carries = block(i, jnp.zeros((t, SB_WIDTH), _F32), (zero,) * SB_HEADS, True)

    def live(state):
        jb, _, carries = state
        top = functools.reduce(jnp.maximum, carries)
        return (jb >= 0) & (jnp.max(top) > SB_SKIP_LOG)

    def step(state):
        jb, acc, carries = state
        acc, carries = block(jb, acc, carries, False)
        return jb - 1, acc, carries

    _, acc, _ = lax.while_loop(live, step, (i - 1, acc, carries))
    o_ref[0] = acc.astype(o_ref.dtype)


def _sb_attention(sb, kt):
    b, s, _ = sb.shape
    t = SB_TILE
    return pl.pallas_call(
        _sb_body,
        grid=(b, s // t),
        in_specs=[
            pl.BlockSpec((1, t, SB_WIDTH), lambda bi, i: (bi, i, 0)),
            pl.BlockSpec((1, s // t, SB_WIDTH, t), lambda bi, i: (bi, 0, 0, 0)),
            pl.BlockSpec((1, s, SB_WIDTH), lambda bi, i: (bi, 0, 2)),
        ],
        out_specs=pl.BlockSpec((1, t, SB_WIDTH), lambda bi, i: (bi, i, 0)),
        out_shape=jax.ShapeDtypeStruct((b, s, SB_WIDTH), _MXU),
        compiler_params=_params("parallel", "arbitrary"),
        name="sb_attn",
    )(sb, kt, sb)


def _sw_consts():
    t = SW_TILE
    assert t == WINDOW
    r_i = lax.broadcasted_iota(jnp.int32, (t, t), 0)
    c_i = lax.broadcasted_iota(jnp.int32, (t, t), 1)
    own = c_i <= r_i
    distf = jnp.where(own, r_i - c_i, r_i - c_i + t).astype(_F32)
    lane_kv = lax.broadcasted_iota(jnp.int32, (1, LANES), 1) // HEAD_DIM
    return own, distf, lane_kv


def _sw_tile(q, kj, vj, has_prev, sinks_ref, consts):
    t = SW_TILE
    own, distf, lane_kv = consts
    valid = own | has_prev
    grp = SW_HEADS // SW_KV_HEADS
    pairs = [(g, kv) for g in range(grp) for kv in range(SW_KV_HEADS)]
    head_rows = [slice(n * t, (n + 1) * t) for n in range(len(pairs))]
    q = q * (HEAD_DIM ** -0.5)
    q_stack = []
    for g, kv in pairs:
        qp = q[:, g * LANES:(g + 1) * LANES]
        q_stack.append(jnp.where(lane_kv == kv, qp, jnp.zeros_like(qp)))
    zz = _dot_nt(jnp.concatenate(q_stack, axis=0), kj)
    probs, denoms = [], []
    for n, (g, kv) in enumerate(pairs):
        head = kv * grp + g
        slope = 2.0 ** (-8.0 * (head + 1.0) / SW_HEADS)
        zh = zz[head_rows[n]]
        z = jnp.where(own, zh[:, t:], zh[:, :t]) - slope * distf
        z = jnp.where(valid, z, -1e30)
        sink = sinks_ref[head]
        m = jnp.maximum(jnp.max(z, axis=1, keepdims=True), sink)
        p = jnp.exp(z - m)
        denoms.append(jnp.sum(p, axis=1, keepdims=True) + jnp.exp(sink - m))
        p = p.astype(_MXU)
        zero = jnp.zeros_like(p)
        probs.append(jnp.concatenate([jnp.where(own, zero, p), jnp.where(own, p, zero)], axis=1))
    oo = _dot(jnp.concatenate(probs, axis=0), vj)
    out = []
    for g in range(grp):
        first, second = (pairs.index((g, kv)) for kv in range(SW_KV_HEADS))
        out.append(jnp.where(lane_kv == 0, oo[head_rows[first]] / denoms[first],
                             oo[head_rows[second]] / denoms[second]))
    return jnp.concatenate(out, axis=1).astype(_MXU)


def _rw_mm(a, b):
    return _dot(a.astype(_MXU), b.astype(_MXU))


def _rw_mm_nt(a, b):
    return _dot_nt(a.astype(_MXU), b.astype(_MXU))


def _head_stack(x, lane_head):
    return jnp.concatenate(
        [jnp.where(lane_head == h, x, 0.0) for h in range(RW_HEADS)], axis=0)


def _rw_chunk_ops(r, kp, v, kk, b, lw):
    cl = RW_CHUNK
    chunks = range(len(r))
    each = lambda f, *xs: [f(*a) for a in zip(*xs)]

    lane_head = lax.broadcasted_iota(jnp.int32, (1, RW_WIDTH), 1) // HEAD_DIM
    tr = lax.broadcasted_iota(jnp.int32, (cl, cl), 0)
    tc = lax.broadcasted_iota(jnp.int32, (cl, cl), 1)
    lower = jnp.where(tr >= tc, 1.0, 0.0)
    cum = each(lambda x: _dot(lower, x, _HI), lw)
    yield
    cum_end = each(lambda x: x[cl - 1:cl, :], cum)
    kk_t = each(lambda x, c, l: x * jnp.exp(c - l), kk, cum, lw)
    r_t = each(lambda x, c: x * jnp.exp(c), r, cum)
    inv = each(lambda c: jnp.exp(-c), cum)
    k_h = each(jnp.multiply, kp, inv)
    b_h = each(jnp.multiply, b, inv)
    yield
    to_end = each(lambda e, c: jnp.exp(e - c), cum_end, cum)
    gt = each(lambda k, bb, e: jnp.concatenate([k * e, -(bb * e)], axis=0).T.astype(_MXU),
              kp, b, to_end)
    pct = each(lambda e: jnp.broadcast_to(jnp.exp(e), (2 * cl, RW_WIDTH)).T, cum_end)
    yield

    stack = lambda x: _head_stack(x, lane_head)
    khs, bhs, vs = each(stack, k_h), each(stack, b_h), each(stack, v)
    t_i = lax.broadcasted_iota(jnp.int32, (cl, RW_WIDTH), 0)
    s_i = lax.broadcasted_iota(jnp.int32, (cl, RW_WIDTH), 1) % cl
    strict, incl = t_i > s_i, t_i >= s_i
    lhs = each(lambda x, y: jnp.concatenate([x, y], axis=0), kk_t, r_t)
    pk = each(_rw_mm_nt, lhs, khs)
    yield
    pb = each(_rw_mm_nt, lhs, bhs)
    yield
    a_b = each(lambda p: jnp.where(strict, p[:cl], 0.0), pb)
    m_b = each(lambda p: jnp.where(incl, p[cl:], 0.0).astype(_MXU), pb)
    akm = each(lambda p: jnp.concatenate(
        [jnp.where(strict, p[:cl], 0.0), jnp.where(incl, p[cl:], 0.0)], axis=0), pk)
    kv = each(_rw_mm, akm, vs)
    yield

    same_head = _head_block_ones()

    def block_diag(x_cat):
        return jnp.where(same_head, jnp.concatenate([x_cat] * RW_HEADS, axis=0), 0.0)

    eye = jnp.where(t_i == s_i, 1.0, 0.0)
    pair = ((t_i ^ s_i) == 1) & ((t_i & 1) != 0)
    t_cat = each(lambda a: eye - jnp.where(pair, a, 0.0), a_b)
    m = 2
    while m < cl:
        below = (((t_i ^ s_i) & -(2 * m)) == 0) & ((t_i & m) != 0) & ((s_i & m) == 0)
        x = each(lambda t, a: _rw_mm(t, block_diag(jnp.where(below, a, 0.0))), t_cat, a_b)
        yield
        t_cat = each(lambda t, xx: t - _rw_mm(xx, block_diag(t)), t_cat, x)
        yield
        m *= 2
    w = each(lambda t, k, p: _rw_mm(t, jnp.concatenate([stack(k), stack(p[:cl])], axis=1)),
             t_cat, kk_t, kv)
    return [(jnp.concatenate([w[c][:, :RW_WIDTH], r_t[c]], axis=0).astype(_MXU),
             w[c][:, RW_WIDTH:], kv[c][cl:], m_b[c], gt[c], pct[c]) for c in chunks]


def _advance(staged, stages=None):
    try:
        while stages is None or stages > 0:
            next(staged)
            stages = None if stages is None else stages - 1
    except StopIteration as done:
        return done.value
    return None


def _rw_core_body(r_ref, kp_ref, v_ref, kk_ref, b_ref, lw_ref, g_ref, gng_ref, gnb_ref, rk_ref,
                  o_ref, h_ref, y_ref):
    cl = RW_CHUNK

    @pl.when(pl.program_id(0) == 0)
    def _():
        h_ref[...] = jnp.zeros_like(h_ref)

    lane_head = lax.broadcasted_iota(jnp.int32, (1, RW_WIDTH), 1) // HEAD_DIM
    same_head = _head_block_ones()
    batch = range(h_ref.shape[0])
    rows = lambda c: slice(c * cl, (c + 1) * cl)
    groups = [range(g, g + RW_GROUP) for g in range(0, RW_STEP_CHUNKS, RW_GROUP)]

    def staged_ops(group):
        load = lambda ref: [ref[bi, rows(c)] for c in group for bi in batch]
        return _rw_chunk_ops(*(load(x) for x in (r_ref, kp_ref, v_ref, kk_ref, b_ref, lw_ref)))

    def state_step(c, ops):
        h = [h_ref[bi] for bi in batch]
        uy = [_dot(ops[bi][0], h[bi].astype(_MXU)) for bi in batch]
        u = [uy[bi][:cl] + ops[bi][1] for bi in batch]
        vu = [jnp.concatenate([v_ref[bi, rows(c)], u[bi]], axis=0) for bi in batch]
        upd = [_dot(ops[bi][4], vu[bi].astype(_MXU)) for bi in batch]
        for bi in batch:
            pct = ops[bi][5]
            h_ref[bi] = (jnp.concatenate([pct, pct], axis=1) * h[bi]
                         + jnp.where(same_head, upd[bi], 0.0))
        for bi in batch:
            y_ref[bi, rows(c)] = (uy[bi][cl:] + ops[bi][2]
                                  - _rw_mm(ops[bi][3], _head_stack(u[bi], lane_head)))

    ops = _advance(staged_ops(groups[0]))
    for gi, group in enumerate(groups):
        following = staged_ops(groups[gi + 1]) if gi + 1 < len(groups) else None
        ready = None
        for k, c in enumerate(group):
            if following is not None and ready is None:
                ready = _advance(following, RW_STAGES_PER_STATE_STEP)
            state_step(c, ops[k * len(batch):(k + 1) * len(batch)])
        if following is not None:
            ops = ready if ready is not None else _advance(following)

    mean_mat = jnp.where(same_head, 1.0 / HEAD_DIM, 0.0).astype(_MXU)
    ones = jnp.where(same_head, 1.0, 0.0).astype(_MXU)
    for bi in batch:
        y = y_ref[bi]
        d = y - _head_sum(y, mean_mat)
        var = _head_sum(d * d, mean_mat)
        yn = d * lax.rsqrt(var + RW_GN_EPS) * gng_ref[...] + gnb_ref[...]
        bonus = _head_sum(r_ref[bi] * kp_ref[bi] * rk_ref[...], ones) * v_ref[bi]
        o_ref[bi] = ((yn + bonus) * g_ref[bi]).astype(o_ref.dtype)


def _rwkv7(r, kp, v, kk, b, lw, g, r_k, gn_g, gn_b):
    bsz, s, _ = v.shape
    n = RW_STEP_CHUNKS * RW_CHUNK
    blk = pl.BlockSpec((bsz, n, RW_WIDTH), lambda i: (0, i, 0))
    vec = _const_spec((1, RW_WIDTH))
    return pl.pallas_call(
        _rw_core_body,
        grid=(s // n,),
        in_specs=[blk] * 7 + [vec] * 3,
        out_specs=blk,
        out_shape=jax.ShapeDtypeStruct((bsz, s, RW_WIDTH), _MXU),
        scratch_shapes=[pltpu.VMEM((bsz, RW_WIDTH, RW_WIDTH), _F32),
                        pltpu.VMEM((bsz, n, RW_WIDTH), _F32)],
        compiler_params=_params("arbitrary"),
        name="rw_core",
    )(r, kp, v, kk, b, lw, g, gn_g, gn_b, r_k)


_SW_PAIR_ORDER = tuple(h for g in range(SW_HEADS // SW_KV_HEADS)
                       for h in (g, g + SW_HEADS // SW_KV_HEADS))


def _pad_rows(w, rows):
    return jnp.pad(w, ((0, rows - w.shape[0]), (0, 0)))


def _pair_heads(w, axis):
    head = lambda h: lax.slice_in_dim(w, h * HEAD_DIM, (h + 1) * HEAD_DIM, axis=axis)
    return jnp.concatenate([head(h) for h in _SW_PAIR_ORDER], axis=axis)


def _mix_in_layout(w_in, mu):
    pad = lambda t, width: jnp.pad(t, [(0, 0)] * (t.ndim - 1) + [(0, width - t.shape[-1])])

    def rw_groups(t):
        c = 3 * RW_WIDTH
        return [t[..., :c], pad(t[..., c:c + DECAY_LORA], LANES),
                pad(t[..., c + DECAY_LORA:c + DECAY_LORA + AAA_LORA], LANES),
                pad(t[..., c + DECAY_LORA + AAA_LORA:], 2 * LANES)]

    rw_end = SB_IN_WIDTH + 3 * RW_WIDTH + DECAY_LORA + AAA_LORA + GATE_LORA
    w = jnp.concatenate(
        [w_in[..., :SB_IN_WIDTH]] + rw_groups(w_in[..., SB_IN_WIDTH:rw_end])
        + [_pair_heads(w_in[..., rw_end:rw_end + SW_WIDTH], 2), w_in[..., rw_end + SW_WIDTH:]],
        axis=-1).astype(_MXU)
    return w, jnp.concatenate(rw_groups(mu[:, None, :]), axis=-1)


def kernel(x, ffn1_norm, ffn1_w_in, ffn1_w_out, mix_norm, mix_w_in, mix_w_out, rw_mu, rw_w0,
           rw_w_up, rw_a0, rw_a_up, rw_g_up, rw_k_k, rw_k_a, rw_r_k, rw_gn_g, rw_gn_b, sw_sinks,
           ffn2_norm, ffn2_w_in, ffn2_w_out, final_norm):
    bsz, s, d = x.shape
    xf = x.reshape(bsz * s, d)
    row = lambda t: t[None, :]
    final_g = row(final_norm)
    w1_in, w1_out = ffn1_w_in.astype(_MXU), ffn1_w_out.astype(_MXU)
    w2_in, w2_out = ffn2_w_in.astype(_MXU), ffn2_w_out.astype(_MXU)
    w_mix_in, mu_all = _mix_in_layout(mix_w_in, rw_mu)
    mix_rows = SB_WIDTH + RW_WIDTH
    w_mix_out = jnp.concatenate(
        [mix_w_out[:, :mix_rows], _pair_heads(mix_w_out[:, mix_rows:], 1)], axis=1).astype(_MXU)
    for l in range(DEPTH):
        xf = _ffn(xf, row(ffn1_norm[l]), w1_in, w1_out, final_g, l)
        sb, kt, sw, *rw_parts = _inproj(
            xf, row(mix_norm[l]), w_mix_in, mu_all[l], row(rw_w0[l]), _pad_rows(rw_w_up[l], LANES),
            row(rw_a0[l]), _pad_rows(rw_a_up[l], LANES), _pad_rows(rw_g_up[l], 2 * LANES),
            row(rw_k_k[l]), row(rw_k_a[l]), s, l)
        sb_out = _sb_attention(sb.reshape(bsz, s, SB_IN_WIDTH),
                               kt.reshape(bsz, s // SB_TILE, SB_WIDTH, SB_TILE))
        rw_out = _rwkv7(
            *(t.reshape(bsz, s, RW_WIDTH) for t in rw_parts),
            rw_r_k[l].reshape(1, RW_WIDTH), row(rw_gn_g[l]), row(rw_gn_b[l]))
        xf = _ffn_mix(xf, row(ffn2_norm[l]), w2_in, w2_out, final_g, l == DEPTH - 1, l,
                      sb_out.reshape(bsz * s, SB_WIDTH), rw_out.reshape(bsz * s, RW_WIDTH),
                      sw, w_mix_out, sw_sinks[l], s)
    return xf.reshape(bsz, s, d)
```

```python
import functools

import jax
import jax.numpy as jnp
from jax import lax
from jax.experimental import pallas as pl
from jax.experimental.pallas import tpu as pltpu

D_MODEL = 1024
DEPTH = 4
HEAD_DIM = 64
SB_HEADS = 4
SB_WIDTH = SB_HEADS * HEAD_DIM
RW_HEADS = 4
RW_WIDTH = RW_HEADS * HEAD_DIM
DECAY_LORA = 64
AAA_LORA = 64
GATE_LORA = 160
RW_GN_EPS = 64e-5
SW_HEADS = 8
SW_KV_HEADS = 2
SW_WIDTH = SW_HEADS * HEAD_DIM
SW_KV_WIDTH = SW_KV_HEADS * HEAD_DIM
WINDOW = 128
D_FF = 2816
NORM_EPS = 1e-6

LANES = 128
MXU_DIM = 256
RW_PAD_WIDTH = 3 * RW_WIDTH + 2 * LANES + 2 * LANES
SB_IN_WIDTH = 3 * SB_WIDTH
SW_IN_WIDTH = SW_WIDTH + 2 * SW_KV_WIDTH
IN_PAD_WIDTH = SB_IN_WIDTH + RW_PAD_WIDTH + SW_IN_WIDTH

FFN_ROWS = 1024
PROJ_ROWS = 1024
PROJ_SUB = 256
FF_TILE = MXU_DIM
SB_TILE = 256
SW_TILE = WINDOW
MIX_ROWS = 512
RW_CHUNK = 64
RW_STEP_CHUNKS = 8
RW_GROUP = 4
RW_STAGES_PER_STATE_STEP = 4

VMEM_LIMIT_BYTES = 56 * 1024 * 1024

_MXU = jnp.bfloat16
_F32 = jnp.float32
_HI = lax.Precision.HIGHEST


def _dot(a, b, precision=None):
    return jnp.dot(a, b, preferred_element_type=_F32, precision=precision)


def _dot_nt(a, b, precision=None):
    return lax.dot_general(a, b, (((1,), (1,)), ((), ())),
                           preferred_element_type=_F32, precision=precision)


def _sigmoid(x):
    return 1.0 / (1.0 + jnp.exp(-x))


def _softplus(x):
    return jnp.maximum(x, 0.0) + jnp.log(1.0 + jnp.exp(-jnp.abs(x)))


def _params(*sem):
    return pltpu.CompilerParams(dimension_semantics=sem, vmem_limit_bytes=VMEM_LIMIT_BYTES)


def _const_spec(shape):
    nd = len(shape)
    return pl.BlockSpec(shape, lambda *_: (0,) * nd, pipeline_mode=pl.Buffered(1))


def _layer_spec(shape, layer, row_block=0):
    return pl.BlockSpec((None,) + shape, lambda *_: (layer, row_block, 0),
                        pipeline_mode=pl.Buffered(1))


def _ffn_math(x, g_ref, win_ref, wout_ref, fg_ref, act_ref, final, between=()):
    between = list(between)
    ms = jnp.mean(x * x, axis=-1, keepdims=True)
    hn = (x * lax.rsqrt(ms + NORM_EPS) * g_ref[...]).astype(_MXU)
    for c in range(D_FF // FF_TILE):
        lo, hi = c * FF_TILE, (c + 1) * FF_TILE
        gate = _dot(hn, win_ref[:, lo:hi])
        up = _dot(hn, win_ref[:, D_FF + lo:D_FF + hi])
        act_ref[:, lo:hi] = (gate * _sigmoid(gate) * up).astype(_MXU)
        if between:
            between.pop(0)()
    for thunk in between:
        thunk()
    y = x + 0.5 * _dot(act_ref[...], wout_ref[...])
    if final:
        ms = jnp.mean(y * y, axis=-1, keepdims=True)
        y = y * lax.rsqrt(ms + NORM_EPS) * fg_ref[...]
    return y


def _ffn_body(x_ref, g_ref, win_ref, wout_ref, fg_ref, o_ref, act_ref):
    o_ref[...] = _ffn_math(x_ref[...], g_ref, win_ref, wout_ref, fg_ref, act_ref, False)


def _ffn(x, g, w_in, w_out, final_g, layer):
    m = x.shape[0]
    row = lambda i: (i, 0)
    return pl.pallas_call(
        _ffn_body,
        grid=(m // FFN_ROWS,),
        in_specs=[
            pl.BlockSpec((FFN_ROWS, D_MODEL), row),
            _const_spec((1, D_MODEL)),
            _layer_spec((D_MODEL, 2 * D_FF), layer),
            _layer_spec((D_FF, D_MODEL), layer),
            _const_spec((1, D_MODEL)),
        ],
        out_specs=pl.BlockSpec((FFN_ROWS, D_MODEL), row),
        out_shape=jax.ShapeDtypeStruct((m, D_MODEL), _F32),
        scratch_shapes=[pltpu.VMEM((FFN_ROWS, D_FF), _MXU)],
        compiler_params=_params("parallel"),
        name="ffn",
    )(x, g, w_in, w_out, final_g)


def _ffn_mix_body(sinks_ref, x_ref, sb_ref, rw_ref, swq_ref, swkv_ref, swkvp_ref,
                  wsb_ref, wrw_ref, wsw_ref, g_ref, win_ref, wout_ref, fg_ref,
                  o_ref, act_ref, swo_ref, *, final, seq_tiles, last_tile):
    s = pl.program_id(0)
    t = SW_TILE
    n_sub = swq_ref.shape[0] // t
    tile = jnp.minimum(s, last_tile)
    consts = _sw_consts()

    def attend(slot, j):
        rows = slice(j * t, (j + 1) * t)
        before = swkvp_ref[...] if j == 0 else swkv_ref[(j - 1) * t:j * t]
        kv = jnp.concatenate([before, swkv_ref[rows]], axis=0)
        has_prev = (tile * n_sub + j) % seq_tiles != 0
        swo_ref[slot, rows] = _sw_tile(swq_ref[rows], kv[:, :SW_KV_WIDTH], kv[:, SW_KV_WIDTH:],
                                       has_prev, sinks_ref, consts)

    @pl.when(s == 0)
    def _():
        for j in range(n_sub):
            attend(0, j)

    @pl.when(s > 0)
    def _():
        x = (x_ref[...] + _dot(sb_ref[...], wsb_ref[...]) + _dot(rw_ref[...], wrw_ref[...])
             + _dot(swo_ref[(s - 1) % 2], wsw_ref[...]))
        o_ref[...] = _ffn_math(
            x, g_ref, win_ref, wout_ref, fg_ref, act_ref, final,
            between=[functools.partial(attend, s % 2, j) for j in range(n_sub)])


def _ffn_mix(x, g, w_in, w_out, final_g, final, layer, sb, rw, sw, w_mix, sinks, seq_len):
    m = x.shape[0]
    r = MIX_ROWS
    n = m // r
    cur = lambda s: (jnp.maximum(s - 1, 0), 0)
    ahead = lambda s: jnp.minimum(s, n - 1)
    kv_block = SW_WIDTH // (2 * SW_KV_WIDTH)
    return pl.pallas_call(
        functools.partial(_ffn_mix_body, final=final, seq_tiles=seq_len // SW_TILE, last_tile=n - 1),
        grid=(n + 1,),
        in_specs=[
            pl.BlockSpec(memory_space=pltpu.SMEM),
            pl.BlockSpec((r, D_MODEL), cur),
            pl.BlockSpec((r, SB_WIDTH), cur),
            pl.BlockSpec((r, RW_WIDTH), cur),
            pl.BlockSpec((r, SW_WIDTH), lambda s: (ahead(s), 0)),
            pl.BlockSpec((r, 2 * SW_KV_WIDTH), lambda s: (ahead(s), kv_block)),
            pl.BlockSpec((SW_TILE, 2 * SW_KV_WIDTH),
                         lambda s: (jnp.maximum(ahead(s) * (r // SW_TILE) - 1, 0), kv_block)),
            _layer_spec((SB_WIDTH, D_MODEL), layer, 0),
            _layer_spec((RW_WIDTH, D_MODEL), layer, SB_WIDTH // RW_WIDTH),
            _layer_spec((SW_WIDTH, D_MODEL), layer, (SB_WIDTH + RW_WIDTH) // SW_WIDTH),
            _const_spec((1, D_MODEL)),
            _layer_spec((D_MODEL, 2 * D_FF), layer),
            _layer_spec((D_FF, D_MODEL), layer),
            _const_spec((1, D_MODEL)),
        ],
        out_specs=pl.BlockSpec((r, D_MODEL), cur),
        out_shape=jax.ShapeDtypeStruct((m, D_MODEL), _F32),
        scratch_shapes=[pltpu.VMEM((r, D_FF), _MXU), pltpu.VMEM((2, r, SW_WIDTH), _MXU)],
        compiler_params=_params("arbitrary"),
        name="ffn_mix",
    )(sinks, x, sb, rw, sw, sw, sw, w_mix, w_mix, w_mix, g, w_in, w_out, final_g)


def _head_block_ones():
    r = lax.broadcasted_iota(jnp.int32, (RW_WIDTH, RW_WIDTH), 0) // HEAD_DIM
    c = lax.broadcasted_iota(jnp.int32, (RW_WIDTH, RW_WIDTH), 1) // HEAD_DIM
    return r == c


def _head_sum(x, mat):
    hi = x.astype(_MXU)
    lo = (x - hi.astype(_F32)).astype(_MXU)
    return _dot(hi, mat) + _dot(lo, mat)


def _inproj_body(x_ref, g_ref, w_ref, mu_ref, w0_ref, wup_ref, a0_ref, aup_ref, gup_ref,
                 kk_ref, ka_ref, sbo_ref, sw_ref, r_o, kp_o, v_o, kkn_o, b_o, lw_o, g_o,
                 last_ref, kt_scr, v_scr, *, seq_tiles):
    i = pl.program_id(0)
    sub = PROJ_SUB
    assert sub == SB_TILE
    n_sub = x_ref.shape[0] // sub
    c = RW_WIDTH
    ones = jnp.where(_head_block_ones(), 1.0, 0.0).astype(_MXU)
    first = lax.broadcasted_iota(jnp.int32, (sub, 1), 0) == 0
    sb_consts = _sb_consts()
    block_rows = lambda jb: pl.ds(pl.multiple_of(jb * sub, sub), sub)
    blk0 = (i % seq_tiles) * n_sub

    @pl.when(i == 0)
    def _():
        last_ref[...] = jnp.zeros_like(last_ref)
        kt_scr[0] = jnp.zeros((SB_WIDTH, sub), _MXU)
        v_scr[0:sub] = jnp.zeros((sub, SB_WIDTH), _MXU)

    before = jnp.maximum(blk0 - 1, 0)
    kv_before = (kt_scr[before], v_scr[block_rows(before), :])

    def project(j):
        rows = slice(j * sub, (j + 1) * sub)
        x = x_ref[rows]
        ms = jnp.mean(x * x, axis=-1, keepdims=True)
        hn = (x * lax.rsqrt(ms + NORM_EPS) * g_ref[...]).astype(_MXU)
        sb = _dot(hn, w_ref[:, :SB_IN_WIDTH])
        q = sb[:, :SB_WIDTH].astype(_MXU)
        ktb = sb[:, SB_WIDTH:2 * SB_WIDTH].T.astype(_MXU)
        vb = sb[:, 2 * SB_WIDTH:].astype(_MXU)
        kt_scr[blk0 + j] = ktb
        v_scr[block_rows(blk0 + j), :] = vb
        sw_ref[rows] = _dot(hn, w_ref[:, SB_IN_WIDTH + RW_PAD_WIDTH:]).astype(sw_ref.dtype)
        return _dot(hn, w_ref[:, SB_IN_WIDTH:SB_IN_WIDTH + RW_PAD_WIDTH]), (q, ktb, vb)

    def attend(qkv, kv_prev, keep):
        q, ktb, vb = qkv
        q_stack = _sb_queries(q, sb_consts)
        zero = jnp.zeros((sub, 1), _F32)
        acc, carries = _sb_block(q_stack, ktb, vb, jnp.zeros((sub, SB_WIDTH), _F32),
                                 (zero,) * SB_HEADS, sb_consts, diagonal=True)
        acc, carries = _sb_block(q_stack, *kv_prev, acc, carries, sb_consts, keep=keep)
        return q_stack, acc, carries

    def finish(j, q_stack, acc, carries):
        def live(state):
            return (state[0] >= 0) & _sb_alive(state[2])

        def step(state):
            jb, acc, carries = state
            acc, carries = _sb_block(q_stack, kt_scr[jb], v_scr[block_rows(jb), :],
                                     acc, carries, sb_consts)
            return jb - 1, acc, carries

        _, acc, _ = lax.while_loop(live, step, (blk0 + j - 2, acc, carries))
        sbo_ref[j * sub:(j + 1) * sub] = acc.astype(sbo_ref.dtype)

    def prepare(j, p, prev_last):
        rows = slice(j * sub, (j + 1) * sub)
        pprev = jnp.where(first, prev_last, pltpu.roll(p, 1, 0))
        xm = p + (pprev - p) * mu_ref[...]
        r, k, v = xm[:, :c], xm[:, c:2 * c], xm[:, 2 * c:3 * c]
        xw = xm[:, 3 * c:3 * c + LANES]
        xa = xm[:, 3 * c + LANES:3 * c + 2 * LANES]
        xg = xm[:, 3 * c + 2 * LANES:]
        log_w = -_softplus(-(w0_ref[...] + _rw_mm(jnp.tanh(xw), wup_ref[...]))) - 0.5
        a = _sigmoid(a0_ref[...] + _rw_mm(xa, aup_ref[...]))
        g = _rw_mm(_sigmoid(xg), gup_ref[...])
        kk = k * kk_ref[...]
        n2 = _head_sum(kk * kk, ones)
        kk = kk * lax.rsqrt(jnp.maximum(n2, 1e-24))
        r_o[rows] = r
        kp_o[rows] = k * (1.0 + (a - 1.0) * ka_ref[...])
        v_o[rows] = v
        kkn_o[rows] = kk
        b_o[rows] = kk * a
        lw_o[rows] = -jnp.exp(log_w)
        g_o[rows] = g

    prev_last = jnp.where(i % seq_tiles == 0, 0.0, last_ref[7:8, :])
    projected = [project(0)]
    attended = []
    for j in range(n_sub):
        if j + 1 < n_sub:
            projected.append(project(j + 1))
        p, qkv = projected[j]
        prepare(j, p, prev_last)
        prev_last = p[sub - 1:sub, :]
        kv_prev, keep = (kv_before, blk0 > 0) if j == 0 else (projected[j - 1][1][1:], None)
        attended.append(attend(qkv, kv_prev, keep))
    last_ref[...] = projected[-1][0][sub - 8:, :]
    for j in range(n_sub):
        finish(j, *attended[j])


def _inproj(x, g, w, mu, w0, wup, a0, aup, gup, k_k, k_a, seq_len, layer):
    m = x.shape[0]
    n = PROJ_ROWS
    row = lambda i: (i, 0)
    vec = _const_spec((1, RW_WIDTH))
    rw_out = pl.BlockSpec((n, RW_WIDTH), row)
    return pl.pallas_call(
        functools.partial(_inproj_body, seq_tiles=seq_len // n),
        grid=(m // n,),
        in_specs=[
            pl.BlockSpec((n, D_MODEL), row),
            _const_spec((1, D_MODEL)),
            _layer_spec((D_MODEL, IN_PAD_WIDTH), layer),
            _const_spec((1, RW_PAD_WIDTH)),
            vec, _const_spec((LANES, RW_WIDTH)),
            vec, _const_spec((LANES, RW_WIDTH)),
            _const_spec((2 * LANES, RW_WIDTH)),
            vec, vec,
        ],
        out_specs=[pl.BlockSpec((n, SB_WIDTH), row), pl.BlockSpec((n, SW_IN_WIDTH), row)]
        + [rw_out] * 7,
        out_shape=[jax.ShapeDtypeStruct((m, SB_WIDTH), _MXU),
                   jax.ShapeDtypeStruct((m, SW_IN_WIDTH), _MXU)]
        + [jax.ShapeDtypeStruct((m, RW_WIDTH), _F32)] * 7,
        scratch_shapes=[pltpu.VMEM((8, RW_PAD_WIDTH), _F32),
                        pltpu.VMEM((seq_len // SB_TILE, SB_WIDTH, SB_TILE), _MXU),
                        pltpu.VMEM((seq_len, SB_WIDTH), _MXU)],
        compiler_params=_params("arbitrary"),
        name="inproj",
    )(x, g, w, mu, w0, wup, a0, aup, gup, k_k, k_a)


SB_SKIP_LOG = -88.0


def _sb_consts():
    t = SB_TILE
    lane_head = lax.broadcasted_iota(jnp.int32, (1, SB_WIDTH), 1) // HEAD_DIM
    row = lax.broadcasted_iota(jnp.int32, (t, t), 0)
    col = lax.broadcasted_iota(jnp.int32, (t, t), 1)
    later = jnp.where(row > col, 1.0, 0.0).astype(_MXU)
    causal = col < row
    return lane_head, later, causal


def _sb_queries(q, consts):
    lane_head = consts[0]
    q = q * (HEAD_DIM ** -0.5)
    return jnp.concatenate(
        [jnp.where(lane_head == h, q, jnp.zeros_like(q)) for h in range(SB_HEADS)], axis=0)


def _sb_block(q_stack, ktb, vb, acc, carries, consts, diagonal=False, keep=None):
    t = SB_TILE
    lane_head, later, causal = consts
    heads = range(SB_HEADS)
    head_rows = [slice(h * t, (h + 1) * t) for h in heads]
    zz = _dot(q_stack, ktb)
    zs = [zz[rows] for rows in head_rows]
    loms = [-_softplus(z) for z in zs]
    if diagonal:
        loms = [jnp.where(causal, lom, 0.0) for lom in loms]
    tails = _dot(jnp.concatenate([lom.astype(_MXU) for lom in loms], axis=0), later)
    weights, new_carries = [], []
    for h in heads:
        tail = tails[head_rows[h]] + carries[h]
        a = jnp.exp(zs[h] + loms[h] + tail)
        if diagonal:
            a = jnp.where(causal, a, 0.0)
        if keep is not None:
            a = jnp.where(keep, a, 0.0)
        weights.append(a.astype(_MXU))
        carry = carries[h] + jnp.sum(loms[h], axis=1, keepdims=True)
        new_carries.append(carry if keep is None else jnp.where(keep, carry, carries[h]))
    v_heads = jnp.concatenate(
        [jnp.where(lane_head == h, vb, jnp.zeros_like(vb)) for h in heads], axis=0)
    acc = acc + _dot(jnp.concatenate(weights, axis=1), v_heads)
    return acc, tuple(new_carries)


def _sb_alive(carries):
    return jnp.max(functools.reduce(jnp.maximum, carries)) > SB_SKIP_LOG


def _sw_consts():
    t = SW_TILE
    assert t == WINDOW
    r_i = lax.broadcasted_iota(jnp.int32, (t, t), 0)
    c_i = lax.broadcasted_iota(jnp.int32, (t, t), 1)
    own = c_i <= r_i
    distf = jnp.where(own, r_i - c_i, r_i - c_i + t).astype(_F32)
    lane_kv = lax.broadcasted_iota(jnp.int32, (1, LANES), 1) // HEAD_DIM
    return own, distf, lane_kv


def _sw_tile(q, kj, vj, has_prev, sinks_ref, consts):
    t = SW_TILE
    own, distf, lane_kv = consts
    valid = own | has_prev
    grp = SW_HEADS // SW_KV_HEADS
    pairs = [(g, kv) for g in range(grp) for kv in range(SW_KV_HEADS)]
    head_rows = [slice(n * t, (n + 1) * t) for n in range(len(pairs))]
    q = q * (HEAD_DIM ** -0.5)
    q_stack = []
    for g, kv in pairs:
        qp = q[:, g * LANES:(g + 1) * LANES]
        q_stack.append(jnp.where(lane_kv == kv, qp, jnp.zeros_like(qp)))
    zz = _dot_nt(jnp.concatenate(q_stack, axis=0), kj)
    probs, denoms = [], []
    for n, (g, kv) in enumerate(pairs):
        head = kv * grp + g
        slope = 2.0 ** (-8.0 * (head + 1.0) / SW_HEADS)
        zh = zz[head_rows[n]]
        z = jnp.where(own, zh[:, t:], zh[:, :t]) - slope * distf
        z = jnp.where(valid, z, -1e30)
        sink = sinks_ref[head]
        m = jnp.maximum(jnp.max(z, axis=1, keepdims=True), sink)
        p = jnp.exp(z - m)
        denoms.append(jnp.sum(p, axis=1, keepdims=True) + jnp.exp(sink - m))
        p = p.astype(_MXU)
        zero = jnp.zeros_like(p)
        probs.append(jnp.concatenate([jnp.where(own, zero, p), jnp.where(own, p, zero)], axis=1))
    oo = _dot(jnp.concatenate(probs, axis=0), vj)
    out = []
    for g in range(grp):
        first, second = (pairs.index((g, kv)) for kv in range(SW_KV_HEADS))
        out.append(jnp.where(lane_kv == 0, oo[head_rows[first]] / denoms[first],
                             oo[head_rows[second]] / denoms[second]))
    return jnp.concatenate(out, axis=1).astype(_MXU)


def _rw_mm(a, b):
    return _dot(a.astype(_MXU), b.astype(_MXU))


def _rw_mm_nt(a, b):
    return _dot_nt(a.astype(_MXU), b.astype(_MXU))


def _head_stack(x, lane_head):
    return jnp.concatenate(
        [jnp.where(lane_head == h, x, 0.0) for h in range(RW_HEADS)], axis=0)


def _rw_chunk_ops(r, kp, v, kk, b, lw):
    cl = RW_CHUNK
    chunks = range(len(r))
    each = lambda f, *xs: [f(*a) for a in zip(*xs)]

    lane_head = lax.broadcasted_iota(jnp.int32, (1, RW_WIDTH), 1) // HEAD_DIM
    tr = lax.broadcasted_iota(jnp.int32, (cl, cl), 0)
    tc = lax.broadcasted_iota(jnp.int32, (cl, cl), 1)
    lower = jnp.where(tr >= tc, 1.0, 0.0)
    cum = each(lambda x: _dot(lower, x, _HI), lw)
    yield
    cum_end = each(lambda x: x[cl - 1:cl, :], cum)
    kk_t = each(lambda x, c, l: x * jnp.exp(c - l), kk, cum, lw)
    r_t = each(lambda x, c: x * jnp.exp(c), r, cum)
    inv = each(lambda c: jnp.exp(-c), cum)
    k_h = each(jnp.multiply, kp, inv)
    b_h = each(jnp.multiply, b, inv)
    yield
    to_end = each(lambda e, c: jnp.exp(e - c), cum_end, cum)
    gt = each(lambda k, bb, e: jnp.concatenate([k * e, -(bb * e)], axis=0).T.astype(_MXU),
              kp, b, to_end)
    pct = each(lambda e: jnp.broadcast_to(jnp.exp(e), (2 * cl, RW_WIDTH)).T, cum_end)
    yield

    stack = lambda x: _head_stack(x, lane_head)
    khs, bhs, vs = each(stack, k_h), each(stack, b_h), each(stack, v)
    t_i = lax.broadcasted_iota(jnp.int32, (cl, RW_WIDTH), 0)
    s_i = lax.broadcasted_iota(jnp.int32, (cl, RW_WIDTH), 1) % cl
    strict, incl = t_i > s_i, t_i >= s_i
    lhs = each(lambda x, y: jnp.concatenate([x, y], axis=0), kk_t, r_t)
    pk = each(_rw_mm_nt, lhs, khs)
    yield
    pb = each(_rw_mm_nt, lhs, bhs)
    yield
    a_b = each(lambda p: jnp.where(strict, p[:cl], 0.0), pb)
    m_b = each(lambda p: jnp.where(incl, p[cl:], 0.0).astype(_MXU), pb)
    akm = each(lambda p: jnp.concatenate(
        [jnp.where(strict, p[:cl], 0.0), jnp.where(incl, p[cl:], 0.0)], axis=0), pk)
    kv = each(_rw_mm, akm, vs)
    yield

    same_head = _head_block_ones()

    def block_diag(x_cat):
        return jnp.where(same_head, jnp.concatenate([x_cat] * RW_HEADS, axis=0), 0.0)

    eye = jnp.where(t_i == s_i, 1.0, 0.0)
    pair = ((t_i ^ s_i) == 1) & ((t_i & 1) != 0)
    t_cat = each(lambda a: eye - jnp.where(pair, a, 0.0), a_b)
    m = 2
    while m < cl:
        below = (((t_i ^ s_i) & -(2 * m)) == 0) & ((t_i & m) != 0) & ((s_i & m) == 0)
        x = each(lambda t, a: _rw_mm(t, block_diag(jnp.where(below, a, 0.0))), t_cat, a_b)
        yield
        t_cat = each(lambda t, xx: t - _rw_mm(xx, block_diag(t)), t_cat, x)
        yield
        m *= 2
    w = each(lambda t, k, p: _rw_mm(t, jnp.concatenate([stack(k), stack(p[:cl])], axis=1)),
             t_cat, kk_t, kv)
    return [(jnp.concatenate([w[c][:, :RW_WIDTH], r_t[c]], axis=0).astype(_MXU),
             w[c][:, RW_WIDTH:], kv[c][cl:], m_b[c], gt[c], pct[c]) for c in chunks]


def _advance(staged, stages=None):
    try:
        while stages is None or stages > 0:
            next(staged)
            stages = None if stages is None else stages - 1
    except StopIteration as done:
        return done.value
    return None


def _rw_core_body(r_ref, kp_ref, v_ref, kk_ref, b_ref, lw_ref, g_ref, gng_ref, gnb_ref, rk_ref,
                  o_ref, h_ref, y_ref):
    cl = RW_CHUNK

    @pl.when(pl.program_id(0) == 0)
    def _():
        h_ref[...] = jnp.zeros_like(h_ref)

    lane_head = lax.broadcasted_iota(jnp.int32, (1, RW_WIDTH), 1) // HEAD_DIM
    same_head = _head_block_ones()
    batch = range(h_ref.shape[0])
    rows = lambda c: slice(c * cl, (c + 1) * cl)
    groups = [range(g, g + RW_GROUP) for g in range(0, RW_STEP_CHUNKS, RW_GROUP)]

    def staged_ops(group):
        load = lambda ref: [ref[bi, rows(c)] for c in group for bi in batch]
        return _rw_chunk_ops(*(load(x) for x in (r_ref, kp_ref, v_ref, kk_ref, b_ref, lw_ref)))

    def state_step(c, ops):
        h = [h_ref[bi] for bi in batch]
        uy = [_dot(ops[bi][0], h[bi].astype(_MXU)) for bi in batch]
        u = [uy[bi][:cl] + ops[bi][1] for bi in batch]
        vu = [jnp.concatenate([v_ref[bi, rows(c)], u[bi]], axis=0) for bi in batch]
        upd = [_dot(ops[bi][4], vu[bi].astype(_MXU)) for bi in batch]
        for bi in batch:
            pct = ops[bi][5]
            h_ref[bi] = (jnp.concatenate([pct, pct], axis=1) * h[bi]
                         + jnp.where(same_head, upd[bi], 0.0))
        for bi in batch:
            y_ref[bi, rows(c)] = (uy[bi][cl:] + ops[bi][2]
                                  - _rw_mm(ops[bi][3], _head_stack(u[bi], lane_head)))

    ops = _advance(staged_ops(groups[0]))
    for gi, group in enumerate(groups):
        following = staged_ops(groups[gi + 1]) if gi + 1 < len(groups) else None
        ready = None
        for k, c in enumerate(group):
            if following is not None and ready is None:
                ready = _advance(following, RW_STAGES_PER_STATE_STEP)
            state_step(c, ops[k * len(batch):(k + 1) * len(batch)])
        if following is not None:
            ops = ready if ready is not None else _advance(following)

    mean_mat = jnp.where(same_head, 1.0 / HEAD_DIM, 0.0).astype(_MXU)
    ones = jnp.where(same_head, 1.0, 0.0).astype(_MXU)
    for bi in batch:
        y = y_ref[bi]
        d = y - _head_sum(y, mean_mat)
        var = _head_sum(d * d, mean_mat)
        yn = d * lax.rsqrt(var + RW_GN_EPS) * gng_ref[...] + gnb_ref[...]
        bonus = _head_sum(r_ref[bi] * kp_ref[bi] * rk_ref[...], ones) * v_ref[bi]
        o_ref[bi] = ((yn + bonus) * g_ref[bi]).astype(o_ref.dtype)


def _rwkv7(r, kp, v, kk, b, lw, g, r_k, gn_g, gn_b):
    bsz, s, _ = v.shape
    n = RW_STEP_CHUNKS * RW_CHUNK
    blk = pl.BlockSpec((bsz, n, RW_WIDTH), lambda i: (0, i, 0))
    vec = _const_spec((1, RW_WIDTH))
    return pl.pallas_call(
        _rw_core_body,
        grid=(s // n,),
        in_specs=[blk] * 7 + [vec] * 3,
        out_specs=blk,
        out_shape=jax.ShapeDtypeStruct((bsz, s, RW_WIDTH), _MXU),
        scratch_shapes=[pltpu.VMEM((bsz, RW_WIDTH, RW_WIDTH), _F32),
                        pltpu.VMEM((bsz, n, RW_WIDTH), _F32)],
        compiler_params=_params("arbitrary"),
        name="rw_core",
    )(r, kp, v, kk, b, lw, g, gn_g, gn_b, r_k)


_SW_PAIR_ORDER = tuple(h for g in range(SW_HEADS // SW_KV_HEADS)
                       for h in (g, g + SW_HEADS // SW_KV_HEADS))


def _pad_rows(w, rows):
    return jnp.pad(w, ((0, rows - w.shape[0]), (0, 0)))


def _pair_heads(w, axis):
    head = lambda h: lax.slice_in_dim(w, h * HEAD_DIM, (h + 1) * HEAD_DIM, axis=axis)
    return jnp.concatenate([head(h) for h in _SW_PAIR_ORDER], axis=axis)


def _mix_in_layout(w_in, mu):
    pad = lambda t, width: jnp.pad(t, [(0, 0)] * (t.ndim - 1) + [(0, width - t.shape[-1])])

    def rw_groups(t):
        c = 3 * RW_WIDTH
        return [t[..., :c], pad(t[..., c:c + DECAY_LORA], LANES),
                pad(t[..., c + DECAY_LORA:c + DECAY_LORA + AAA_LORA], LANES),
                pad(t[..., c + DECAY_LORA + AAA_LORA:], 2 * LANES)]

    rw_end = SB_IN_WIDTH + 3 * RW_WIDTH + DECAY_LORA + AAA_LORA + GATE_LORA
    w = jnp.concatenate(
        [w_in[..., :SB_IN_WIDTH]] + rw_groups(w_in[..., SB_IN_WIDTH:rw_end])
        + [_pair_heads(w_in[..., rw_end:rw_end + SW_WIDTH], 2), w_in[..., rw_end + SW_WIDTH:]],
        axis=-1).astype(_MXU)
    return w, jnp.concatenate(rw_groups(mu[:, None, :]), axis=-1)


def kernel(x, ffn1_norm, ffn1_w_in, ffn1_w_out, mix_norm, mix_w_in, mix_w_out, rw_mu, rw_w0,
           rw_w_up, rw_a0, rw_a_up, rw_g_up, rw_k_k, rw_k_a, rw_r_k, rw_gn_g, rw_gn_b, sw_sinks,
           ffn2_norm, ffn2_w_in, ffn2_w_out, final_norm):
    bsz, s, d = x.shape
    xf = x.reshape(bsz * s, d)
    row = lambda t: t[None, :]
    final_g = row(final_norm)
    w1_in, w1_out = ffn1_w_in.astype(_MXU), ffn1_w_out.astype(_MXU)
    w2_in, w2_out = ffn2_w_in.astype(_MXU), ffn2_w_out.astype(_MXU)
    w_mix_in, mu_all = _mix_in_layout(mix_w_in, rw_mu)
    mix_rows = SB_WIDTH + RW_WIDTH
    w_mix_out = jnp.concatenate(
        [mix_w_out[:, :mix_rows], _pair_heads(mix_w_out[:, mix_rows:], 1)], axis=1).astype(_MXU)
    for l in range(DEPTH):
        xf = _ffn(xf, row(ffn1_norm[l]), w1_in, w1_out, final_g, l)
        sb_out, sw, *rw_parts = _inproj(
            xf, row(mix_norm[l]), w_mix_in, mu_all[l], row(rw_w0[l]), _pad_rows(rw_w_up[l], LANES),
            row(rw_a0[l]), _pad_rows(rw_a_up[l], LANES), _pad_rows(rw_g_up[l], 2 * LANES),
            row(rw_k_k[l]), row(rw_k_a[l]), s, l)
        rw_out = _rwkv7(
            *(t.reshape(bsz, s, RW_WIDTH) for t in rw_parts),
            rw_r_k[l].reshape(1, RW_WIDTH), row(rw_gn_g[l]), row(rw_gn_b[l]))
        xf = _ffn_mix(xf, row(ffn2_norm[l]), w2_in, w2_out, final_g, l == DEPTH - 1, l,
                      sb_out, rw_out.reshape(bsz * s, RW_WIDTH), sw, w_mix_out, sw_sinks[l], s)
    return xf.reshape(bsz, s, d)
```

```python
import functools

import jax
import jax.numpy as jnp
from jax import lax
from jax.experimental import pallas as pl
from jax.experimental.pallas import tpu as pltpu

D_MODEL = 1024
DEPTH = 4
HEAD_DIM = 64
SB_HEADS = 4
SB_WIDTH = SB_HEADS * HEAD_DIM
RW_HEADS = 4
RW_WIDTH = RW_HEADS * HEAD_DIM
DECAY_LORA = 64
AAA_LORA = 64
GATE_LORA = 160
RW_GN_EPS = 64e-5
SW_HEADS = 8
SW_KV_HEADS = 2
SW_WIDTH = SW_HEADS * HEAD_DIM
SW_KV_WIDTH = SW_KV_HEADS * HEAD_DIM
WINDOW = 128
D_FF = 2816
NORM_EPS = 1e-6

LANES = 128
MXU_DIM = 256
RW_PAD_WIDTH = 3 * RW_WIDTH + 2 * LANES + 2 * LANES
SB_IN_WIDTH = 3 * SB_WIDTH
SW_IN_WIDTH = SW_WIDTH + 2 * SW_KV_WIDTH
IN_PAD_WIDTH = SB_IN_WIDTH + RW_PAD_WIDTH + SW_IN_WIDTH

FFN_ROWS = 512
PROJ_ROWS = 1024
PROJ_SUB = 256
FF_TILE = MXU_DIM
SB_TILE = 256
SW_TILE = WINDOW
MIX_ROWS = 512
RW_CHUNK = 64
RW_STEP_CHUNKS = 8
RW_GROUP = 4
RW_STAGES_PER_STATE_STEP = 4

VMEM_LIMIT_BYTES = 56 * 1024 * 1024

_MXU = jnp.bfloat16
_F32 = jnp.float32
_HI = lax.Precision.HIGHEST


def _dot(a, b, precision=None):
    return jnp.dot(a, b, preferred_element_type=_F32, precision=precision)


def _dot_nt(a, b, precision=None):
    return lax.dot_general(a, b, (((1,), (1,)), ((), ())),
                           preferred_element_type=_F32, precision=precision)


def _sigmoid(x):
    return 1.0 / (1.0 + jnp.exp(-x))


def _softplus(x):
    return jnp.maximum(x, 0.0) + jnp.log(1.0 + jnp.exp(-jnp.abs(x)))


def _params(*sem):
    return pltpu.CompilerParams(dimension_semantics=sem, vmem_limit_bytes=VMEM_LIMIT_BYTES)


def _const_spec(shape):
    nd = len(shape)
    return pl.BlockSpec(shape, lambda *_: (0,) * nd, pipeline_mode=pl.Buffered(1))


def _layer_spec(shape, layer, row_block=0):
    return pl.BlockSpec((None,) + shape, lambda *_: (layer, row_block, 0),
                        pipeline_mode=pl.Buffered(1))


def _ffn_math(x, g_ref, win_ref, wout_ref, fg_ref, act_ref, final, between=()):
    between = list(between)
    ms = jnp.mean(x * x, axis=-1, keepdims=True)
    hn = (x * lax.rsqrt(ms + NORM_EPS) * g_ref[...]).astype(_MXU)
    for c in range(D_FF // FF_TILE):
        lo, hi = c * FF_TILE, (c + 1) * FF_TILE
        gate = _dot(hn, win_ref[:, lo:hi])
        up = _dot(hn, win_ref[:, D_FF + lo:D_FF + hi])
        act_ref[:, lo:hi] = (gate * _sigmoid(gate) * up).astype(_MXU)
        if between:
            between.pop(0)()
    for thunk in between:
        thunk()
    y = x + 0.5 * _dot(act_ref[...], wout_ref[...])
    if final:
        ms = jnp.mean(y * y, axis=-1, keepdims=True)
        y = y * lax.rsqrt(ms + NORM_EPS) * fg_ref[...]
    return y


def _cast_slab(src_ref, dst_ref):
    dst_ref[...] = src_ref[...].astype(dst_ref.dtype)


def _ffn_body(*refs, n_cast):
    x_ref, g_ref, win_ref, wout_ref, fg_ref = refs[:5]
    src = refs[5:5 + n_cast]
    o_ref = refs[5 + n_cast]
    dst = refs[6 + n_cast:6 + 2 * n_cast]
    act_ref = refs[-1]
    casts = [functools.partial(_cast_slab, s_ref, d_ref) for s_ref, d_ref in zip(src, dst)]
    o_ref[...] = _ffn_math(x_ref[...], g_ref, win_ref, wout_ref, fg_ref, act_ref, False, between=casts)


def _ffn(x, g, w_in, w_out, final_g, cast_next=()):
    m = x.shape[0]
    steps = m // FFN_ROWS
    row = lambda i: (i, 0)
    cast_in, cast_out, cast_shapes = [], [], []
    for w, layer in cast_next:
        rows, cols = w.shape[1:]
        per = next(k for k in (1, 2, 4, 8) if rows % (steps // k) == 0 and (rows * k // steps) % 16 == 0)
        slab = rows * per // steps
        cast_in.append(pl.BlockSpec((None, slab, cols), lambda i, per=per, layer=layer: (layer, i // per, 0)))
        cast_out.append(pl.BlockSpec((slab, cols), lambda i, per=per: (i // per, 0)))
        cast_shapes.append(jax.ShapeDtypeStruct((rows, cols), _MXU))
    out = pl.pallas_call(
        functools.partial(_ffn_body, n_cast=len(cast_next)),
        grid=(steps,),
        in_specs=[
            pl.BlockSpec((FFN_ROWS, D_MODEL), row),
            _const_spec((1, D_MODEL)),
            _const_spec((D_MODEL, 2 * D_FF)),
            _const_spec((D_FF, D_MODEL)),
            _const_spec((1, D_MODEL)),
        ] + cast_in,
        out_specs=[pl.BlockSpec((FFN_ROWS, D_MODEL), row)] + cast_out,
        out_shape=[jax.ShapeDtypeStruct((m, D_MODEL), _F32)] + cast_shapes,
        scratch_shapes=[pltpu.VMEM((FFN_ROWS, D_FF), _MXU)],
        compiler_params=_params("arbitrary"),
        name="ffn",
    )(x, g, w_in, w_out, final_g, *(w for w, _ in cast_next))
    return out


def _ffn_mix_body(sinks_ref, x_ref, sb_ref, rw_ref, swq_ref, swkv_ref, swkvp_ref,
                  wsb_ref, wrw_ref, wsw_ref, g_ref, win_ref, wout_ref, fg_ref,
                  o_ref, act_ref, swo_ref, *, final, seq_tiles, last_tile):
    s = pl.program_id(0)
    t = SW_TILE
    n_sub = swq_ref.shape[0] // t
    tile = jnp.minimum(s, last_tile)
    consts = _sw_consts()

    def attend(slot, j):
        rows = slice(j * t, (j + 1) * t)
        before = swkvp_ref[...] if j == 0 else swkv_ref[(j - 1) * t:j * t]
        kv = jnp.concatenate([before, swkv_ref[rows]], axis=0)
        has_prev = (tile * n_sub + j) % seq_tiles != 0
        swo_ref[slot, rows] = _sw_tile(swq_ref[rows], kv[:, :SW_KV_WIDTH], kv[:, SW_KV_WIDTH:],
                                       has_prev, sinks_ref, consts)

    @pl.when(s == 0)
    def _():
        for j in range(n_sub):
            attend(0, j)

    @pl.when(s > 0)
    def _():
        x = (x_ref[...] + _dot(sb_ref[...], wsb_ref[...]) + _dot(rw_ref[...], wrw_ref[...])
             + _dot(swo_ref[(s - 1) % 2], wsw_ref[...]))
        o_ref[...] = _ffn_math(
            x, g_ref, win_ref, wout_ref, fg_ref, act_ref, final,
            between=[functools.partial(attend, s % 2, j) for j in range(n_sub)])


def _ffn_mix(x, g, w_in, w_out, final_g, final, layer, sb, rw, sw, w_mix, sinks, seq_len):
    m = x.shape[0]
    r = MIX_ROWS
    n = m // r
    cur = lambda s: (jnp.maximum(s - 1, 0), 0)
    ahead = lambda s: jnp.minimum(s, n - 1)
    kv_block = SW_WIDTH // (2 * SW_KV_WIDTH)
    return pl.pallas_call(
        functools.partial(_ffn_mix_body, final=final, seq_tiles=seq_len // SW_TILE, last_tile=n - 1),
        grid=(n + 1,),
        in_specs=[
            pl.BlockSpec(memory_space=pltpu.SMEM),
            pl.BlockSpec((r, D_MODEL), cur),
            pl.BlockSpec((r, SB_WIDTH), cur),
            pl.BlockSpec((r, RW_WIDTH), cur),
            pl.BlockSpec((r, SW_WIDTH), lambda s: (ahead(s), 0)),
            pl.BlockSpec((r, 2 * SW_KV_WIDTH), lambda s: (ahead(s), kv_block)),
            pl.BlockSpec((SW_TILE, 2 * SW_KV_WIDTH),
                         lambda s: (jnp.maximum(ahead(s) * (r // SW_TILE) - 1, 0), kv_block)),
            _layer_spec((SB_WIDTH, D_MODEL), layer, 0),
            _layer_spec((RW_WIDTH, D_MODEL), layer, SB_WIDTH // RW_WIDTH),
            _layer_spec((SW_WIDTH, D_MODEL), layer, (SB_WIDTH + RW_WIDTH) // SW_WIDTH),
            _const_spec((1, D_MODEL)),
            _const_spec((D_MODEL, 2 * D_FF)),
            _const_spec((D_FF, D_MODEL)),
            _const_spec((1, D_MODEL)),
        ],
        out_specs=pl.BlockSpec((r, D_MODEL), cur),
        out_shape=jax.ShapeDtypeStruct((m, D_MODEL), _F32),
        scratch_shapes=[pltpu.VMEM((r, D_FF), _MXU), pltpu.VMEM((2, r, SW_WIDTH), _MXU)],
        compiler_params=_params("arbitrary"),
        name="ffn_mix",
    )(sinks, x, sb, rw, sw, sw, sw, w_mix, w_mix, w_mix, g, w_in, w_out, final_g)


def _head_block_ones():
    r = lax.broadcasted_iota(jnp.int32, (RW_WIDTH, RW_WIDTH), 0) // HEAD_DIM
    c = lax.broadcasted_iota(jnp.int32, (RW_WIDTH, RW_WIDTH), 1) // HEAD_DIM
    return r == c


def _head_sum(x, mat):
    hi = x.astype(_MXU)
    lo = (x - hi.astype(_F32)).astype(_MXU)
    return _dot(hi, mat) + _dot(lo, mat)


def _inproj_body(x_ref, g_ref, w_ref, mu_ref, w0_ref, wup_ref, a0_ref, aup_ref, gup_ref,
                 kk_ref, ka_ref, sbo_ref, sw_ref, r_o, kp_o, v_o, kkn_o, b_o, lw_o, g_o,
                 last_ref, kt_scr, v_scr, *, seq_tiles):
    i = pl.program_id(0)
    sub = PROJ_SUB
    assert sub == SB_TILE
    n_sub = x_ref.shape[0] // sub
    c = RW_WIDTH
    ones = jnp.where(_head_block_ones(), 1.0, 0.0).astype(_MXU)
    first = lax.broadcasted_iota(jnp.int32, (sub, 1), 0) == 0
    sb_consts = _sb_consts()
    block_rows = lambda jb: pl.ds(pl.multiple_of(jb * sub, sub), sub)
    blk0 = (i % seq_tiles) * n_sub

    @pl.when(i == 0)
    def _():
        last_ref[...] = jnp.zeros_like(last_ref)
        kt_scr[0] = jnp.zeros((SB_WIDTH, sub), _MXU)
        v_scr[0:sub] = jnp.zeros((sub, SB_WIDTH), _MXU)

    before = jnp.maximum(blk0 - 1, 0)
    kv_before = (kt_scr[before], v_scr[block_rows(before), :])

    def project(j):
        rows = slice(j * sub, (j + 1) * sub)
        x = x_ref[rows]
        ms = jnp.mean(x * x, axis=-1, keepdims=True)
        hn = (x * lax.rsqrt(ms + NORM_EPS) * g_ref[...]).astype(_MXU)
        sb = _dot(hn, w_ref[:, :SB_IN_WIDTH])
        q = sb[:, :SB_WIDTH].astype(_MXU)
        ktb = sb[:, SB_WIDTH:2 * SB_WIDTH].T.astype(_MXU)
        vb = sb[:, 2 * SB_WIDTH:].astype(_MXU)
        kt_scr[blk0 + j] = ktb
        v_scr[block_rows(blk0 + j), :] = vb
        yield
        sw_ref[rows] = _dot(hn, w_ref[:, SB_IN_WIDTH + RW_PAD_WIDTH:]).astype(sw_ref.dtype)
        yield
        return _dot(hn, w_ref[:, SB_IN_WIDTH:SB_IN_WIDTH + RW_PAD_WIDTH]), (q, ktb, vb)

    def attend(qkv, kv_prev, keep):
        q, ktb, vb = qkv
        q_stack = _sb_queries(q, sb_consts)
        zero = jnp.zeros((sub, 1), _F32)
        acc, carries = _sb_block(q_stack, ktb, vb, jnp.zeros((sub, SB_WIDTH), _F32),
                                 (zero,) * SB_HEADS, sb_consts, diagonal=True)
        yield
        acc, carries = _sb_block(q_stack, *kv_prev, acc, carries, sb_consts, keep=keep)
        return q_stack, acc, carries

    def finish(j, q_stack, acc, carries):
        def live(state):
            return (state[0] >= 0) & _sb_alive(state[2])

        def step(state):
            jb, acc, carries = state
            acc, carries = _sb_block(q_stack, kt_scr[jb], v_scr[block_rows(jb), :],
                                     acc, carries, sb_consts)
            return jb - 1, acc, carries

        _, acc, _ = lax.while_loop(live, step, (blk0 + j - 2, acc, carries))
        sbo_ref[j * sub:(j + 1) * sub] = acc.astype(sbo_ref.dtype)

    def prepare(j, p, prev_last):
        rows = slice(j * sub, (j + 1) * sub)
        pprev = jnp.where(first, prev_last, pltpu.roll(p, 1, 0))
        xm = p + (pprev - p) * mu_ref[...]
        r, k, v = xm[:, :c], xm[:, c:2 * c], xm[:, 2 * c:3 * c]
        xw = xm[:, 3 * c:3 * c + LANES]
        xa = xm[:, 3 * c + LANES:3 * c + 2 * LANES]
        xg = xm[:, 3 * c + 2 * LANES:]
        log_w = -_softplus(-(w0_ref[...] + _rw_mm(jnp.tanh(xw), wup_ref[...]))) - 0.5
        a = _sigmoid(a0_ref[...] + _rw_mm(xa, aup_ref[...]))
        g = _rw_mm(_sigmoid(xg), gup_ref[...])
        kk = k * kk_ref[...]
        n2 = _head_sum(kk * kk, ones)
        kk = kk * lax.rsqrt(jnp.maximum(n2, 1e-24))
        r_o[rows] = r
        kp_o[rows] = k * (1.0 + (a - 1.0) * ka_ref[...])
        v_o[rows] = v
        kkn_o[rows] = kk
        b_o[rows] = kk * a
        lw_o[rows] = -jnp.exp(log_w)
        g_o[rows] = g

    prev_last = jnp.where(i % seq_tiles == 0, 0.0, last_ref[7:8, :])
    projected = [_advance(project(0))]
    attended = []
    for j in range(n_sub):
        p, qkv = projected[j]
        kv_prev, keep = (kv_before, blk0 > 0) if j == 0 else (projected[j - 1][1][1:], None)
        ahead = project(j + 1) if j + 1 < n_sub else iter(())
        attention = attend(qkv, kv_prev, keep)
        _advance(ahead, 1)
        _advance(attention, 1)
        _advance(ahead, 1)
        prepare(j, p, prev_last)
        prev_last = p[sub - 1:sub, :]
        if j + 1 < n_sub:
            projected.append(_advance(ahead))
        attended.append(_advance(attention))
    last_ref[...] = projected[-1][0][sub - 8:, :]
    for j in range(n_sub):
        finish(j, *attended[j])


def _inproj(x, g, w, mu, w0, wup, a0, aup, gup, k_k, k_a, seq_len, layer):
    m = x.shape[0]
    n = PROJ_ROWS
    row = lambda i: (i, 0)
    vec = _const_spec((1, RW_WIDTH))
    rw_out = pl.BlockSpec((n, RW_WIDTH), row)
    return pl.pallas_call(
        functools.partial(_inproj_body, seq_tiles=seq_len // n),
        grid=(m // n,),
        in_specs=[
            pl.BlockSpec((n, D_MODEL), row),
            _const_spec((1, D_MODEL)),
            _layer_spec((D_MODEL, IN_PAD_WIDTH), layer),
            _const_spec((1, RW_PAD_WIDTH)),
            vec, _const_spec((LANES, RW_WIDTH)),
            vec, _const_spec((LANES, RW_WIDTH)),
            _const_spec((2 * LANES, RW_WIDTH)),
            vec, vec,
        ],
        out_specs=[pl.BlockSpec((n, SB_WIDTH), row), pl.BlockSpec((n, SW_IN_WIDTH), row)]
        + [rw_out] * 7,
        out_shape=[jax.ShapeDtypeStruct((m, SB_WIDTH), _MXU),
                   jax.ShapeDtypeStruct((m, SW_IN_WIDTH), _MXU)]
        + [jax.ShapeDtypeStruct((m, RW_WIDTH), _F32)] * 7,
        scratch_shapes=[pltpu.VMEM((8, RW_PAD_WIDTH), _F32),
                        pltpu.VMEM((seq_len // SB_TILE, SB_WIDTH, SB_TILE), _MXU),
                        pltpu.VMEM((seq_len, SB_WIDTH), _MXU)],
        compiler_params=_params("arbitrary"),
        name="inproj",
    )(x, g, w, mu, w0, wup, a0, aup, gup, k_k, k_a)


SB_SKIP_LOG = -88.0


def _sb_consts():
    t = SB_TILE
    lane_head = lax.broadcasted_iota(jnp.int32, (1, SB_WIDTH), 1) // HEAD_DIM
    row = lax.broadcasted_iota(jnp.int32, (t, t), 0)
    col = lax.broadcasted_iota(jnp.int32, (t, t), 1)
    later = jnp.where(row > col, 1.0, 0.0).astype(_MXU)
    causal = col < row
    return lane_head, later, causal


def _sb_queries(q, consts):
    lane_head = consts[0]
    q = q * (HEAD_DIM ** -0.5)
    return jnp.concatenate(
        [jnp.where(lane_head == h, q, jnp.zeros_like(q)) for h in range(SB_HEADS)], axis=0)


def _sb_block(q_stack, ktb, vb, acc, carries, consts, diagonal=False, keep=None):
    t = SB_TILE
    lane_head, later, causal = consts
    heads = range(SB_HEADS)
    head_rows = [slice(h * t, (h + 1) * t) for h in heads]
    zz = _dot(q_stack, ktb)
    zs = [zz[rows] for rows in head_rows]
    loms = [-_softplus(z) for z in zs]
    if diagonal:
        loms = [jnp.where(causal, lom, 0.0) for lom in loms]
    tails = _dot(jnp.concatenate([lom.astype(_MXU) for lom in loms], axis=0), later)
    weights, new_carries = [], []
    for h in heads:
        tail = tails[head_rows[h]] + carries[h]
        a = jnp.exp(zs[h] + loms[h] + tail)
        if diagonal:
            a = jnp.where(causal, a, 0.0)
        if keep is not None:
            a = jnp.where(keep, a, 0.0)
        weights.append(a.astype(_MXU))
        carry = carries[h] + jnp.sum(loms[h], axis=1, keepdims=True)
        new_carries.append(carry if keep is None else jnp.where(keep, carry, carries[h]))
    v_heads = jnp.concatenate(
        [jnp.where(lane_head == h, vb, jnp.zeros_like(vb)) for h in heads], axis=0)
    acc = acc + _dot(jnp.concatenate(weights, axis=1), v_heads)
    return acc, tuple(new_carries)


def _sb_alive(carries):
    return jnp.max(functools.reduce(jnp.maximum, carries)) > SB_SKIP_LOG


def _sw_consts():
    t = SW_TILE
    assert t == WINDOW
    r_i = lax.broadcasted_iota(jnp.int32, (t, t), 0)
    c_i = lax.broadcasted_iota(jnp.int32, (t, t), 1)
    own = c_i <= r_i
    distf = jnp.where(own, r_i - c_i, r_i - c_i + t).astype(_F32)
    lane_kv = lax.broadcasted_iota(jnp.int32, (1, LANES), 1) // HEAD_DIM
    return own, distf, lane_kv


def _sw_tile(q, kj, vj, has_prev, sinks_ref, consts):
    t = SW_TILE
    own, distf, lane_kv = consts
    valid = own | has_prev
    grp = SW_HEADS // SW_KV_HEADS
    pairs = [(g, kv) for g in range(grp) for kv in range(SW_KV_HEADS)]
    head_rows = [slice(n * t, (n + 1) * t) for n in range(len(pairs))]
    q = q * (HEAD_DIM ** -0.5)
    q_stack = []
    for g, kv in pairs:
        qp = q[:, g * LANES:(g + 1) * LANES]
        q_stack.append(jnp.where(lane_kv == kv, qp, jnp.zeros_like(qp)))
    zz = _dot_nt(jnp.concatenate(q_stack, axis=0), kj)
    probs, denoms = [], []
    for n, (g, kv) in enumerate(pairs):
        head = kv * grp + g
        slope = 2.0 ** (-8.0 * (head + 1.0) / SW_HEADS)
        zh = zz[head_rows[n]]
        z = jnp.where(own, zh[:, t:], zh[:, :t]) - slope * distf
        z = jnp.where(valid, z, -1e30)
        sink = sinks_ref[head]
        m = jnp.maximum(jnp.max(z, axis=1, keepdims=True), sink)
        p = jnp.exp(z - m)
        denoms.append(jnp.sum(p, axis=1, keepdims=True) + jnp.exp(sink - m))
        p = p.astype(_MXU)
        zero = jnp.zeros_like(p)
        probs.append(jnp.concatenate([jnp.where(own, zero, p), jnp.where(own, p, zero)], axis=1))
    oo = _dot(jnp.concatenate(probs, axis=0), vj)
    out = []
    for g in range(grp):
        first, second = (pairs.index((g, kv)) for kv in range(SW_KV_HEADS))
        out.append(jnp.where(lane_kv == 0, oo[head_rows[first]] / denoms[first],
                             oo[head_rows[second]] / denoms[second]))
    return jnp.concatenate(out, axis=1).astype(_MXU)


def _rw_mm(a, b):
    return _dot(a.astype(_MXU), b.astype(_MXU))


def _rw_mm_nt(a, b):
    return _dot_nt(a.astype(_MXU), b.astype(_MXU))


def _head_stack(x, lane_head):
    return jnp.concatenate(
        [jnp.where(lane_head == h, x, 0.0) for h in range(RW_HEADS)], axis=0)


def _rw_chunk_ops(r, kp, v, kk, b, lw):
    cl = RW_CHUNK
    chunks = range(len(r))
    each = lambda f, *xs: [f(*a) for a in zip(*xs)]

    lane_head = lax.broadcasted_iota(jnp.int32, (1, RW_WIDTH), 1) // HEAD_DIM
    tr = lax.broadcasted_iota(jnp.int32, (cl, cl), 0)
    tc = lax.broadcasted_iota(jnp.int32, (cl, cl), 1)
    lower = jnp.where(tr >= tc, 1.0, 0.0)
    cum = each(lambda x: _dot(lower, x, _HI), lw)
    yield
    cum_end = each(lambda x: x[cl - 1:cl, :], cum)
    kk_t = each(lambda x, c, l: x * jnp.exp(c - l), kk, cum, lw)
    r_t = each(lambda x, c: x * jnp.exp(c), r, cum)
    inv = each(lambda c: jnp.exp(-c), cum)
    k_h = each(jnp.multiply, kp, inv)
    b_h = each(jnp.multiply, b, inv)
    yield
    to_end = each(lambda e, c: jnp.exp(e - c), cum_end, cum)
    gt = each(lambda k, bb, e: jnp.concatenate([k * e, -(bb * e)], axis=0).T.astype(_MXU),
              kp, b, to_end)
    pct = each(lambda e: jnp.broadcast_to(jnp.exp(e), (2 * cl, RW_WIDTH)).T, cum_end)
    yield

    stack = lambda x: _head_stack(x, lane_head)
    khs, bhs, vs = each(stack, k_h), each(stack, b_h), each(stack, v)
    t_i = lax.broadcasted_iota(jnp.int32, (cl, RW_WIDTH), 0)
    s_i = lax.broadcasted_iota(jnp.int32, (cl, RW_WIDTH), 1) % cl
    strict, incl = t_i > s_i, t_i >= s_i
    lhs = each(lambda x, y: jnp.concatenate([x, y], axis=0), kk_t, r_t)
    pk = each(_rw_mm_nt, lhs, khs)
    yield
    pb = each(_rw_mm_nt, lhs, bhs)
    yield
    a_b = each(lambda p: jnp.where(strict, p[:cl], 0.0), pb)
    m_b = each(lambda p: jnp.where(incl, p[cl:], 0.0).astype(_MXU), pb)
    akm = each(lambda p: jnp.concatenate(
        [jnp.where(strict, p[:cl], 0.0), jnp.where(incl, p[cl:], 0.0)], axis=0), pk)
    kv = each(_rw_mm, akm, vs)
    yield

    same_head = _head_block_ones()

    def block_diag(x_cat):
        return jnp.where(same_head, jnp.concatenate([x_cat] * RW_HEADS, axis=0), 0.0)

    eye = jnp.where(t_i == s_i, 1.0, 0.0)
    pair = ((t_i ^ s_i) == 1) & ((t_i & 1) != 0)
    t_cat = each(lambda a: eye - jnp.where(pair, a, 0.0), a_b)
    m = 2
    while m < cl:
        below = (((t_i ^ s_i) & -(2 * m)) == 0) & ((t_i & m) != 0) & ((s_i & m) == 0)
        x = each(lambda t, a: _rw_mm(t, block_diag(jnp.where(below, a, 0.0))), t_cat, a_b)
        yield
        t_cat = each(lambda t, xx: t - _rw_mm(xx, block_diag(t)), t_cat, x)
        yield
        m *= 2
    w = each(lambda t, k, p: _rw_mm(t, jnp.concatenate([stack(k), stack(p[:cl])], axis=1)),
             t_cat, kk_t, kv)
    return [(jnp.concatenate([w[c][:, :RW_WIDTH], r_t[c]], axis=0).astype(_MXU),
             w[c][:, RW_WIDTH:], kv[c][cl:], m_b[c], gt[c], pct[c]) for c in chunks]


def _advance(staged, stages=None):
    try:
        while stages is None or stages > 0:
            next(staged)
            stages = None if stages is None else stages - 1
    except StopIteration as done:
        return done.value
    return None


def _rw_core_body(r_ref, kp_ref, v_ref, kk_ref, b_ref, lw_ref, g_ref, gng_ref, gnb_ref, rk_ref,
                  o_ref, h_ref, y_ref):
    cl = RW_CHUNK

    @pl.when(pl.program_id(0) == 0)
    def _():
        h_ref[...] = jnp.zeros_like(h_ref)

    lane_head = lax.broadcasted_iota(jnp.int32, (1, RW_WIDTH), 1) // HEAD_DIM
    same_head = _head_block_ones()
    batch = range(h_ref.shape[0])
    rows = lambda c: slice(c * cl, (c + 1) * cl)
    groups = [range(g, g + RW_GROUP) for g in range(0, RW_STEP_CHUNKS, RW_GROUP)]

    def staged_ops(group):
        load = lambda ref: [ref[bi, rows(c)] for c in group for bi in batch]
        return _rw_chunk_ops(*(load(x) for x in (r_ref, kp_ref, v_ref, kk_ref, b_ref, lw_ref)))

    def state_step(c, ops):
        h = [h_ref[bi] for bi in batch]
        uy = [_dot(ops[bi][0], h[bi].astype(_MXU)) for bi in batch]
        u = [uy[bi][:cl] + ops[bi][1] for bi in batch]
        vu = [jnp.concatenate([v_ref[bi, rows(c)], u[bi]], axis=0) for bi in batch]
        upd = [_dot(ops[bi][4], vu[bi].astype(_MXU)) for bi in batch]
        for bi in batch:
            pct = ops[bi][5]
            h_ref[bi] = (jnp.concatenate([pct, pct], axis=1) * h[bi]
                         + jnp.where(same_head, upd[bi], 0.0))
        for bi in batch:
            y_ref[bi, rows(c)] = (uy[bi][cl:] + ops[bi][2]
                                  - _rw_mm(ops[bi][3], _head_stack(u[bi], lane_head)))

    ops = _advance(staged_ops(groups[0]))
    for gi, group in enumerate(groups):
        following = staged_ops(groups[gi + 1]) if gi + 1 < len(groups) else None
        ready = None
        for k, c in enumerate(group):
            if following is not None and ready is None:
                ready = _advance(following, RW_STAGES_PER_STATE_STEP)
            state_step(c, ops[k * len(batch):(k + 1) * len(batch)])
        if following is not None:
            ops = ready if ready is not None else _advance(following)

    mean_mat = jnp.where(same_head, 1.0 / HEAD_DIM, 0.0).astype(_MXU)
    ones = jnp.where(same_head, 1.0, 0.0).astype(_MXU)
    for bi in batch:
        y = y_ref[bi]
        d = y - _head_sum(y, mean_mat)
        var = _head_sum(d * d, mean_mat)
        yn = d * lax.rsqrt(var + RW_GN_EPS) * gng_ref[...] + gnb_ref[...]
        bonus = _head_sum(r_ref[bi] * kp_ref[bi] * rk_ref[...], ones) * v_ref[bi]
        o_ref[bi] = ((yn + bonus) * g_ref[bi]).astype(o_ref.dtype)


def _rwkv7(r, kp, v, kk, b, lw, g, r_k, gn_g, gn_b):
    bsz, s, _ = v.shape
    n = RW_STEP_CHUNKS * RW_CHUNK
    blk = pl.BlockSpec((bsz, n, RW_WIDTH), lambda i: (0, i, 0))
    vec = _const_spec((1, RW_WIDTH))
    return pl.pallas_call(
        _rw_core_body,
        grid=(s // n,),
        in_specs=[blk] * 7 + [vec] * 3,
        out_specs=blk,
        out_shape=jax.ShapeDtypeStruct((bsz, s, RW_WIDTH), _MXU),
        scratch_shapes=[pltpu.VMEM((bsz, RW_WIDTH, RW_WIDTH), _F32),
                        pltpu.VMEM((bsz, n, RW_WIDTH), _F32)],
        compiler_params=_params("arbitrary"),
        name="rw_core",
    )(r, kp, v, kk, b, lw, g, gn_g, gn_b, r_k)


_SW_PAIR_ORDER = tuple(h for g in range(SW_HEADS // SW_KV_HEADS)
                       for h in (g, g + SW_HEADS // SW_KV_HEADS))


def _pad_rows(w, rows):
    return jnp.pad(w, ((0, rows - w.shape[0]), (0, 0)))


def _pair_heads(w, axis):
    head = lambda h: lax.slice_in_dim(w, h * HEAD_DIM, (h + 1) * HEAD_DIM, axis=axis)
    return jnp.concatenate([head(h) for h in _SW_PAIR_ORDER], axis=axis)


def _mix_in_layout(w_in, mu):
    pad = lambda t, width: jnp.pad(t, [(0, 0)] * (t.ndim - 1) + [(0, width - t.shape[-1])])

    def rw_groups(t):
        c = 3 * RW_WIDTH
        return [t[..., :c], pad(t[..., c:c + DECAY_LORA], LANES),
                pad(t[..., c + DECAY_LORA:c + DECAY_LORA + AAA_LORA], LANES),
                pad(t[..., c + DECAY_LORA + AAA_LORA:], 2 * LANES)]

    rw_end = SB_IN_WIDTH + 3 * RW_WIDTH + DECAY_LORA + AAA_LORA + GATE_LORA
    w = jnp.concatenate(
        [w_in[..., :SB_IN_WIDTH]] + rw_groups(w_in[..., SB_IN_WIDTH:rw_end])
        + [_pair_heads(w_in[..., rw_end:rw_end + SW_WIDTH], 2), w_in[..., rw_end + SW_WIDTH:]],
        axis=-1).astype(_MXU)
    return w, jnp.concatenate(rw_groups(mu[:, None, :]), axis=-1)


def kernel(x, ffn1_norm, ffn1_w_in, ffn1_w_out, mix_norm, mix_w_in, mix_w_out, rw_mu, rw_w0,
           rw_w_up, rw_a0, rw_a_up, rw_g_up, rw_k_k, rw_k_a, rw_r_k, rw_gn_g, rw_gn_b, sw_sinks,
           ffn2_norm, ffn2_w_in, ffn2_w_out, final_norm):
    bsz, s, d = x.shape
    xf = x.reshape(bsz * s, d)
    row = lambda t: t[None, :]
    final_g = row(final_norm)
    w_mix_in, mu_all = _mix_in_layout(mix_w_in, rw_mu)
    mix_rows = SB_WIDTH + RW_WIDTH
    w_mix_out = jnp.concatenate(
        [mix_w_out[:, :mix_rows], _pair_heads(mix_w_out[:, mix_rows:], 1)], axis=1).astype(_MXU)
    ffn_f32 = (ffn1_w_in, ffn1_w_out, ffn2_w_in, ffn2_w_out)
    w1_in, w1_out, w2_in, w2_out = (w[0].astype(_MXU) for w in ffn_f32)
    for l in range(DEPTH):
        cast_next = [(w, l + 1) for w in ffn_f32] if l + 1 < DEPTH else []
        xf, *ffn_next = _ffn(xf, row(ffn1_norm[l]), w1_in, w1_out, final_g, cast_next)
        sb_out, sw, *rw_parts = _inproj(
            xf, row(mix_norm[l]), w_mix_in, mu_all[l], row(rw_w0[l]), _pad_rows(rw_w_up[l], LANES),
            row(rw_a0[l]), _pad_rows(rw_a_up[l], LANES), _pad_rows(rw_g_up[l], 2 * LANES),
            row(rw_k_k[l]), row(rw_k_a[l]), s, l)
        rw_out = _rwkv7(
            *(t.reshape(bsz, s, RW_WIDTH) for t in rw_parts),
            rw_r_k[l].reshape(1, RW_WIDTH), row(rw_gn_g[l]), row(rw_gn_b[l]))
        xf = _ffn_mix(xf, row(ffn2_norm[l]), w2_in, w2_out, final_g, l == DEPTH - 1, l,
                      sb_out, rw_out.reshape(bsz * s, RW_WIDTH), sw, w_mix_out, sw_sinks[l], s)
        if ffn_next:
            w1_in, w1_out, w2_in, w2_out = ffn_next
    return xf.reshape(bsz, s, d)
```

```python
import functools

import jax
import jax.numpy as jnp
from jax import lax
from jax.experimental import pallas as pl
from jax.experimental.pallas import tpu as pltpu

D_MODEL = 1024
DEPTH = 4
HEAD_DIM = 64
SB_HEADS = 4
SB_WIDTH = SB_HEADS * HEAD_DIM
RW_HEADS = 4
RW_WIDTH = RW_HEADS * HEAD_DIM
DECAY_LORA = 64
AAA_LORA = 64
GATE_LORA = 160
RW_GN_EPS = 64e-5
SW_HEADS = 8
SW_KV_HEADS = 2
SW_WIDTH = SW_HEADS * HEAD_DIM
SW_KV_WIDTH = SW_KV_HEADS * HEAD_DIM
WINDOW = 128
D_FF = 2816
NORM_EPS = 1e-6

LANES = 128
MXU_DIM = 256
SB_IN_WIDTH = 3 * SB_WIDTH
SW_IN_WIDTH = SW_WIDTH + 2 * SW_KV_WIDTH
RW_IN_WIDTH = 3 * RW_WIDTH + DECAY_LORA + AAA_LORA + GATE_LORA
RW_SPAN = -(-RW_IN_WIDTH // LANES) * LANES

FFN_ROWS = 512
PROJ_ROWS = 1024
PROJ_SUB = 256
FF_TILE = MXU_DIM
SB_TILE = 256
SW_TILE = WINDOW
MIX_ROWS = 512
RW_CHUNK = 64
RW_STEP_CHUNKS = 8
RW_GROUP = 4
RW_STAGES_PER_STATE_STEP = 4

VMEM_LIMIT_BYTES = 56 * 1024 * 1024

_MXU = jnp.bfloat16
_F32 = jnp.float32
_HI = lax.Precision.HIGHEST


def _dot(a, b, precision=None):
    return jnp.dot(a, b, preferred_element_type=_F32, precision=precision)


def _dot_nt(a, b, precision=None):
    return lax.dot_general(a, b, (((1,), (1,)), ((), ())),
                           preferred_element_type=_F32, precision=precision)


def _sigmoid(x):
    return 1.0 / (1.0 + jnp.exp(-x))


def _softplus(x):
    return jnp.maximum(x, 0.0) + jnp.log(1.0 + jnp.exp(-jnp.abs(x)))


def _params(*sem):
    return pltpu.CompilerParams(dimension_semantics=sem, vmem_limit_bytes=VMEM_LIMIT_BYTES)


def _const_spec(shape):
    nd = len(shape)
    return pl.BlockSpec(shape, lambda *_: (0,) * nd, pipeline_mode=pl.Buffered(1))


def _layer_spec(shape, layer, row_block=0):
    return pl.BlockSpec((None,) + shape, lambda *_: (layer, row_block, 0),
                        pipeline_mode=pl.Buffered(1))


def _ffn_math(x, g_ref, win_ref, wout_ref, fg_ref, act_ref, final, between=()):
    between = list(between)
    ms = jnp.mean(x * x, axis=-1, keepdims=True)
    hn = (x * lax.rsqrt(ms + NORM_EPS) * g_ref[...]).astype(_MXU)
    for c in range(D_FF // FF_TILE):
        lo, hi = c * FF_TILE, (c + 1) * FF_TILE
        gate = _dot(hn, win_ref[:, lo:hi])
        up = _dot(hn, win_ref[:, D_FF + lo:D_FF + hi])
        act_ref[:, lo:hi] = (gate * _sigmoid(gate) * up).astype(_MXU)
        if between:
            between.pop(0)()
    for thunk in between:
        thunk()
    y = x + 0.5 * _dot(act_ref[...], wout_ref[...])
    if final:
        ms = jnp.mean(y * y, axis=-1, keepdims=True)
        y = y * lax.rsqrt(ms + NORM_EPS) * fg_ref[...]
    return y


def _cast_slab(src_ref, dst_ref):
    dst_ref[...] = src_ref[...].astype(dst_ref.dtype)


def _ffn_body(*refs, n_cast):
    x_ref, g_ref, win_ref, wout_ref, fg_ref = refs[:5]
    src = refs[5:5 + n_cast]
    o_ref = refs[5 + n_cast]
    dst = refs[6 + n_cast:6 + 2 * n_cast]
    act_ref = refs[-1]
    casts = [functools.partial(_cast_slab, s_ref, d_ref) for s_ref, d_ref in zip(src, dst)]
    o_ref[...] = _ffn_math(x_ref[...], g_ref, win_ref, wout_ref, fg_ref, act_ref, False, between=casts)


def _ffn(x, g, w_in, w_out, final_g, cast_next=()):
    m = x.shape[0]
    steps = m // FFN_ROWS
    row = lambda i: (i, 0)
    cast_in, cast_out, cast_shapes = [], [], []
    for w, layer in cast_next:
        rows, cols = w.shape[1:]
        per = next(k for k in (1, 2, 4, 8) if rows % (steps // k) == 0 and (rows * k // steps) % 16 == 0)
        slab = rows * per // steps
        cast_in.append(pl.BlockSpec((None, slab, cols), lambda i, per=per, layer=layer: (layer, i // per, 0)))
        cast_out.append(pl.BlockSpec((slab, cols), lambda i, per=per: (i // per, 0)))
        cast_shapes.append(jax.ShapeDtypeStruct((rows, cols), _MXU))
    out = pl.pallas_call(
        functools.partial(_ffn_body, n_cast=len(cast_next)),
        grid=(steps,),
        in_specs=[
            pl.BlockSpec((FFN_ROWS, D_MODEL), row),
            _const_spec((1, D_MODEL)),
            _const_spec((D_MODEL, 2 * D_FF)),
            _const_spec((D_FF, D_MODEL)),
            _const_spec((1, D_MODEL)),
        ] + cast_in,
        out_specs=[pl.BlockSpec((FFN_ROWS, D_MODEL), row)] + cast_out,
        out_shape=[jax.ShapeDtypeStruct((m, D_MODEL), _F32)] + cast_shapes,
        scratch_shapes=[pltpu.VMEM((FFN_ROWS, D_FF), _MXU)],
        compiler_params=_params("arbitrary"),
        name="ffn",
    )(x, g, w_in, w_out, final_g, *(w for w, _ in cast_next))
    return out


def _ffn_mix_body(sinks_ref, x_ref, sb_ref, rw_ref, swq_ref, swkv_ref, swkvp_ref,
                  wsb_ref, wrw_ref, wsw_ref, g_ref, win_ref, wout_ref, fg_ref,
                  o_ref, act_ref, swo_ref, *, final, seq_tiles, last_tile):
    s = pl.program_id(0)
    t = SW_TILE
    n_sub = swq_ref.shape[0] // t
    tile = jnp.minimum(s, last_tile)
    consts = _sw_consts()

    def attend(slot, j):
        rows = slice(j * t, (j + 1) * t)
        before = swkvp_ref[...] if j == 0 else swkv_ref[(j - 1) * t:j * t]
        kv = jnp.concatenate([before, swkv_ref[rows]], axis=0)
        has_prev = (tile * n_sub + j) % seq_tiles != 0
        swo_ref[slot, rows] = _sw_tile(swq_ref[rows], kv[:, :SW_KV_WIDTH], kv[:, SW_KV_WIDTH:],
                                       has_prev, sinks_ref, consts)

    @pl.when(s == 0)
    def _():
        for j in range(n_sub):
            attend(0, j)

    @pl.when(s > 0)
    def _():
        x = (x_ref[...] + _dot(sb_ref[...], wsb_ref[...]) + _dot(rw_ref[...], wrw_ref[...])
             + _dot(swo_ref[(s - 1) % 2], wsw_ref[...]))
        o_ref[...] = _ffn_math(
            x, g_ref, win_ref, wout_ref, fg_ref, act_ref, final,
            between=[functools.partial(attend, s % 2, j) for j in range(n_sub)])


def _ffn_mix(x, g, w_in, w_out, final_g, final, layer, sb, rw, sw, w_mix, sinks, seq_len):
    m = x.shape[0]
    r = MIX_ROWS
    n = m // r
    cur = lambda s: (jnp.maximum(s - 1, 0), 0)
    ahead = lambda s: jnp.minimum(s, n - 1)
    kv_block = SW_WIDTH // (2 * SW_KV_WIDTH)
    return pl.pallas_call(
        functools.partial(_ffn_mix_body, final=final, seq_tiles=seq_len // SW_TILE, last_tile=n - 1),
        grid=(n + 1,),
        in_specs=[
            pl.BlockSpec(memory_space=pltpu.SMEM),
            pl.BlockSpec((r, D_MODEL), cur),
            pl.BlockSpec((r, SB_WIDTH), cur),
            pl.BlockSpec((r, RW_WIDTH), cur),
            pl.BlockSpec((r, SW_WIDTH), lambda s: (ahead(s), 0)),
            pl.BlockSpec((r, 2 * SW_KV_WIDTH), lambda s: (ahead(s), kv_block)),
            pl.BlockSpec((SW_TILE, 2 * SW_KV_WIDTH),
                         lambda s: (jnp.maximum(ahead(s) * (r // SW_TILE) - 1, 0), kv_block)),
            _layer_spec((SB_WIDTH, D_MODEL), layer, 0),
            _layer_spec((RW_WIDTH, D_MODEL), layer, SB_WIDTH // RW_WIDTH),
            _layer_spec((SW_WIDTH, D_MODEL), layer, (SB_WIDTH + RW_WIDTH) // SW_WIDTH),
            _const_spec((1, D_MODEL)),
            _const_spec((D_MODEL, 2 * D_FF)),
            _const_spec((D_FF, D_MODEL)),
            _const_spec((1, D_MODEL)),
        ],
        out_specs=pl.BlockSpec((r, D_MODEL), cur),
        out_shape=jax.ShapeDtypeStruct((m, D_MODEL), _F32),
        scratch_shapes=[pltpu.VMEM((r, D_FF), _MXU), pltpu.VMEM((2, r, SW_WIDTH), _MXU)],
        compiler_params=_params("arbitrary"),
        name="ffn_mix",
    )(sinks, x, sb, rw, sw, sw, sw, w_mix, w_mix, w_mix, g, w_in, w_out, final_g)


def _head_block_ones():
    r = lax.broadcasted_iota(jnp.int32, (RW_WIDTH, RW_WIDTH), 0) // HEAD_DIM
    c = lax.broadcasted_iota(jnp.int32, (RW_WIDTH, RW_WIDTH), 1) // HEAD_DIM
    return r == c


def _head_sum(x, mat):
    hi = x.astype(_MXU)
    lo = (x - hi.astype(_F32)).astype(_MXU)
    return _dot(hi, mat) + _dot(lo, mat)


def _inproj_body(x_ref, g_ref, w_ref, wsw_ref, mu_ref, w0_ref, wup_ref, a0_ref, aup_ref, gup_ref,
                 kk_ref, ka_ref, sbo_ref, sw_ref, r_o, kp_o, v_o, kkn_o, b_o, lw_o, g_o,
                 last_ref, kt_scr, v_scr, *, seq_tiles):
    i = pl.program_id(0)
    sub = PROJ_SUB
    assert sub == SB_TILE
    n_sub = x_ref.shape[0] // sub
    c = RW_WIDTH
    ones = jnp.where(_head_block_ones(), 1.0, 0.0).astype(_MXU)
    first = lax.broadcasted_iota(jnp.int32, (sub, 1), 0) == 0
    sb_consts = _sb_consts()
    block_rows = lambda jb: pl.ds(pl.multiple_of(jb * sub, sub), sub)
    blk0 = (i % seq_tiles) * n_sub

    @pl.when(i == 0)
    def _():
        last_ref[...] = jnp.zeros_like(last_ref)
        kt_scr[0] = jnp.zeros((SB_WIDTH, sub), _MXU)
        v_scr[0:sub] = jnp.zeros((sub, SB_WIDTH), _MXU)

    before = jnp.maximum(blk0 - 1, 0)
    kv_before = (kt_scr[before], v_scr[block_rows(before), :])

    def project(j):
        rows = slice(j * sub, (j + 1) * sub)
        x = x_ref[rows]
        ms = jnp.mean(x * x, axis=-1, keepdims=True)
        hn = (x * lax.rsqrt(ms + NORM_EPS) * g_ref[...]).astype(_MXU)
        sb = _dot(hn, w_ref[:, :SB_IN_WIDTH])
        q = sb[:, :SB_WIDTH].astype(_MXU)
        ktb = sb[:, SB_WIDTH:2 * SB_WIDTH].T.astype(_MXU)
        vb = sb[:, 2 * SB_WIDTH:].astype(_MXU)
        kt_scr[blk0 + j] = ktb
        v_scr[block_rows(blk0 + j), :] = vb
        yield
        sw_ref[rows] = _dot(hn, wsw_ref[...]).astype(sw_ref.dtype)
        yield
        return _dot(hn, w_ref[:, SB_IN_WIDTH:]), (q, ktb, vb)

    def attend(qkv, kv_prev, keep):
        q, ktb, vb = qkv
        q_stack = _sb_queries(q, sb_consts)
        zero = jnp.zeros((sub, 1), _F32)
        acc, carries = _sb_block(q_stack, ktb, vb, jnp.zeros((sub, SB_WIDTH), _F32),
                                 (zero,) * SB_HEADS, sb_consts, diagonal=True)
        yield
        acc, carries = _sb_block(q_stack, *kv_prev, acc, carries, sb_consts, keep=keep)
        return q_stack, acc, carries

    def finish(j, q_stack, acc, carries):
        def live(state):
            return (state[0] >= 0) & _sb_alive(state[2])

        def step(state):
            jb, acc, carries = state
            acc, carries = _sb_block(q_stack, kt_scr[jb], v_scr[block_rows(jb), :],
                                     acc, carries, sb_consts)
            return jb - 1, acc, carries

        _, acc, _ = lax.while_loop(live, step, (blk0 + j - 2, acc, carries))
        sbo_ref[j * sub:(j + 1) * sub] = acc.astype(sbo_ref.dtype)

    def prepare(j, p, prev_last):
        rows = slice(j * sub, (j + 1) * sub)
        pprev = jnp.where(first, prev_last, pltpu.roll(p, 1, 0))
        xm = p + (pprev - p) * mu_ref[...]
        r, k, v = xm[:, :c], xm[:, c:2 * c], xm[:, 2 * c:3 * c]
        xwa = xm[:, 3 * c:3 * c + LANES]
        xg = xm[:, 3 * c + LANES:]
        log_w = -_softplus(-(w0_ref[...] + _rw_mm(jnp.tanh(xwa), wup_ref[...]))) - 0.5
        a = _sigmoid(a0_ref[...] + _rw_mm(xwa, aup_ref[...]))
        g = _rw_mm(_sigmoid(xg), gup_ref[...])
        kk = k * kk_ref[...]
        n2 = _head_sum(kk * kk, ones)
        kk = kk * lax.rsqrt(jnp.maximum(n2, 1e-24))
        r_o[rows] = r
        kp_o[rows] = k * (1.0 + (a - 1.0) * ka_ref[...])
        v_o[rows] = v
        kkn_o[rows] = kk
        b_o[rows] = kk * a
        lw_o[rows] = -jnp.exp(log_w)
        g_o[rows] = g

    prev_last = jnp.where(i % seq_tiles == 0, 0.0, last_ref[7:8, :])
    projected = [_advance(project(0))]
    attended = []
    for j in range(n_sub):
        p, qkv = projected[j]
        kv_prev, keep = (kv_before, blk0 > 0) if j == 0 else (projected[j - 1][1][1:], None)
        ahead = project(j + 1) if j + 1 < n_sub else iter(())
        attention = attend(qkv, kv_prev, keep)
        _advance(ahead, 1)
        _advance(attention, 1)
        _advance(ahead, 1)
        prepare(j, p, prev_last)
        prev_last = p[sub - 1:sub, :]
        if j + 1 < n_sub:
            projected.append(_advance(ahead))
        attended.append(_advance(attention))
    last_ref[...] = projected[-1][0][sub - 8:, :]
    for j in range(n_sub):
        finish(j, *attended[j])


def _inproj(x, g, w, w_sw, mu, w0, wup, a0, aup, gup, k_k, k_a, seq_len, layer):
    assert DECAY_LORA + AAA_LORA == LANES
    m = x.shape[0]
    n = PROJ_ROWS
    row = lambda i: (i, 0)
    vec = _const_spec((1, RW_WIDTH))
    rw_out = pl.BlockSpec((n, RW_WIDTH), row)
    return pl.pallas_call(
        functools.partial(_inproj_body, seq_tiles=seq_len // n),
        grid=(m // n,),
        in_specs=[
            pl.BlockSpec((n, D_MODEL), row),
            _const_spec((1, D_MODEL)),
            _layer_spec((D_MODEL, SB_IN_WIDTH + RW_SPAN), layer),
            _layer_spec((D_MODEL, SW_IN_WIDTH), layer),
            _const_spec((1, RW_SPAN)),
            vec, _const_spec((LANES, RW_WIDTH)),
            vec, _const_spec((LANES, RW_WIDTH)),
            _const_spec((RW_SPAN - 3 * RW_WIDTH - LANES, RW_WIDTH)),
            vec, vec,
        ],
        out_specs=[pl.BlockSpec((n, SB_WIDTH), row), pl.BlockSpec((n, SW_IN_WIDTH), row)]
        + [rw_out] * 7,
        out_shape=[jax.ShapeDtypeStruct((m, SB_WIDTH), _MXU),
                   jax.ShapeDtypeStruct((m, SW_IN_WIDTH), _MXU)]
        + [jax.ShapeDtypeStruct((m, RW_WIDTH), _F32)] * 7,
        scratch_shapes=[pltpu.VMEM((8, RW_SPAN), _F32),
                        pltpu.VMEM((seq_len // SB_TILE, SB_WIDTH, SB_TILE), _MXU),
                        pltpu.VMEM((seq_len, SB_WIDTH), _MXU)],
        compiler_params=_params("arbitrary"),
        name="inproj",
    )(x, g, w, w_sw, mu, w0, wup, a0, aup, gup, k_k, k_a)


SB_SKIP_LOG = -88.0


def _sb_consts():
    t = SB_TILE
    lane_head = lax.broadcasted_iota(jnp.int32, (1, SB_WIDTH), 1) // HEAD_DIM
    row = lax.broadcasted_iota(jnp.int32, (t, t), 0)
    col = lax.broadcasted_iota(jnp.int32, (t, t), 1)
    later = jnp.where(row > col, 1.0, 0.0).astype(_MXU)
    causal = col < row
    return lane_head, later, causal


def _sb_queries(q, consts):
    lane_head = consts[0]
    q = q * (HEAD_DIM ** -0.5)
    return jnp.concatenate(
        [jnp.where(lane_head == h, q, jnp.zeros_like(q)) for h in range(SB_HEADS)], axis=0)


def _sb_block(q_stack, ktb, vb, acc, carries, consts, diagonal=False, keep=None):
    t = SB_TILE
    lane_head, later, causal = consts
    heads = range(SB_HEADS)
    head_rows = [slice(h * t, (h + 1) * t) for h in heads]
    zz = _dot(q_stack, ktb)
    zs = [zz[rows] for rows in head_rows]
    loms = [-_softplus(z) for z in zs]
    if diagonal:
        loms = [jnp.where(causal, lom, 0.0) for lom in loms]
    tails = _dot(jnp.concatenate([lom.astype(_MXU) for lom in loms], axis=0), later)
    weights, new_carries = [], []
    for h in heads:
        tail = tails[head_rows[h]] + carries[h]
        a = jnp.exp(zs[h] + loms[h] + tail)
        if diagonal:
            a = jnp.where(causal, a, 0.0)
        if keep is not None:
            a = jnp.where(keep, a, 0.0)
        weights.append(a.astype(_MXU))
        carry = carries[h] + jnp.sum(loms[h], axis=1, keepdims=True)
        new_carries.append(carry if keep is None else jnp.where(keep, carry, carries[h]))
    v_heads = jnp.concatenate(
        [jnp.where(lane_head == h, vb, jnp.zeros_like(vb)) for h in heads], axis=0)
    acc = acc + _dot(jnp.concatenate(weights, axis=1), v_heads)
    return acc, tuple(new_carries)


def _sb_alive(carries):
    return jnp.max(functools.reduce(jnp.maximum, carries)) > SB_SKIP_LOG


def _sw_consts():
    t = SW_TILE
    assert t == WINDOW
    r_i = lax.broadcasted_iota(jnp.int32, (t, t), 0)
    c_i = lax.broadcasted_iota(jnp.int32, (t, t), 1)
    own = c_i <= r_i
    distf = jnp.where(own, r_i - c_i, r_i - c_i + t).astype(_F32)
    lane_kv = lax.broadcasted_iota(jnp.int32, (1, LANES), 1) // HEAD_DIM
    return own, distf, lane_kv


def _sw_tile(q, kj, vj, has_prev, sinks_ref, consts):
    t = SW_TILE
    own, distf, lane_kv = consts
    valid = own | has_prev
    grp = SW_HEADS // SW_KV_HEADS
    pairs = [(g, kv) for g in range(grp) for kv in range(SW_KV_HEADS)]
    head_rows = [slice(n * t, (n + 1) * t) for n in range(len(pairs))]
    q = q * (HEAD_DIM ** -0.5)
    q_stack = []
    for g, kv in pairs:
        qp = q[:, g * LANES:(g + 1) * LANES]
        q_stack.append(jnp.where(lane_kv == kv, qp, jnp.zeros_like(qp)))
    zz = _dot_nt(jnp.concatenate(q_stack, axis=0), kj)
    probs, denoms = [], []
    for n, (g, kv) in enumerate(pairs):
        head = kv * grp + g
        slope = 2.0 ** (-8.0 * (head + 1.0) / SW_HEADS)
        zh = zz[head_rows[n]]
        z = jnp.where(own, zh[:, t:], zh[:, :t]) - slope * distf
        z = jnp.where(valid, z, -1e30)
        sink = sinks_ref[head]
        m = jnp.maximum(jnp.max(z, axis=1, keepdims=True), sink)
        p = jnp.exp(z - m)
        denoms.append(jnp.sum(p, axis=1, keepdims=True) + jnp.exp(sink - m))
        p = p.astype(_MXU)
        zero = jnp.zeros_like(p)
        probs.append(jnp.concatenate([jnp.where(own, zero, p), jnp.where(own, p, zero)], axis=1))
    oo = _dot(jnp.concatenate(probs, axis=0), vj)
    out = []
    for g in range(grp):
        first, second = (pairs.index((g, kv)) for kv in range(SW_KV_HEADS))
        out.append(jnp.where(lane_kv == 0, oo[head_rows[first]] / denoms[first],
                             oo[head_rows[second]] / denoms[second]))
    return jnp.concatenate(out, axis=1).astype(_MXU)


def _rw_mm(a, b):
    return _dot(a.astype(_MXU), b.astype(_MXU))


def _rw_mm_nt(a, b):
    return _dot_nt(a.astype(_MXU), b.astype(_MXU))


def _head_stack(x, lane_head):
    return jnp.concatenate(
        [jnp.where(lane_head == h, x, 0.0) for h in range(RW_HEADS)], axis=0)


def _rw_chunk_ops(r, kp, v, kk, b, lw):
    cl = RW_CHUNK
    chunks = range(len(r))
    each = lambda f, *xs: [f(*a) for a in zip(*xs)]

    lane_head = lax.broadcasted_iota(jnp.int32, (1, RW_WIDTH), 1) // HEAD_DIM
    tr = lax.broadcasted_iota(jnp.int32, (cl, cl), 0)
    tc = lax.broadcasted_iota(jnp.int32, (cl, cl), 1)
    lower = jnp.where(tr >= tc, 1.0, 0.0)
    cum = each(lambda x: _dot(lower, x, _HI), lw)
    yield
    cum_end = each(lambda x: x[cl - 1:cl, :], cum)
    kk_t = each(lambda x, c, l: x * jnp.exp(c - l), kk, cum, lw)
    r_t = each(lambda x, c: x * jnp.exp(c), r, cum)
    inv = each(lambda c: jnp.exp(-c), cum)
    k_h = each(jnp.multiply, kp, inv)
    b_h = each(jnp.multiply, b, inv)
    yield
    to_end = each(lambda e, c: jnp.exp(e - c), cum_end, cum)
    gt = each(lambda k, bb, e: jnp.concatenate([k * e, -(bb * e)], axis=0).T.astype(_MXU),
              kp, b, to_end)
    pct = each(lambda e: jnp.broadcast_to(jnp.exp(e), (2 * cl, RW_WIDTH)).T, cum_end)
    yield

    stack = lambda x: _head_stack(x, lane_head)
    khs, bhs, vs = each(stack, k_h), each(stack, b_h), each(stack, v)
    t_i = lax.broadcasted_iota(jnp.int32, (cl, RW_WIDTH), 0)
    s_i = lax.broadcasted_iota(jnp.int32, (cl, RW_WIDTH), 1) % cl
    strict, incl = t_i > s_i, t_i >= s_i
    lhs = each(lambda x, y: jnp.concatenate([x, y], axis=0), kk_t, r_t)
    pk = each(_rw_mm_nt, lhs, khs)
    yield
    pb = each(_rw_mm_nt, lhs, bhs)
    yield
    a_b = each(lambda p: jnp.where(strict, p[:cl], 0.0), pb)
    m_b = each(lambda p: jnp.where(incl, p[cl:], 0.0).astype(_MXU), pb)
    akm = each(lambda p: jnp.concatenate(
        [jnp.where(strict, p[:cl], 0.0), jnp.where(incl, p[cl:], 0.0)], axis=0), pk)
    kv = each(_rw_mm, akm, vs)
    yield

    same_head = _head_block_ones()

    def block_diag(x_cat):
        return jnp.where(same_head, jnp.concatenate([x_cat] * RW_HEADS, axis=0), 0.0)

    eye = jnp.where(t_i == s_i, 1.0, 0.0)
    pair = ((t_i ^ s_i) == 1) & ((t_i & 1) != 0)
    t_cat = each(lambda a: eye - jnp.where(pair, a, 0.0), a_b)
    m = 2
    while m < cl:
        below = (((t_i ^ s_i) & -(2 * m)) == 0) & ((t_i & m) != 0) & ((s_i & m) == 0)
        x = each(lambda t, a: _rw_mm(t, block_diag(jnp.where(below, a, 0.0))), t_cat, a_b)
        yield
        t_cat = each(lambda t, xx: t - _rw_mm(xx, block_diag(t)), t_cat, x)
        yield
        m *= 2
    w = each(lambda t, k, p: _rw_mm(t, jnp.concatenate([stack(k), stack(p[:cl])], axis=1)),
             t_cat, kk_t, kv)
    return [(jnp.concatenate([w[c][:, :RW_WIDTH], r_t[c]], axis=0).astype(_MXU),
             w[c][:, RW_WIDTH:], kv[c][cl:], m_b[c], gt[c], pct[c]) for c in chunks]


def _advance(staged, stages=None):
    try:
        while stages is None or stages > 0:
            next(staged)
            stages = None if stages is None else stages - 1
    except StopIteration as done:
        return done.value
    return None


def _rw_core_body(r_ref, kp_ref, v_ref, kk_ref, b_ref, lw_ref, g_ref, gng_ref, gnb_ref, rk_ref,
                  o_ref, h_ref, y_ref):
    cl = RW_CHUNK

    @pl.when(pl.program_id(0) == 0)
    def _():
        h_ref[...] = jnp.zeros_like(h_ref)

    lane_head = lax.broadcasted_iota(jnp.int32, (1, RW_WIDTH), 1) // HEAD_DIM
    same_head = _head_block_ones()
    batch = range(h_ref.shape[0])
    rows = lambda c: slice(c * cl, (c + 1) * cl)
    groups = [range(g, g + RW_GROUP) for g in range(0, RW_STEP_CHUNKS, RW_GROUP)]

    def staged_ops(group):
        load = lambda ref: [ref[bi, rows(c)] for c in group for bi in batch]
        return _rw_chunk_ops(*(load(x) for x in (r_ref, kp_ref, v_ref, kk_ref, b_ref, lw_ref)))

    def state_step(c, ops):
        h = [h_ref[bi] for bi in batch]
        uy = [_dot(ops[bi][0], h[bi].astype(_MXU)) for bi in batch]
        u = [uy[bi][:cl] + ops[bi][1] for bi in batch]
        vu = [jnp.concatenate([v_ref[bi, rows(c)], u[bi]], axis=0) for bi in batch]
        upd = [_dot(ops[bi][4], vu[bi].astype(_MXU)) for bi in batch]
        for bi in batch:
            pct = ops[bi][5]
            h_ref[bi] = (jnp.concatenate([pct, pct], axis=1) * h[bi]
                         + jnp.where(same_head, upd[bi], 0.0))
        for bi in batch:
            y_ref[bi, rows(c)] = (uy[bi][cl:] + ops[bi][2]
                                  - _rw_mm(ops[bi][3], _head_stack(u[bi], lane_head)))

    ops = _advance(staged_ops(groups[0]))
    for gi, group in enumerate(groups):
        following = staged_ops(groups[gi + 1]) if gi + 1 < len(groups) else None
        ready = None
        for k, c in enumerate(group):
            if following is not None and ready is None:
                ready = _advance(following, RW_STAGES_PER_STATE_STEP)
            state_step(c, ops[k * len(batch):(k + 1) * len(batch)])
        if following is not None:
            ops = ready if ready is not None else _advance(following)

    mean_mat = jnp.where(same_head, 1.0 / HEAD_DIM, 0.0).astype(_MXU)
    ones = jnp.where(same_head, 1.0, 0.0).astype(_MXU)
    for bi in batch:
        y = y_ref[bi]
        d = y - _head_sum(y, mean_mat)
        var = _head_sum(d * d, mean_mat)
        yn = d * lax.rsqrt(var + RW_GN_EPS) * gng_ref[...] + gnb_ref[...]
        bonus = _head_sum(r_ref[bi] * kp_ref[bi] * rk_ref[...], ones) * v_ref[bi]
        o_ref[bi] = ((yn + bonus) * g_ref[bi]).astype(o_ref.dtype)


def _rwkv7(r, kp, v, kk, b, lw, g, r_k, gn_g, gn_b):
    bsz, s, _ = v.shape
    n = RW_STEP_CHUNKS * RW_CHUNK
    blk = pl.BlockSpec((bsz, n, RW_WIDTH), lambda i: (0, i, 0))
    vec = _const_spec((1, RW_WIDTH))
    return pl.pallas_call(
        _rw_core_body,
        grid=(s // n,),
        in_specs=[blk] * 7 + [vec] * 3,
        out_specs=blk,
        out_shape=jax.ShapeDtypeStruct((bsz, s, RW_WIDTH), _MXU),
        scratch_shapes=[pltpu.VMEM((bsz, RW_WIDTH, RW_WIDTH), _F32),
                        pltpu.VMEM((bsz, n, RW_WIDTH), _F32)],
        compiler_params=_params("arbitrary"),
        name="rw_core",
    )(r, kp, v, kk, b, lw, g, gn_g, gn_b, r_k)


_SW_PAIR_ORDER = tuple(h for g in range(SW_HEADS // SW_KV_HEADS)
                       for h in (g, g + SW_HEADS // SW_KV_HEADS))


def _pad_rows(w, before, total):
    return jnp.pad(w, ((before, total - before - w.shape[0]), (0, 0)))


def _pair_heads(w, axis):
    head = lambda h: lax.slice_in_dim(w, h * HEAD_DIM, (h + 1) * HEAD_DIM, axis=axis)
    return jnp.concatenate([head(h) for h in _SW_PAIR_ORDER], axis=axis)


def _mix_in_layout(w_in, mu):
    rw_end = SB_IN_WIDTH + RW_IN_WIDTH
    w = w_in.astype(_MXU)
    w_sw = jnp.concatenate([_pair_heads(w[..., rw_end:rw_end + SW_WIDTH], 2),
                            w[..., rw_end + SW_WIDTH:]], axis=-1)
    return w, w_sw, jnp.pad(mu, ((0, 0), (0, RW_SPAN - RW_IN_WIDTH)))[:, None, :]


def kernel(x, ffn1_norm, ffn1_w_in, ffn1_w_out, mix_norm, mix_w_in, mix_w_out, rw_mu, rw_w0,
           rw_w_up, rw_a0, rw_a_up, rw_g_up, rw_k_k, rw_k_a, rw_r_k, rw_gn_g, rw_gn_b, sw_sinks,
           ffn2_norm, ffn2_w_in, ffn2_w_out, final_norm):
    bsz, s, d = x.shape
    xf = x.reshape(bsz * s, d)
    row = lambda t: t[None, :]
    final_g = row(final_norm)
    w_mix_in, w_mix_sw, mu_all = _mix_in_layout(mix_w_in, rw_mu)
    gate_rows = RW_SPAN - 3 * RW_WIDTH - LANES
    mix_rows = SB_WIDTH + RW_WIDTH
    w_mix_out = jnp.concatenate(
        [mix_w_out[:, :mix_rows], _pair_heads(mix_w_out[:, mix_rows:], 1)], axis=1).astype(_MXU)
    ffn_f32 = (ffn1_w_in, ffn1_w_out, ffn2_w_in, ffn2_w_out)
    w1_in, w1_out, w2_in, w2_out = (w[0].astype(_MXU) for w in ffn_f32)
    for l in range(DEPTH):
        cast_next = [(w, l + 1) for w in ffn_f32] if l + 1 < DEPTH else []
        xf, *ffn_next = _ffn(xf, row(ffn1_norm[l]), w1_in, w1_out, final_g, cast_next)
        sb_out, sw, *rw_parts = _inproj(
            xf, row(mix_norm[l]), w_mix_in, w_mix_sw, mu_all[l],
            row(rw_w0[l]), _pad_rows(rw_w_up[l], 0, LANES),
            row(rw_a0[l]), _pad_rows(rw_a_up[l], DECAY_LORA, LANES),
            _pad_rows(rw_g_up[l], 0, gate_rows), row(rw_k_k[l]), row(rw_k_a[l]), s, l)
        rw_out = _rwkv7(
            *(t.reshape(bsz, s, RW_WIDTH) for t in rw_parts),
            rw_r_k[l].reshape(1, RW_WIDTH), row(rw_gn_g[l]), row(rw_gn_b[l]))
        xf = _ffn_mix(xf, row(ffn2_norm[l]), w2_in, w2_out, final_g, l == DEPTH - 1, l,
                      sb_out, rw_out.reshape(bsz * s, RW_WIDTH), sw, w_mix_out, sw_sinks[l], s)
        if ffn_next:
            w1_in, w1_out, w2_in, w2_out = ffn_next
    return xf.reshape(bsz, s, d)
```

```python
import functools

import jax
import jax.numpy as jnp
from jax import lax
from jax.experimental import pallas as pl
from jax.experimental.pallas import tpu as pltpu

D_MODEL = 1024
DEPTH = 4
HEAD_DIM = 64
SB_HEADS = 4
SB_WIDTH = SB_HEADS * HEAD_DIM
RW_HEADS = 4
RW_WIDTH = RW_HEADS * HEAD_DIM
DECAY_LORA = 64
AAA_LORA = 64
GATE_LORA = 160
RW_GN_EPS = 64e-5
SW_HEADS = 8
SW_KV_HEADS = 2
SW_WIDTH = SW_HEADS * HEAD_DIM
SW_KV_WIDTH = SW_KV_HEADS * HEAD_DIM
WINDOW = 128
D_FF = 2816
NORM_EPS = 1e-6

LANES = 128
MXU_DIM = 256
SB_IN_WIDTH = 3 * SB_WIDTH
SW_IN_WIDTH = SW_WIDTH + 2 * SW_KV_WIDTH
RW_IN_WIDTH = 3 * RW_WIDTH + DECAY_LORA + AAA_LORA + GATE_LORA
RW_SPAN = -(-RW_IN_WIDTH // LANES) * LANES

FFN_ROWS = 512
PROJ_ROWS = 1024
PROJ_SUB = 256
FF_TILE = MXU_DIM
SB_TILE = 256
SW_TILE = WINDOW
MIX_ROWS = 512
RW_CHUNK = 64
RW_STEP_CHUNKS = 8
RW_GROUP = 4
RW_STAGES_PER_STATE_STEP = 4

VMEM_LIMIT_BYTES = 56 * 1024 * 1024

_MXU = jnp.bfloat16
_F32 = jnp.float32
_HI = lax.Precision.HIGHEST


def _dot(a, b, precision=None):
    return jnp.dot(a, b, preferred_element_type=_F32, precision=precision)


def _dot_nt(a, b, precision=None):
    return lax.dot_general(a, b, (((1,), (1,)), ((), ())),
                           preferred_element_type=_F32, precision=precision)


def _sigmoid(x):
    return 1.0 / (1.0 + jnp.exp(-x))


def _softplus(x):
    return jnp.maximum(x, 0.0) + jnp.log(1.0 + jnp.exp(-jnp.abs(x)))


def _params(*sem):
    return pltpu.CompilerParams(dimension_semantics=sem, vmem_limit_bytes=VMEM_LIMIT_BYTES)


def _const_spec(shape):
    nd = len(shape)
    return pl.BlockSpec(shape, lambda *_: (0,) * nd, pipeline_mode=pl.Buffered(1))


def _layer_spec(shape, layer, row_block=0):
    return pl.BlockSpec((None,) + shape, lambda *_: (layer, row_block, 0),
                        pipeline_mode=pl.Buffered(1))


def _ffn_math(x, g_ref, win_ref, wout_ref, fg_ref, act_ref, final, between=()):
    between = list(between)
    ms = jnp.mean(x * x, axis=-1, keepdims=True)
    hn = (x * lax.rsqrt(ms + NORM_EPS) * g_ref[...]).astype(_MXU)
    for c in range(D_FF // FF_TILE):
        lo, hi = c * FF_TILE, (c + 1) * FF_TILE
        gate = _dot(hn, win_ref[:, lo:hi])
        up = _dot(hn, win_ref[:, D_FF + lo:D_FF + hi])
        act_ref[:, lo:hi] = (gate * _sigmoid(gate) * up).astype(_MXU)
        if between:
            between.pop(0)()
    for thunk in between:
        thunk()
    y = x + 0.5 * _dot(act_ref[...], wout_ref[...])
    if final:
        ms = jnp.mean(y * y, axis=-1, keepdims=True)
        y = y * lax.rsqrt(ms + NORM_EPS) * fg_ref[...]
    return y


def _cast_slab(src_ref, dst_ref):
    dst_ref[...] = src_ref[...].astype(dst_ref.dtype)


def _ffn_body(*refs, n_cast):
    x_ref, g_ref, win_ref, wout_ref, fg_ref = refs[:5]
    src = refs[5:5 + n_cast]
    o_ref = refs[5 + n_cast]
    dst = refs[6 + n_cast:6 + 2 * n_cast]
    act_ref = refs[-1]
    casts = [functools.partial(_cast_slab, s_ref, d_ref) for s_ref, d_ref in zip(src, dst)]
    o_ref[...] = _ffn_math(x_ref[...], g_ref, win_ref, wout_ref, fg_ref, act_ref, False, between=casts)


def _ffn(x, g, w_in, w_out, final_g, cast_next=()):
    m = x.shape[0]
    steps = m // FFN_ROWS
    row = lambda i: (i, 0)
    cast_in, cast_out, cast_shapes = [], [], []
    for w, layer in cast_next:
        rows, cols = w.shape[1:]
        per = next(k for k in (1, 2, 4, 8) if rows % (steps // k) == 0 and (rows * k // steps) % 16 == 0)
        slab = rows * per // steps
        cast_in.append(pl.BlockSpec((None, slab, cols), lambda i, per=per, layer=layer: (layer, i // per, 0)))
        cast_out.append(pl.BlockSpec((slab, cols), lambda i, per=per: (i // per, 0)))
        cast_shapes.append(jax.ShapeDtypeStruct((rows, cols), _MXU))
    out = pl.pallas_call(
        functools.partial(_ffn_body, n_cast=len(cast_next)),
        grid=(steps,),
        in_specs=[
            pl.BlockSpec((FFN_ROWS, D_MODEL), row),
            _const_spec((1, D_MODEL)),
            _const_spec((D_MODEL, 2 * D_FF)),
            _const_spec((D_FF, D_MODEL)),
            _const_spec((1, D_MODEL)),
        ] + cast_in,
        out_specs=[pl.BlockSpec((FFN_ROWS, D_MODEL), row)] + cast_out,
        out_shape=[jax.ShapeDtypeStruct((m, D_MODEL), _F32)] + cast_shapes,
        scratch_shapes=[pltpu.VMEM((FFN_ROWS, D_FF), _MXU)],
        compiler_params=_params("arbitrary"),
        name="ffn",
    )(x, g, w_in, w_out, final_g, *(w for w, _ in cast_next))
    return out


def _ffn_mix_body(sinks_ref, x_ref, sb_ref, rw_ref, swq0_ref, swkv0_ref, swq_ref, swkv_ref, swkvp_ref,
                  wsb_ref, wrw_ref, wsw_ref, g_ref, win_ref, wout_ref, fg_ref,
                  o_ref, act_ref, swo_ref, *, final, seq_tiles, last_tile):
    s = pl.program_id(0)
    t = SW_TILE
    n_sub = swq_ref.shape[0] // t
    ahead = jnp.minimum(s + 1, last_tile)
    consts = _sw_consts()

    def attend(q_ref, kv_ref, kv_before, tile, slot, j):
        rows = slice(j * t, (j + 1) * t)
        before = kv_before if j == 0 else kv_ref[(j - 1) * t:j * t]
        kv = jnp.concatenate([before, kv_ref[rows]], axis=0)
        has_prev = (tile * n_sub + j) % seq_tiles != 0
        swo_ref[slot, rows] = _sw_tile(q_ref[rows], kv[:, :SW_KV_WIDTH], kv[:, SW_KV_WIDTH:],
                                       has_prev, sinks_ref, consts)

    @pl.when(s == 0)
    def _():
        for j in range(n_sub):
            attend(swq0_ref, swkv0_ref, swkv0_ref[0:t], 0, 0, j)

    x = (x_ref[...] + _dot(sb_ref[...], wsb_ref[...]) + _dot(rw_ref[...], wrw_ref[...])
         + _dot(swo_ref[s % 2], wsw_ref[...]))
    o_ref[...] = _ffn_math(
        x, g_ref, win_ref, wout_ref, fg_ref, act_ref, final,
        between=[functools.partial(attend, swq_ref, swkv_ref, swkvp_ref[...], ahead, (s + 1) % 2, j)
                 for j in range(n_sub)])


def _ffn_mix(x, g, w_in, w_out, final_g, final, layer, sb, rw, sw, w_mix, sinks, seq_len):
    m = x.shape[0]
    r = MIX_ROWS
    n = m // r
    cur = lambda s: (s, 0)
    ahead = lambda s: jnp.minimum(s + 1, n - 1)
    kv_block = SW_WIDTH // (2 * SW_KV_WIDTH)
    once = dict(pipeline_mode=pl.Buffered(1))
    return pl.pallas_call(
        functools.partial(_ffn_mix_body, final=final, seq_tiles=seq_len // SW_TILE, last_tile=n - 1),
        grid=(n,),
        in_specs=[
            pl.BlockSpec(memory_space=pltpu.SMEM),
            pl.BlockSpec((r, D_MODEL), cur),
            pl.BlockSpec((r, SB_WIDTH), cur),
            pl.BlockSpec((r, RW_WIDTH), cur),
            pl.BlockSpec((r, SW_WIDTH), lambda s: (0, 0), **once),
            pl.BlockSpec((r, 2 * SW_KV_WIDTH), lambda s: (0, kv_block), **once),
            pl.BlockSpec((r, SW_WIDTH), lambda s: (ahead(s), 0)),
            pl.BlockSpec((r, 2 * SW_KV_WIDTH), lambda s: (ahead(s), kv_block)),
            pl.BlockSpec((SW_TILE, 2 * SW_KV_WIDTH),
                         lambda s: (jnp.maximum(ahead(s) * (r // SW_TILE) - 1, 0), kv_block)),
            _layer_spec((SB_WIDTH, D_MODEL), layer, 0),
            _layer_spec((RW_WIDTH, D_MODEL), layer, SB_WIDTH // RW_WIDTH),
            _layer_spec((SW_WIDTH, D_MODEL), layer, (SB_WIDTH + RW_WIDTH) // SW_WIDTH),
            _const_spec((1, D_MODEL)),
            _const_spec((D_MODEL, 2 * D_FF)),
            _const_spec((D_FF, D_MODEL)),
            _const_spec((1, D_MODEL)),
        ],
        out_specs=pl.BlockSpec((r, D_MODEL), cur),
        out_shape=jax.ShapeDtypeStruct((m, D_MODEL), _F32),
        scratch_shapes=[pltpu.VMEM((r, D_FF), _MXU), pltpu.VMEM((2, r, SW_WIDTH), _MXU)],
        compiler_params=_params("arbitrary"),
        name="ffn_mix",
    )(sinks, x, sb, rw, sw, sw, sw, sw, sw, w_mix, w_mix, w_mix, g, w_in, w_out, final_g)


def _head_block_ones():
    r = lax.broadcasted_iota(jnp.int32, (RW_WIDTH, RW_WIDTH), 0) // HEAD_DIM
    c = lax.broadcasted_iota(jnp.int32, (RW_WIDTH, RW_WIDTH), 1) // HEAD_DIM
    return r == c


def _head_sum(x, mat):
    hi = x.astype(_MXU)
    lo = (x - hi.astype(_F32)).astype(_MXU)
    return _dot(hi, mat) + _dot(lo, mat)


def _inproj_body(x_ref, g_ref, w_ref, wsw_ref, mu_ref, w0_ref, wup_ref, a0_ref, aup_ref, gup_ref,
                 kk_ref, ka_ref, sbo_ref, sw_ref, r_o, kp_o, v_o, kkn_o, b_o, lw_o, g_o,
                 last_ref, kt_scr, v_scr, *, seq_tiles):
    i = pl.program_id(0)
    sub = PROJ_SUB
    assert sub == SB_TILE
    n_sub = x_ref.shape[0] // sub
    c = RW_WIDTH
    ones = jnp.where(_head_block_ones(), 1.0, 0.0).astype(_MXU)
    first = lax.broadcasted_iota(jnp.int32, (sub, 1), 0) == 0
    sb_consts = _sb_consts()
    block_rows = lambda jb: pl.ds(pl.multiple_of(jb * sub, sub), sub)
    blk0 = (i % seq_tiles) * n_sub

    @pl.when(i == 0)
    def _():
        last_ref[...] = jnp.zeros_like(last_ref)
        kt_scr[0] = jnp.zeros((SB_WIDTH, sub), _MXU)
        v_scr[0:sub] = jnp.zeros((sub, SB_WIDTH), _MXU)

    before = jnp.maximum(blk0 - 1, 0)
    kv_before = (kt_scr[before], v_scr[block_rows(before), :])

    def project(j):
        rows = slice(j * sub, (j + 1) * sub)
        x = x_ref[rows]
        ms = jnp.mean(x * x, axis=-1, keepdims=True)
        hn = (x * lax.rsqrt(ms + NORM_EPS) * g_ref[...]).astype(_MXU)
        sb = _dot(hn, w_ref[:, :SB_IN_WIDTH])
        q = sb[:, :SB_WIDTH].astype(_MXU)
        ktb = sb[:, SB_WIDTH:2 * SB_WIDTH].T.astype(_MXU)
        vb = sb[:, 2 * SB_WIDTH:].astype(_MXU)
        kt_scr[blk0 + j] = ktb
        v_scr[block_rows(blk0 + j), :] = vb
        yield
        sw_ref[rows] = _dot(hn, wsw_ref[...]).astype(sw_ref.dtype)
        yield
        return _dot(hn, w_ref[:, SB_IN_WIDTH:]), (q, ktb, vb)

    def attend(qkv, kv_prev, keep):
        q, ktb, vb = qkv
        q_stack = _sb_queries(q, sb_consts)
        zero = jnp.zeros((sub, 1), _F32)
        acc, carries = _sb_block(q_stack, ktb, vb, jnp.zeros((sub, SB_WIDTH), _F32),
                                 (zero,) * SB_HEADS, sb_consts, diagonal=True)
        yield
        acc, carries = _sb_block(q_stack, *kv_prev, acc, carries, sb_consts, keep=keep)
        return q_stack, acc, carries

    def finish(j, q_stack, acc, carries):
        def live(state):
            return (state[0] >= 0) & _sb_alive(state[2])

        def step(state):
            jb, acc, carries = state
            acc, carries = _sb_block(q_stack, kt_scr[jb], v_scr[block_rows(jb), :],
                                     acc, carries, sb_consts)
            return jb - 1, acc, carries

        _, acc, _ = lax.while_loop(live, step, (blk0 + j - 2, acc, carries))
        sbo_ref[j * sub:(j + 1) * sub] = acc.astype(sbo_ref.dtype)

    def prepare(j, p, prev_last):
        rows = slice(j * sub, (j + 1) * sub)
        pprev = jnp.where(first, prev_last, pltpu.roll(p, 1, 0))
        xm = p + (pprev - p) * mu_ref[...]
        r, k, v = xm[:, :c], xm[:, c:2 * c], xm[:, 2 * c:3 * c]
        xwa = xm[:, 3 * c:3 * c + LANES]
        xg = xm[:, 3 * c + LANES:]
        log_w = -_softplus(-(w0_ref[...] + _rw_mm(jnp.tanh(xwa), wup_ref[...]))) - 0.5
        a = _sigmoid(a0_ref[...] + _rw_mm(xwa, aup_ref[...]))
        g = _rw_mm(_sigmoid(xg), gup_ref[...])
        kk = k * kk_ref[...]
        n2 = _head_sum(kk * kk, ones)
        kk = kk * lax.rsqrt(jnp.maximum(n2, 1e-24))
        r_o[rows] = r
        kp_o[rows] = k * (1.0 + (a - 1.0) * ka_ref[...])
        v_o[rows] = v
        kkn_o[rows] = kk
        b_o[rows] = kk * a
        lw_o[rows] = -jnp.exp(log_w)
        g_o[rows] = g

    prev_last = jnp.where(i % seq_tiles == 0, 0.0, last_ref[7:8, :])
    projected = [_advance(project(0))]
    attended = []
    for j in range(n_sub):
        p, qkv = projected[j]
        kv_prev, keep = (kv_before, blk0 > 0) if j == 0 else (projected[j - 1][1][1:], None)
        ahead = project(j + 1) if j + 1 < n_sub else iter(())
        attention = attend(qkv, kv_prev, keep)
        _advance(ahead, 1)
        _advance(attention, 1)
        _advance(ahead, 1)
        prepare(j, p, prev_last)
        prev_last = p[sub - 1:sub, :]
        if j + 1 < n_sub:
            projected.append(_advance(ahead))
        attended.append(_advance(attention))
    last_ref[...] = projected[-1][0][sub - 8:, :]
    for j in range(n_sub):
        finish(j, *attended[j])


def _inproj(x, g, w, w_sw, mu, w0, wup, a0, aup, gup, k_k, k_a, seq_len, layer):
    assert DECAY_LORA + AAA_LORA == LANES
    m = x.shape[0]
    n = PROJ_ROWS
    row = lambda i: (i, 0)
    vec = _const_spec((1, RW_WIDTH))
    rw_out = pl.BlockSpec((n, RW_WIDTH), row)
    return pl.pallas_call(
        functools.partial(_inproj_body, seq_tiles=seq_len // n),
        grid=(m // n,),
        in_specs=[
            pl.BlockSpec((n, D_MODEL), row),
            _const_spec((1, D_MODEL)),
            _layer_spec((D_MODEL, SB_IN_WIDTH + RW_SPAN), layer),
            _layer_spec((D_MODEL, SW_IN_WIDTH), layer),
            _const_spec((1, RW_SPAN)),
            vec, _const_spec((LANES, RW_WIDTH)),
            vec, _const_spec((LANES, RW_WIDTH)),
            _const_spec((RW_SPAN - 3 * RW_WIDTH - LANES, RW_WIDTH)),
            vec, vec,
        ],
        out_specs=[pl.BlockSpec((n, SB_WIDTH), row), pl.BlockSpec((n, SW_IN_WIDTH), row)]
        + [rw_out] * 7,
        out_shape=[jax.ShapeDtypeStruct((m, SB_WIDTH), _MXU),
                   jax.ShapeDtypeStruct((m, SW_IN_WIDTH), _MXU)]
        + [jax.ShapeDtypeStruct((m, RW_WIDTH), _F32)] * 7,
        scratch_shapes=[pltpu.VMEM((8, RW_SPAN), _F32),
                        pltpu.VMEM((seq_len // SB_TILE, SB_WIDTH, SB_TILE), _MXU),
                        pltpu.VMEM((seq_len, SB_WIDTH), _MXU)],
        compiler_params=_params("arbitrary"),
        name="inproj",
    )(x, g, w, w_sw, mu, w0, wup, a0, aup, gup, k_k, k_a)


SB_SKIP_LOG = -88.0


def _sb_consts():
    t = SB_TILE
    lane_head = lax.broadcasted_iota(jnp.int32, (1, SB_WIDTH), 1) // HEAD_DIM
    row = lax.broadcasted_iota(jnp.int32, (t, t), 0)
    col = lax.broadcasted_iota(jnp.int32, (t, t), 1)
    later = jnp.where(row > col, 1.0, 0.0).astype(_MXU)
    causal = col < row
    return lane_head, later, causal


def _sb_queries(q, consts):
    lane_head = consts[0]
    q = q * (HEAD_DIM ** -0.5)
    return jnp.concatenate(
        [jnp.where(lane_head == h, q, jnp.zeros_like(q)) for h in range(SB_HEADS)], axis=0)


def _sb_block(q_stack, ktb, vb, acc, carries, consts, diagonal=False, keep=None):
    t = SB_TILE
    lane_head, later, causal = consts
    heads = range(SB_HEADS)
    head_rows = [slice(h * t, (h + 1) * t) for h in heads]
    zz = _dot(q_stack, ktb)
    zs = [zz[rows] for rows in head_rows]
    loms = [-_softplus(z) for z in zs]
    if diagonal:
        loms = [jnp.where(causal, lom, 0.0) for lom in loms]
    tails = _dot(jnp.concatenate([lom.astype(_MXU) for lom in loms], axis=0), later)
    weights, new_carries = [], []
    for h in heads:
        tail = tails[head_rows[h]] + carries[h]
        a = jnp.exp(zs[h] + loms[h] + tail)
        if diagonal:
            a = jnp.where(causal, a, 0.0)
        if keep is not None:
            a = jnp.where(keep, a, 0.0)
        weights.append(a.astype(_MXU))
        carry = carries[h] + jnp.sum(loms[h], axis=1, keepdims=True)
        new_carries.append(carry if keep is None else jnp.where(keep, carry, carries[h]))
    v_heads = jnp.concatenate(
        [jnp.where(lane_head == h, vb, jnp.zeros_like(vb)) for h in heads], axis=0)
    acc = acc + _dot(jnp.concatenate(weights, axis=1), v_heads)
    return acc, tuple(new_carries)


def _sb_alive(carries):
    return jnp.max(functools.reduce(jnp.maximum, carries)) > SB_SKIP_LOG


def _sw_consts():
    t = SW_TILE
    assert t == WINDOW
    r_i = lax.broadcasted_iota(jnp.int32, (t, t), 0)
    c_i = lax.broadcasted_iota(jnp.int32, (t, t), 1)
    own = c_i <= r_i
    distf = jnp.where(own, r_i - c_i, r_i - c_i + t).astype(_F32)
    lane_kv = lax.broadcasted_iota(jnp.int32, (1, LANES), 1) // HEAD_DIM
    return own, distf, lane_kv


def _sw_tile(q, kj, vj, has_prev, sinks_ref, consts):
    t = SW_TILE
    own, distf, lane_kv = consts
    valid = own | has_prev
    grp = SW_HEADS // SW_KV_HEADS
    pairs = [(g, kv) for g in range(grp) for kv in range(SW_KV_HEADS)]
    head_rows = [slice(n * t, (n + 1) * t) for n in range(len(pairs))]
    q = q * (HEAD_DIM ** -0.5)
    q_stack = []
    for g, kv in pairs:
        qp = q[:, g * LANES:(g + 1) * LANES]
        q_stack.append(jnp.where(lane_kv == kv, qp, jnp.zeros_like(qp)))
    zz = _dot_nt(jnp.concatenate(q_stack, axis=0), kj)
    probs, denoms = [], []
    for n, (g, kv) in enumerate(pairs):
        head = kv * grp + g
        slope = 2.0 ** (-8.0 * (head + 1.0) / SW_HEADS)
        zh = zz[head_rows[n]]
        z = jnp.where(own, zh[:, t:], zh[:, :t]) - slope * distf
        z = jnp.where(valid, z, -1e30)
        sink = sinks_ref[head]
        m = jnp.maximum(jnp.max(z, axis=1, keepdims=True), sink)
        p = jnp.exp(z - m)
        denoms.append(jnp.sum(p, axis=1, keepdims=True) + jnp.exp(sink - m))
        p = p.astype(_MXU)
        zero = jnp.zeros_like(p)
        probs.append(jnp.concatenate([jnp.where(own, zero, p), jnp.where(own, p, zero)], axis=1))
    oo = _dot(jnp.concatenate(probs, axis=0), vj)
    out = []
    for g in range(grp):
        first, second = (pairs.index((g, kv)) for kv in range(SW_KV_HEADS))
        out.append(jnp.where(lane_kv == 0, oo[head_rows[first]] / denoms[first],
                             oo[head_rows[second]] / denoms[second]))
    return jnp.concatenate(out, axis=1).astype(_MXU)


def _rw_mm(a, b):
    return _dot(a.astype(_MXU), b.astype(_MXU))


def _rw_mm_nt(a, b):
    return _dot_nt(a.astype(_MXU), b.astype(_MXU))


def _head_stack(x, lane_head):
    return jnp.concatenate(
        [jnp.where(lane_head == h, x, 0.0) for h in range(RW_HEADS)], axis=0)


def _rw_chunk_ops(r, kp, v, kk, b, lw):
    cl = RW_CHUNK
    chunks = range(len(r))
    each = lambda f, *xs: [f(*a) for a in zip(*xs)]

    lane_head = lax.broadcasted_iota(jnp.int32, (1, RW_WIDTH), 1) // HEAD_DIM
    tr = lax.broadcasted_iota(jnp.int32, (cl, cl), 0)
    tc = lax.broadcasted_iota(jnp.int32, (cl, cl), 1)
    lower = jnp.where(tr >= tc, 1.0, 0.0)
    cum = each(lambda x: _dot(lower, x, _HI), lw)
    yield
    cum_end = each(lambda x: x[cl - 1:cl, :], cum)
    kk_t = each(lambda x, c, l: x * jnp.exp(c - l), kk, cum, lw)
    r_t = each(lambda x, c: x * jnp.exp(c), r, cum)
    inv = each(lambda c: jnp.exp(-c), cum)
    k_h = each(jnp.multiply, kp, inv)
    b_h = each(jnp.multiply, b, inv)
    yield
    to_end = each(lambda e, c: jnp.exp(e - c), cum_end, cum)
    gt = each(lambda k, bb, e: jnp.concatenate([k * e, -(bb * e)], axis=0).T.astype(_MXU),
              kp, b, to_end)
    pct = each(lambda e: jnp.broadcast_to(jnp.exp(e), (2 * cl, RW_WIDTH)).T, cum_end)
    yield

    stack = lambda x: _head_stack(x, lane_head)
    khs, bhs, vs = each(stack, k_h), each(stack, b_h), each(stack, v)
    t_i = lax.broadcasted_iota(jnp.int32, (cl, RW_WIDTH), 0)
    s_i = lax.broadcasted_iota(jnp.int32, (cl, RW_WIDTH), 1) % cl
    strict, incl = t_i > s_i, t_i >= s_i
    lhs = each(lambda x, y: jnp.concatenate([x, y], axis=0), kk_t, r_t)
    pk = each(_rw_mm_nt, lhs, khs)
    yield
    pb = each(_rw_mm_nt, lhs, bhs)
    yield
    a_b = each(lambda p: jnp.where(strict, p[:cl], 0.0), pb)
    m_b = each(lambda p: jnp.where(incl, p[cl:], 0.0).astype(_MXU), pb)
    akm = each(lambda p: jnp.concatenate(
        [jnp.where(strict, p[:cl], 0.0), jnp.where(incl, p[cl:], 0.0)], axis=0), pk)
    kv = each(_rw_mm, akm, vs)
    yield

    same_head = _head_block_ones()

    def block_diag(x_cat):
        return jnp.where(same_head, jnp.concatenate([x_cat] * RW_HEADS, axis=0), 0.0)

    eye = jnp.where(t_i == s_i, 1.0, 0.0)
    pair = ((t_i ^ s_i) == 1) & ((t_i & 1) != 0)
    t_cat = each(lambda a: eye - jnp.where(pair, a, 0.0), a_b)
    m = 2
    while m < cl:
        below = (((t_i ^ s_i) & -(2 * m)) == 0) & ((t_i & m) != 0) & ((s_i & m) == 0)
        x = each(lambda t, a: _rw_mm(t, block_diag(jnp.where(below, a, 0.0))), t_cat, a_b)
        yield
        t_cat = each(lambda t, xx: t - _rw_mm(xx, block_diag(t)), t_cat, x)
        yield
        m *= 2
    w = each(lambda t, k, p: _rw_mm(t, jnp.concatenate([stack(k), stack(p[:cl])], axis=1)),
             t_cat, kk_t, kv)
    return [(jnp.concatenate([w[c][:, :RW_WIDTH], r_t[c]], axis=0).astype(_MXU),
             w[c][:, RW_WIDTH:], kv[c][cl:], m_b[c], gt[c], pct[c]) for c in chunks]


def _advance(staged, stages=None):
    try:
        while stages is None or stages > 0:
            next(staged)
            stages = None if stages is None else stages - 1
    except StopIteration as done:
        return done.value
    return None


def _rw_core_body(r_ref, kp_ref, v_ref, kk_ref, b_ref, lw_ref, g_ref, gng_ref, gnb_ref, rk_ref,
                  o_ref, h_ref, y_ref):
    cl = RW_CHUNK

    @pl.when(pl.program_id(0) == 0)
    def _():
        h_ref[...] = jnp.zeros_like(h_ref)

    lane_head = lax.broadcasted_iota(jnp.int32, (1, RW_WIDTH), 1) // HEAD_DIM
    same_head = _head_block_ones()
    batch = range(h_ref.shape[0])
    rows = lambda c: slice(c * cl, (c + 1) * cl)
    groups = [range(g, g + RW_GROUP) for g in range(0, RW_STEP_CHUNKS, RW_GROUP)]

    def staged_ops(group):
        load = lambda ref: [ref[bi, rows(c)] for c in group for bi in batch]
        return _rw_chunk_ops(*(load(x) for x in (r_ref, kp_ref, v_ref, kk_ref, b_ref, lw_ref)))

    def state_step(c, ops):
        h = [h_ref[bi] for bi in batch]
        uy = [_dot(ops[bi][0], h[bi].astype(_MXU)) for bi in batch]
        u = [uy[bi][:cl] + ops[bi][1] for bi in batch]
        vu = [jnp.concatenate([v_ref[bi, rows(c)], u[bi]], axis=0) for bi in batch]
        upd = [_dot(ops[bi][4], vu[bi].astype(_MXU)) for bi in batch]
        for bi in batch:
            pct = ops[bi][5]
            h_ref[bi] = (jnp.concatenate([pct, pct], axis=1) * h[bi]
                         + jnp.where(same_head, upd[bi], 0.0))
        for bi in batch:
            y_ref[bi, rows(c)] = (uy[bi][cl:] + ops[bi][2]
                                  - _rw_mm(ops[bi][3], _head_stack(u[bi], lane_head)))

    ops = _advance(staged_ops(groups[0]))
    for gi, group in enumerate(groups):
        following = staged_ops(groups[gi + 1]) if gi + 1 < len(groups) else None
        ready = None
        for k, c in enumerate(group):
            if following is not None and ready is None:
                ready = _advance(following, RW_STAGES_PER_STATE_STEP)
            state_step(c, ops[k * len(batch):(k + 1) * len(batch)])
        if following is not None:
            ops = ready if ready is not None else _advance(following)

    mean_mat = jnp.where(same_head, 1.0 / HEAD_DIM, 0.0).astype(_MXU)
    ones = jnp.where(same_head, 1.0, 0.0).astype(_MXU)
    for bi in batch:
        y = y_ref[bi]
        d = y - _head_sum(y, mean_mat)
        var = _head_sum(d * d, mean_mat)
        yn = d * lax.rsqrt(var + RW_GN_EPS) * gng_ref[...] + gnb_ref[...]
        bonus = _head_sum(r_ref[bi] * kp_ref[bi] * rk_ref[...], ones) * v_ref[bi]
        o_ref[bi] = ((yn + bonus) * g_ref[bi]).astype(o_ref.dtype)


def _rwkv7(r, kp, v, kk, b, lw, g, r_k, gn_g, gn_b):
    bsz, s, _ = v.shape
    n = RW_STEP_CHUNKS * RW_CHUNK
    blk = pl.BlockSpec((bsz, n, RW_WIDTH), lambda i: (0, i, 0))
    vec = _const_spec((1, RW_WIDTH))
    return pl.pallas_call(
        _rw_core_body,
        grid=(s // n,),
        in_specs=[blk] * 7 + [vec] * 3,
        out_specs=blk,
        out_shape=jax.ShapeDtypeStruct((bsz, s, RW_WIDTH), _MXU),
        scratch_shapes=[pltpu.VMEM((bsz, RW_WIDTH, RW_WIDTH), _F32),
                        pltpu.VMEM((bsz, n, RW_WIDTH), _F32)],
        compiler_params=_params("arbitrary"),
        name="rw_core",
    )(r, kp, v, kk, b, lw, g, gn_g, gn_b, r_k)


_SW_PAIR_ORDER = tuple(h for g in range(SW_HEADS // SW_KV_HEADS)
                       for h in (g, g + SW_HEADS // SW_KV_HEADS))


def _pad_rows(w, before, total):
    return jnp.pad(w, ((before, total - before - w.shape[0]), (0, 0)))


def _pair_heads(w, axis):
    head = lambda h: lax.slice_in_dim(w, h * HEAD_DIM, (h + 1) * HEAD_DIM, axis=axis)
    return jnp.concatenate([head(h) for h in _SW_PAIR_ORDER], axis=axis)


def _mix_in_layout(w_in, mu):
    rw_end = SB_IN_WIDTH + RW_IN_WIDTH
    w = w_in.astype(_MXU)
    w_sw = jnp.concatenate([_pair_heads(w[..., rw_end:rw_end + SW_WIDTH], 2),
                            w[..., rw_end + SW_WIDTH:]], axis=-1)
    return w, w_sw, jnp.pad(mu, ((0, 0), (0, RW_SPAN - RW_IN_WIDTH)))[:, None, :]


def kernel(x, ffn1_norm, ffn1_w_in, ffn1_w_out, mix_norm, mix_w_in, mix_w_out, rw_mu, rw_w0,
           rw_w_up, rw_a0, rw_a_up, rw_g_up, rw_k_k, rw_k_a, rw_r_k, rw_gn_g, rw_gn_b, sw_sinks,
           ffn2_norm, ffn2_w_in, ffn2_w_out, final_norm):
    bsz, s, d = x.shape
    xf = x.reshape(bsz * s, d)
    row = lambda t: t[None, :]
    final_g = row(final_norm)
    w_mix_in, w_mix_sw, mu_all = _mix_in_layout(mix_w_in, rw_mu)
    gate_rows = RW_SPAN - 3 * RW_WIDTH - LANES
    mix_rows = SB_WIDTH + RW_WIDTH
    w_mix_out = jnp.concatenate(
        [mix_w_out[:, :mix_rows], _pair_heads(mix_w_out[:, mix_rows:], 1)], axis=1).astype(_MXU)
    ffn_f32 = (ffn1_w_in, ffn1_w_out, ffn2_w_in, ffn2_w_out)
    w1_in, w1_out, w2_in, w2_out = (w[0].astype(_MXU) for w in ffn_f32)
    for l in range(DEPTH):
        cast_next = [(w, l + 1) for w in ffn_f32] if l + 1 < DEPTH else []
        xf, *ffn_next = _ffn(xf, row(ffn1_norm[l]), w1_in, w1_out, final_g, cast_next)
        sb_out, sw, *rw_parts = _inproj(
            xf, row(mix_norm[l]), w_mix_in, w_mix_sw, mu_all[l],
            row(rw_w0[l]), _pad_rows(rw_w_up[l], 0, LANES),
            row(rw_a0[l]), _pad_rows(rw_a_up[l], DECAY_LORA, LANES),
            _pad_rows(rw_g_up[l], 0, gate_rows), row(rw_k_k[l]), row(rw_k_a[l]), s, l)
        rw_out = _rwkv7(
            *(t.reshape(bsz, s, RW_WIDTH) for t in rw_parts),
            rw_r_k[l].reshape(1, RW_WIDTH), row(rw_gn_g[l]), row(rw_gn_b[l]))
        xf = _ffn_mix(xf, row(ffn2_norm[l]), w2_in, w2_out, final_g, l == DEPTH - 1, l,
                      sb_out, rw_out.reshape(bsz * s, RW_WIDTH), sw, w_mix_out, sw_sinks[l], s)
        if ffn_next:
            w1_in, w1_out, w2_in, w2_out = ffn_next
    return xf.reshape(bsz, s, d)
```
